```python
import math
import jax
import jax.numpy as jnp
from jax import lax
import numpy as np

D_MODEL = 1024
BATCH = 32
SEQ = 256
DEPTH = 2
DEC_BATCH = 4
DEC_SEQ = 4096
PAST_LEN = 256

GRID_W = 64
CHUNK = 128
N_EVEN = (DEPTH + 1) // 2
N_ODD = DEPTH // 2
SGU_WIDTH = D_MODEL // 2
SGU_GROUPS = 4
SGU_GW = SGU_WIDTH // SGU_GROUPS
FNET_WIDTH = D_MODEL - SGU_WIDTH
FNET_GROUPS = 4
FNET_GW = FNET_WIDTH // FNET_GROUPS
IN0_WIDTH = 2 * SGU_WIDTH + FNET_WIDTH
N_HEADS = 8
HEAD_DIM = D_MODEL // N_HEADS // 2
V_DIM = 2 * HEAD_DIM
QBLK = 128
ROPE_THETA = 10000.0
D_FF = 2816
N_EXPERTS = 8
TOP_K = 2
D_FF_EXPERT = 3584
EPS = 1e-6
F32 = jnp.float32

kernel_name = 'hybrid_diffusion_prefix_step'


def lambda_init(l):
    return 0.8 - 0.6 * math.exp(-0.3 * l)


def rmsnorm(x, g):
    xf = x.astype(F32)
    y = xf * lax.rsqrt(jnp.mean(xf * xf, axis=-1, keepdims=True) + EPS)
    return (y * g.astype(F32)).astype(x.dtype)


def adaln(cvec, w, b):
    m = jax.nn.silu(cvec) @ w + b
    return [t[:, None, :] for t in jnp.split(m, 6, axis=-1)]


def modulate(x, g, shift, scale):
    return rmsnorm(x, g) * (1 + scale) + shift


def swiglu(h, w1, w3, w2):
    return (jax.nn.silu(h @ w1) * (h @ w3)) @ w2


def moe_swiglu(h, router_w, w1, w3, w2):
    logits = (h @ router_w).astype(F32)
    top_v, top_i = lax.top_k(logits, TOP_K)
    gates = jax.nn.softmax(top_v, axis=-1)
    gate_full = jnp.sum(jax.nn.one_hot(top_i, N_EXPERTS, dtype=F32) * gates[..., None], axis=-2)
    gate_full = gate_full.astype(h.dtype)
    out = jnp.zeros_like(h)
    for e in range(N_EXPERTS):
        out = out + gate_full[..., e:e + 1] * swiglu(h, w1[e], w3[e], w2[e])
    return out


def spatial_fourier_mix(h, w_in, sgu_g, sgu_w, sgu_b, w_out):
    B, S, _ = h.shape
    p = h @ w_in
    a = jax.nn.gelu(p[..., :2 * SGU_WIDTH])
    u, v = a[..., :SGU_WIDTH], a[..., SGU_WIDTH:]
    v = rmsnorm(v.reshape(B, S // CHUNK, CHUNK, SGU_GROUPS, SGU_GW), sgu_g)
    v = jnp.einsum('gpq,bnqgc->bnpgc', sgu_w, v) + sgu_b.T[:, :, None]
    a_out = u * v.reshape(B, S, SGU_WIDTH)
    f = p[..., 2 * SGU_WIDTH:].reshape(B, S, FNET_GROUPS, FNET_GW).astype(F32)
    f = jnp.real(jnp.fft.fft2(f, axes=(1, 3), norm='ortho'))
    f_out = f.astype(h.dtype).reshape(B, S, FNET_WIDTH)
    return jnp.concatenate([a_out, f_out], axis=-1) @ w_out


def axial_rope(x, row, col):
    half = HEAD_DIM // 2
    inv_freq = ROPE_THETA ** (-jnp.arange(0, half, 2, dtype=F32) / half)

    def rot(xp, pos):
        ang = pos.astype(F32)[:, None] * inv_freq[None, :]
        cos = jnp.cos(ang)[None, :, None, None, :].astype(x.dtype)
        sin = jnp.sin(ang)[None, :, None, None, :].astype(x.dtype)
        x1, x2 = xp[..., :half // 2], xp[..., half // 2:]
        return jnp.concatenate([x1 * cos - x2 * sin, x2 * cos + x1 * sin], axis=-1)

    return jnp.concatenate([rot(x[..., :half], row), rot(x[..., half:], col)], axis=-1)


def diff_qkv(h, w_qkv):
    B, S, _ = h.shape
    q, k, v = jnp.split(h @ w_qkv, 3, axis=-1)
    return (q.reshape(B, S, N_HEADS, 2, HEAD_DIM),
            k.reshape(B, S, N_HEADS, 2, HEAD_DIM),
            v.reshape(B, S, N_HEADS, V_DIM))


def diff_lambda(lq1, lk1, lq2, lk2, lam0):
    return (jnp.exp(jnp.sum(lq1.astype(F32) * lk1.astype(F32)))
            - jnp.exp(jnp.sum(lq2.astype(F32) * lk2.astype(F32))) + lam0)


def diff_attend(q, k, v, lam):
    B, S, H, _, dh = q.shape
    nb = S // QBLK
    qb = q.reshape(B, nb, QBLK, H, 2, dh).transpose(1, 0, 2, 3, 4, 5)
    scale = dh ** -0.5

    def one(qblk):
        s = jnp.einsum('bqhcd,bkhcd->bhcqk', qblk, k).astype(F32) * scale
        p = jax.nn.softmax(s, axis=-1)
        w = p[:, :, 0] - lam * p[:, :, 1]
        return jnp.einsum('bhqk,bkhe->bqhe', w.astype(v.dtype), v)

    o = lax.map(one, qb)
    return o.transpose(1, 0, 2, 3, 4).reshape(B, S, H, V_DIM)


def diff_out(o, g, lam0, w_o):
    B, S = o.shape[:2]
    o = rmsnorm(o, g) * (1.0 - lam0)
    return o.reshape(B, S, N_HEADS * V_DIM) @ w_o


def setup_inputs(seed: int = 0) -> dict:
    key = jax.random.key(seed)
    ks = iter(jax.random.split(key, 40))
    D = D_MODEL

    def nrm(shape, scale):
        return jax.random.normal(next(ks), shape, F32) * scale

    def gain(shape):
        return 1.0 + nrm(shape, 0.01)

    return {
        'x_prompt': nrm((BATCH, SEQ, D), 1.0),
        'x_sample': nrm((DEC_BATCH, DEC_SEQ, D), 1.0),
        'cache_k': nrm((DEC_BATCH, N_ODD, PAST_LEN, N_HEADS, 2 * HEAD_DIM), 1.0),
        'cache_v': nrm((DEC_BATCH, N_ODD, PAST_LEN, N_HEADS, V_DIM), 1.0),
        'c': nrm((DEC_BATCH, D), 1.0),
        'c_ctx': nrm((D,), 1.0),
        'ada_w': nrm((DEPTH, D, 6 * D), 0.5 * D ** -0.5),
        'ada_b': nrm((DEPTH, 6 * D), 0.01),
        'norm1_g': gain((DEPTH, D)),
        'norm2_g': gain((DEPTH, D)),
        'final_g': gain((D,)),
        'mix_w_in': nrm((N_EVEN, D, IN0_WIDTH), D ** -0.5),
        'sgu_g': gain((N_EVEN, SGU_GROUPS, SGU_GW)),
        'sgu_w': nrm((N_EVEN, SGU_GROUPS, CHUNK, CHUNK), CHUNK ** -0.5),
        'sgu_b': gain((N_EVEN, SGU_GROUPS, CHUNK)),
        'mix_w_out': nrm((N_EVEN, SGU_WIDTH + FNET_WIDTH, D), D ** -0.5),
        'ffn_w1': nrm((N_EVEN, D, D_FF), D ** -0.5),
        'ffn_w3': nrm((N_EVEN, D, D_FF), D ** -0.5),
        'ffn_w2': nrm((N_EVEN, D_FF, D), D_FF ** -0.5),
        'attn_w_qkv': nrm((N_ODD, D, 3 * D), D ** -0.5),
        'lam_q1': nrm((N_ODD, HEAD_DIM), 0.1),
        'lam_k1': nrm((N_ODD, HEAD_DIM), 0.1),
        'lam_q2': nrm((N_ODD, HEAD_DIM), 0.1),
        'lam_k2': nrm((N_ODD, HEAD_DIM), 0.1),
        'subln_g': gain((N_ODD, V_DIM)),
        'attn_w_o': nrm((N_ODD, N_HEADS * V_DIM, D), D ** -0.5),
        'router_w': nrm((N_ODD, D, N_EXPERTS), D ** -0.5),
        'moe_w1': nrm((N_ODD, N_EXPERTS, D, D_FF_EXPERT), D ** -0.5),
        'moe_w3': nrm((N_ODD, N_EXPERTS, D, D_FF_EXPERT), D ** -0.5),
        'moe_w2': nrm((N_ODD, N_EXPERTS, D_FF_EXPERT, D), D_FF_EXPERT ** -0.5),
    }


def reference(x_prompt, x_sample, cache_k, cache_v, c, c_ctx, ada_w, ada_b, norm1_g, norm2_g,
              final_g, mix_w_in, sgu_g, sgu_w, sgu_b, mix_w_out, ffn_w1, ffn_w3, ffn_w2,
              attn_w_qkv, lam_q1, lam_k1, lam_q2, lam_k2, subln_g, attn_w_o, router_w,
              moe_w1, moe_w3, moe_w2):
    n_lat = x_sample.shape[1]
    rows = n_lat // GRID_W
    row = jnp.repeat(jnp.arange(rows), GRID_W)
    col = jnp.tile(jnp.arange(GRID_W), rows)
    xp, xs = x_prompt, x_sample
    new_k, new_v = [], []
    for l in range(DEPTH):
        i = l // 2
        mp = adaln(c_ctx[None, :], ada_w[l], ada_b[l])
        ms = adaln(c, ada_w[l], ada_b[l])
        if l % 2 == 0:
            def mixer(h):
                return spatial_fourier_mix(h, mix_w_in[i], sgu_g[i], sgu_w[i], sgu_b[i], mix_w_out[i])

            def ffn(h):
                return swiglu(h, ffn_w1[i], ffn_w3[i], ffn_w2[i])

            xp = xp + mp[2] * mixer(modulate(xp, norm1_g[l], mp[0], mp[1]))
            xs = xs + ms[2] * mixer(modulate(xs, norm1_g[l], ms[0], ms[1]))
        else:
            lam0 = lambda_init(l)
            lam = diff_lambda(lam_q1[i], lam_k1[i], lam_q2[i], lam_k2[i], lam0)
            q, k, v = diff_qkv(modulate(xp, norm1_g[l], mp[0], mp[1]), attn_w_qkv[i])
            xp = xp + mp[2] * diff_out(diff_attend(q, k, v, lam), subln_g[i], lam0, attn_w_o[i])
            new_k.append(k.reshape(k.shape[:3] + (2 * HEAD_DIM,)))
            new_v.append(v)
            q, k, v = diff_qkv(modulate(xs, norm1_g[l], ms[0], ms[1]), attn_w_qkv[i])
            q = axial_rope(q, row, col)
            k = axial_rope(k, row, col)
            ck = cache_k[:, i]
            ck = ck.reshape(ck.shape[:3] + (2, HEAD_DIM)).astype(k.dtype)
            k_all = jnp.concatenate([ck, k], axis=1)
            v_all = jnp.concatenate([cache_v[:, i].astype(v.dtype), v], axis=1)
            xs = xs + ms[2] * diff_out(diff_attend(q, k_all, v_all, lam), subln_g[i], lam0, attn_w_o[i])

            def ffn(h):
                return moe_swiglu(h, router_w[i], moe_w1[i], moe_w3[i], moe_w2[i])

        xp = xp + mp[5] * ffn(modulate(xp, norm2_g[l], mp[3], mp[4]))
        xs = xs + ms[5] * ffn(modulate(xs, norm2_g[l], ms[3], ms[4]))
    y_prompt = rmsnorm(xp, final_g)
    y_sample = rmsnorm(xs, final_g)
    new_cache_k = jnp.stack(new_k, axis=1)
    new_cache_v = jnp.stack(new_v, axis=1)
    return (y_prompt, y_sample, new_cache_k, new_cache_v)
```

```python
import functools
import math

import numpy as np
import jax
import jax.numpy as jnp
from jax import lax
from jax.experimental import pallas as pl
from jax.experimental.pallas import tpu as pltpu

F32 = jnp.float32
BF16 = jnp.bfloat16
I32 = jnp.int32

D = 1024
N_CTX_SEQ = 32
CTX_LEN = 256
N_LAT_SEQ = 4
LAT_LEN = 4096
NP = N_CTX_SEQ * CTX_LEN
NS = N_LAT_SEQ * LAT_LEN
T = NP + NS
GRID_W = 64
CHUNK = 128
SGU_W = 512
FNET_W = 512
GW = 128
N_HEADS = 8
HEAD_DIM = 64
V_DIM = 128
ROPE_THETA = 10000.0
D_FF = 2816
N_EXPERTS = 8
D_FF_EXPERT = 3584
EPS = 1e-6
LAM0 = 0.8 - 0.6 * math.exp(-0.3 * 1)
CTX_ROW = 4

LANES = 128
TM = 512
TQ = 512
TK = 1024
Q_SCALE = HEAD_DIM ** -0.5 * math.log2(math.e)
T_MOE = 512
F_MOE = 1792
N_F_MOE = D_FF_EXPERT // F_MOE
SEG_ALIGN = 8
SEG_BITS = (64, 32, 16, 8, 4, 2, 1)
L_TILE = 1152
N_ROWS = -(-(2 * T + (T // TM) * N_EXPERTS * (SEG_ALIGN - 1) + N_EXPERTS * (T_MOE - SEG_ALIGN)) // T_MOE) * T_MOE
N_MOE_TILES = N_ROWS // T_MOE
MAX_UNUSED_TILES = N_MOE_TILES - 2 * T // T_MOE
FFN_CHUNK = 1408
DFT_N1 = 256
DFT_N2 = 16
VMEM_LIMIT = 56 * 2 ** 20


def _params(sem, vmem=VMEM_LIMIT):
    return pltpu.CompilerParams(dimension_semantics=sem, vmem_limit_bytes=vmem)


def _resident(shape):
    nd = len(shape)
    return pl.BlockSpec(shape, lambda *_: (0,) * nd, pipeline_mode=pl.Buffered(1))


def _mod_row(i, tm):
    npt = NP // tm
    return jnp.where(i < npt, CTX_ROW, (i - npt) // (LAT_LEN // tm))


def _mod_spec(layer, tm, offset_tiles=0):
    return pl.BlockSpec((None, None, 6, D), lambda i, *_: (layer, _mod_row(i + offset_tiles, tm), 0, 0))


def _two_stream_specs(tm, width):
    npt = NP // tm
    ctx = pl.BlockSpec((tm, width), lambda i: (jnp.minimum(i, npt - 1), 0))
    lat = pl.BlockSpec((tm, width), lambda i: (jnp.maximum(i - npt, 0), 0))
    return ctx, lat


def _pick_stream(tm, ctx_ref, lat_ref):
    return jnp.where(pl.program_id(0) < NP // tm, ctx_ref[...], lat_ref[...])


def _modulate(x, g, scale, shift):
    ms = jnp.mean(x * x, axis=-1, keepdims=True)
    return x * lax.rsqrt(ms + EPS) * (g * (1.0 + scale)) + shift


def _silu(a):
    return a * jax.nn.sigmoid(a)


def _gelu_tanh(x):
    return 0.5 * x * (1.0 + jnp.tanh(0.7978845608028654 * (x + 0.044715 * (x * x * x))))


@functools.lru_cache(maxsize=None)
def _tables():
    def cs(n):
        k = np.arange(n, dtype=np.int64)
        ang = 2.0 * np.pi * ((k[:, None] * k[None, :]) % n) / n
        return np.cos(ang), np.sin(ang)

    c128, s128 = cs(GW)
    dft_ch = np.concatenate([c128, s128], axis=1)
    c256, s256 = cs(CTX_LEN)
    dft_ctx = np.concatenate([c256, -s256], axis=1)
    k1 = np.arange(DFT_N1, dtype=np.int64)[:, None]
    n2 = np.arange(DFT_N2, dtype=np.int64)[None, :]
    tw = 2.0 * np.pi * ((k1 * n2) % LAT_LEN) / LAT_LEN
    c16, s16 = cs(DFT_N2)
    half = HEAD_DIM // 2
    inv_freq = ROPE_THETA ** (-np.arange(0, half, 2, dtype=np.float64) / half)
    t = np.arange(LAT_LEN)
    ang_r = (t // GRID_W).astype(np.float64)[:, None] * inv_freq[None, :]
    ang_c = (t % GRID_W).astype(np.float64)[:, None] * inv_freq[None, :]

    def blk(ang):
        c = np.cos(ang)
        s = np.sin(ang)
        return np.concatenate([c, c], axis=1), np.concatenate([-s, s], axis=1)

    cr, sr = blk(ang_r)
    cc, sc = blk(ang_c)
    cos64 = np.concatenate([cr, cc], axis=1)
    sin64 = np.concatenate([sr, sc], axis=1)
    rope_cos = np.concatenate([cos64, cos64], axis=1).astype(np.float32)
    rope_sin = np.concatenate([sin64, sin64], axis=1).astype(np.float32)
    tri = np.tril(np.ones((TM, TM), np.float32), k=-1)
    below = np.triu(np.ones((LANES, LANES), np.float32), k=1)
    f32 = lambda a: np.asarray(a, np.float32)
    return dict(dft_ch=f32(dft_ch), dft_ctx=f32(dft_ctx), c256=f32(c256), s256=f32(s256),
                tw_cos=f32(np.cos(tw)), tw_sin=f32(np.sin(tw)), c16=f32(c16), s16=f32(s16),
                rope_cos=rope_cos, rope_sin=rope_sin, tri=tri, below=below)


def _device_tables():
    tb = {k: jnp.asarray(v) for k, v in _tables().items()}
    for k in ("dft_ch", "dft_ctx", "c256", "s256", "tri", "below"):
        tb[k] = tb[k].astype(BF16)
    return tb


def _adaln_kernel(c_ref, w_ref, b_ref, o_ref):
    s = _silu(c_ref[...]).astype(BF16)
    o_ref[...] = jnp.dot(s, w_ref[...].astype(BF16), preferred_element_type=F32) + b_ref[...]


def _adaln(cvec, ada_w, ada_b):
    tn = 1536
    out = pl.pallas_call(
        _adaln_kernel,
        out_shape=jax.ShapeDtypeStruct((2, 8, 6 * D), F32),
        grid=(2, 6 * D // tn),
        in_specs=[
            pl.BlockSpec((8, D), lambda l, j: (0, 0)),
            pl.BlockSpec((None, D, tn), lambda l, j: (l, 0, j)),
            pl.BlockSpec((None, 1, tn), lambda l, j: (l, 0, j)),
        ],
        out_specs=pl.BlockSpec((None, 8, tn), lambda l, j: (l, 0, j)),
        compiler_params=_params(("arbitrary", "arbitrary")),
        name="adaln",
    )(cvec, ada_w, ada_b.reshape(2, 1, 6 * D))
    return out.reshape(2, 8, 6, D)


def _inproj_kernel(xc_ref, xl_ref, mod_ref, g_ref, win_ref, sgug_ref, sguw_ref, sgub_ref, dft_ref, a_ref, y_ref):
    x = _pick_stream(TM, xc_ref, xl_ref)
    h = _modulate(x, g_ref[...], mod_ref[1:2, :], mod_ref[0:1, :]).astype(BF16)
    p = jnp.dot(h, win_ref[...], preferred_element_type=F32)
    act = _gelu_tanh(p[:, :2 * SGU_W])
    for g in range(4):
        lo, hi = g * GW, (g + 1) * GW
        u = act[:, lo:hi]
        v = act[:, SGU_W + lo:SGU_W + hi]
        ms = jnp.mean(v * v, axis=-1, keepdims=True)
        vn = (v * lax.rsqrt(ms + EPS) * sgug_ref[:, lo:hi]).astype(BF16)
        w = sguw_ref[g]
        for c in range(TM // CHUNK):
            r0, r1 = c * CHUNK, (c + 1) * CHUNK
            mix = jnp.dot(w, vn[r0:r1, :], preferred_element_type=F32) + sgub_ref[:, lo:hi]
            a_ref[r0:r1, lo:hi] = (u[r0:r1, :] * mix).astype(BF16)
        fg = p[:, 2 * SGU_W + lo:2 * SGU_W + hi].astype(BF16)
        yy = jnp.dot(fg, dft_ref[...], preferred_element_type=F32)
        y_ref[:, lo:hi] = yy[:, :GW].astype(BF16)
        y_ref[:, FNET_W + lo:FNET_W + hi] = yy[:, GW:].astype(BF16)


def _inproj(x_ctx, x_lat, mod, g, w_in, sgu_g, sgu_w, sgu_b, tb):
    ctx, lat = _two_stream_specs(TM, D)
    return pl.pallas_call(
        _inproj_kernel,
        out_shape=(jax.ShapeDtypeStruct((T, SGU_W), BF16), jax.ShapeDtypeStruct((T, 2 * FNET_W), BF16)),
        grid=(T // TM,),
        in_specs=[
            ctx, lat,
            _mod_spec(0, TM),
            _resident((1, D)),
            _resident((D, 3 * SGU_W)),
            _resident((1, SGU_W)),
            _resident((4, CHUNK, CHUNK)),
            _resident((CHUNK, SGU_W)),
            _resident((GW, 2 * GW)),
        ],
        out_specs=(pl.BlockSpec((TM, SGU_W), lambda i: (i, 0)), pl.BlockSpec((TM, 2 * FNET_W), lambda i: (i, 0))),
        compiler_params=_params(("arbitrary",)),
        name="inproj_mix",
    )(x_ctx, x_lat, mod, g, w_in, sgu_g, sgu_w, sgu_b, tb["dft_ch"])


def _mix_out(f, a_ref, x_ref, mod_ref, wout_ref, o_ref):
    mix = (jnp.dot(a_ref[...], wout_ref[:SGU_W, :], preferred_element_type=F32)
           + jnp.dot(f.astype(BF16), wout_ref[SGU_W:, :], preferred_element_type=F32))
    o_ref[...] = x_ref[...] + mod_ref[2:3, :] * mix


def _mixout_ctx_kernel(y_ref, a_ref, x_ref, mod_ref, dft_ref, wout_ref, o_ref):
    f = (jnp.dot(dft_ref[:, :CTX_LEN], y_ref[:, :FNET_W], preferred_element_type=F32)
         + jnp.dot(dft_ref[:, CTX_LEN:], y_ref[:, FNET_W:], preferred_element_type=F32))
    _mix_out(f * (1.0 / math.sqrt(CTX_LEN * GW)), a_ref, x_ref, mod_ref, wout_ref, o_ref)


def _mixout_ctx(y, a, x, mod, w_out, tb):
    return pl.pallas_call(
        _mixout_ctx_kernel,
        out_shape=jax.ShapeDtypeStruct((NP, D), F32),
        grid=(N_CTX_SEQ,),
        in_specs=[
            pl.BlockSpec((CTX_LEN, 2 * FNET_W), lambda i: (i, 0)),
            pl.BlockSpec((CTX_LEN, SGU_W), lambda i: (i, 0)),
            pl.BlockSpec((CTX_LEN, D), lambda i: (i, 0)),
            pl.BlockSpec((None, None, 6, D), lambda i: (0, CTX_ROW, 0, 0)),
            _resident((CTX_LEN, 2 * CTX_LEN)),
            _resident((D, D)),
        ],
        out_specs=pl.BlockSpec((CTX_LEN, D), lambda i: (i, 0)),
        compiler_params=_params(("arbitrary",)),
        name="mixout_ctx",
    )(y, a, x, mod, tb["dft_ctx"], w_out)


def _dft1_kernel(y1_ref, y2_ref, c_ref, s_ref, twc_ref, tws_ref, br_ref, bi_ref):
    y1 = y1_ref[...]
    y2 = y2_ref[...]
    c = c_ref[...]
    s = s_ref[...]
    ar = jnp.dot(c, y1, preferred_element_type=F32) - jnp.dot(s, y2, preferred_element_type=F32)
    ai = -(jnp.dot(c, y2, preferred_element_type=F32) + jnp.dot(s, y1, preferred_element_type=F32))
    n2 = pl.program_id(1)
    lane = lax.broadcasted_iota(I32, (DFT_N1, DFT_N2), 1)
    tc = jnp.sum(jnp.where(lane == n2, twc_ref[...], 0.0), axis=-1, keepdims=True)
    ts = jnp.sum(jnp.where(lane == n2, tws_ref[...], 0.0), axis=-1, keepdims=True)
    br_ref[...] = tc * ar + ts * ai
    bi_ref[...] = tc * ai - ts * ar


def _dft1(y1r, y2r, tb):
    shp = jax.ShapeDtypeStruct((N_LAT_SEQ, DFT_N2, DFT_N1, FNET_W), F32)
    return pl.pallas_call(
        _dft1_kernel,
        out_shape=(shp, shp),
        grid=(N_LAT_SEQ, DFT_N2),
        in_specs=[
            pl.BlockSpec((None, DFT_N1, FNET_W), lambda b, n: (b, 0, n)),
            pl.BlockSpec((None, DFT_N1, FNET_W), lambda b, n: (b, 0, n)),
            _resident((DFT_N1, DFT_N1)),
            _resident((DFT_N1, DFT_N1)),
            _resident((DFT_N1, DFT_N2)),
            _resident((DFT_N1, DFT_N2)),
        ],
        out_specs=(pl.BlockSpec((None, None, DFT_N1, FNET_W), lambda b, n: (b, n, 0, 0)),
                   pl.BlockSpec((None, None, DFT_N1, FNET_W), lambda b, n: (b, n, 0, 0))),
        compiler_params=_params(("arbitrary", "arbitrary")),
        name="dft_stage1",
    )(y1r, y2r, tb["c256"], tb["s256"], tb["tw_cos"], tb["tw_sin"])


def _mixout_lat_kernel(c16_ref, s16_ref, br_ref, bi_ref, a_ref, x_ref, mod_ref, wout_ref, o_ref):
    k2 = pl.program_id(1)
    f = None
    for n2 in range(DFT_N2):
        term = c16_ref[k2, n2] * br_ref[n2] + s16_ref[k2, n2] * bi_ref[n2]
        f = term if f is None else f + term
    _mix_out(f * (1.0 / math.sqrt(LAT_LEN * GW)), a_ref, x_ref, mod_ref, wout_ref, o_ref)


def _mixout_lat(br, bi, a, x, mod, w_out, tb):
    off = NP // DFT_N1
    tok = lambda b, k: (off + b * DFT_N2 + k, 0)
    smem = pl.BlockSpec(memory_space=pltpu.SMEM)
    return pl.pallas_call(
        _mixout_lat_kernel,
        out_shape=jax.ShapeDtypeStruct((NS, D), F32),
        grid=(N_LAT_SEQ, DFT_N2),
        in_specs=[
            smem, smem,
            pl.BlockSpec((None, DFT_N2, DFT_N1, FNET_W), lambda b, k: (b, 0, 0, 0), pipeline_mode=pl.Buffered(1)),
            pl.BlockSpec((None, DFT_N2, DFT_N1, FNET_W), lambda b, k: (b, 0, 0, 0), pipeline_mode=pl.Buffered(1)),
            pl.BlockSpec((DFT_N1, SGU_W), tok),
            pl.BlockSpec((DFT_N1, D), lambda b, k: (b * DFT_N2 + k, 0)),
            pl.BlockSpec((None, None, 6, D), lambda b, k: (0, b, 0, 0)),
            _resident((D, D)),
        ],
        out_specs=pl.BlockSpec((DFT_N1, D), lambda b, k: (b * DFT_N2 + k, 0)),
        compiler_params=_params(("arbitrary", "arbitrary")),
        name="mixout_lat",
    )(tb["c16"], tb["s16"], br, bi, a, x, mod, w_out)


def _ffn_kernel(xc_ref, xl_ref, mod_ref, g_ref, w1_ref, w3_ref, w2_ref, o_ref):
    x = _pick_stream(TM, xc_ref, xl_ref)
    h = _modulate(x, g_ref[...], mod_ref[4:5, :], mod_ref[3:4, :]).astype(BF16)
    acc = None
    for c in range(D_FF // FFN_CHUNK):
        lo, hi = c * FFN_CHUNK, (c + 1) * FFN_CHUNK
        a = jnp.dot(h, w1_ref[:, lo:hi], preferred_element_type=F32)
        b = jnp.dot(h, w3_ref[:, lo:hi], preferred_element_type=F32)
        d = jnp.dot((_silu(a) * b).astype(BF16), w2_ref[lo:hi, :], preferred_element_type=F32)
        acc = d if acc is None else acc + d
    o_ref[...] = x + mod_ref[5:6, :] * acc


def _ffn(x_ctx, x_lat, mod, g, w1, w3, w2):
    ctx, lat = _two_stream_specs(TM, D)
    return pl.pallas_call(
        _ffn_kernel,
        out_shape=jax.ShapeDtypeStruct((T, D), F32),
        grid=(T // TM,),
        in_specs=[
            ctx, lat,
            _mod_spec(0, TM),
            _resident((1, D)),
            _resident((D, D_FF)),
            _resident((D, D_FF)),
            _resident((D_FF, D)),
        ],
        out_specs=pl.BlockSpec((TM, D), lambda i: (i, 0)),
        compiler_params=_params(("arbitrary",)),
        name="ffn",
    )(x_ctx, x_lat, mod, g, w1, w3, w2)


def _qkv_ctx_kernel(x_ref, mod_ref, g_ref, w_ref, q_ref, k_ref, v_ref, kf_ref, vf_ref):
    h = _modulate(x_ref[...], g_ref[...], mod_ref[1:2, :], mod_ref[0:1, :]).astype(BF16)
    qkv = jnp.dot(h, w_ref[...], preferred_element_type=F32)
    k = qkv[:, D:2 * D]
    v = qkv[:, 2 * D:]
    q_ref[...] = (qkv[:, :D] * Q_SCALE).astype(BF16)
    k_ref[...] = k.astype(BF16)
    v_ref[...] = v.astype(BF16)
    kf_ref[...] = k
    vf_ref[...] = v


def _qkv_lat_kernel(x_ref, mod_ref, g_ref, w_ref, cos_ref, sin_ref, q_ref, k_ref, v_ref):
    h = _modulate(x_ref[...], g_ref[...], mod_ref[1:2, :], mod_ref[0:1, :]).astype(BF16)
    qkv = jnp.dot(h, w_ref[...], preferred_element_type=F32)
    cos = cos_ref[...]
    sin = sin_ref[...]
    first = (lax.broadcasted_iota(I32, (TM, LANES), 1) & 31) < 16
    for hh in range(N_HEADS):
        for base, dst, sc in ((0, q_ref, Q_SCALE), (D, k_ref, 1.0)):
            xh = qkv[:, base + hh * LANES:base + (hh + 1) * LANES]
            partner = jnp.where(first, pltpu.roll(xh, LANES - 16, 1), pltpu.roll(xh, 16, 1))
            dst[:, hh * LANES:(hh + 1) * LANES] = ((xh * cos + partner * sin) * sc).astype(BF16)
    v_ref[...] = qkv[:, 2 * D:].astype(BF16)


def _qkv(x, mod, g, w_qkv, tb):
    tok = pl.BlockSpec((TM, D), lambda i: (i, 0))
    bshape = jax.ShapeDtypeStruct((NP, D), BF16)
    lshape = jax.ShapeDtypeStruct((NS, D), BF16)
    fshape = jax.ShapeDtypeStruct((NP, D), F32)
    qc, kc, vc, kf, vf = pl.pallas_call(
        _qkv_ctx_kernel,
        out_shape=(bshape, bshape, bshape, fshape, fshape),
        grid=(NP // TM,),
        in_specs=[tok, _mod_spec(1, TM), _resident((1, D)), _resident((D, 3 * D))],
        out_specs=(tok, tok, tok, tok, tok),
        compiler_params=_params(("arbitrary",)),
        name="qkv_ctx",
    )(x, mod, g, w_qkv)
    off = NP // TM
    ltok = pl.BlockSpec((TM, D), lambda i: (i + off, 0))
    rope = pl.BlockSpec((TM, LANES), lambda i: (i % (LAT_LEN // TM), 0))
    ql, kl, vl = pl.pallas_call(
        _qkv_lat_kernel,
        out_shape=(lshape, lshape, lshape),
        grid=(NS // TM,),
        in_specs=[ltok, _mod_spec(1, TM, off), _resident((1, D)), _resident((D, 3 * D)), rope, rope],
        out_specs=(tok, tok, tok),
        compiler_params=_params(("arbitrary",)),
        name="qkv_lat",
    )(x, mod, g, w_qkv, tb["rope_cos"], tb["rope_sin"])
    return (qc, kc, vc), (ql, kl, vl), kf, vf


def _lam(lp_ref):
    lp = lp_ref[...]
    a = jnp.sum(lp[0:1, :] * lp[1:2, :], axis=-1, keepdims=True)
    b = jnp.sum(lp[2:3, :] * lp[3:4, :], axis=-1, keepdims=True)
    return jnp.exp(a) - jnp.exp(b) + LAM0


def _head_norm(o, sg):
    ms = jnp.mean(o * o, axis=-1, keepdims=True)
    return o * lax.rsqrt(ms + EPS) * (sg * (1.0 - LAM0))


_NT = (((1,), (1,)), ((), ()))


def _attn_ctx_kernel(q_ref, k_ref, v_ref, lp_ref, sg_ref, o_ref):
    lam = _lam(lp_ref)
    map0 = lax.broadcasted_iota(I32, (CTX_LEN, LANES), 1) < HEAD_DIM
    for hh in range(N_HEADS):
        sl = slice(hh * LANES, (hh + 1) * LANES)
        q = q_ref[:, sl]
        k = k_ref[:, sl]
        zero = jnp.zeros_like(q)
        probs = []
        for qc in (jnp.where(map0, q, zero), jnp.where(map0, zero, q)):
            s = lax.dot_general(qc, k, _NT, preferred_element_type=F32)
            p = jnp.exp2(s - jnp.max(s, axis=-1, keepdims=True))
            probs.append(p / jnp.sum(p, axis=-1, keepdims=True))
        w = (probs[0] - lam * probs[1]).astype(BF16)
        o = jnp.dot(w, v_ref[:, sl], preferred_element_type=F32)
        o_ref[:, sl] = _head_norm(o, sg_ref[...]).astype(BF16)


def _attn_ctx(q, k, v, lam_p, subln_g):
    seq = pl.BlockSpec((CTX_LEN, D), lambda i: (i, 0))
    return pl.pallas_call(
        _attn_ctx_kernel,
        out_shape=jax.ShapeDtypeStruct((NP, D), BF16),
        grid=(N_CTX_SEQ,),
        in_specs=[seq, seq, seq, _resident((4, HEAD_DIM)), _resident((1, V_DIM))],
        out_specs=seq,
        compiler_params=_params(("arbitrary",)),
        name="attn_ctx",
    )(q, k, v, lam_p, subln_g)


def _softmax_step(qc, kc, vc, state):
    m, l, acc = state
    s = lax.dot_general(qc, kc, _NT, preferred_element_type=F32)
    blocks = [s[:, j * LANES:(j + 1) * LANES] for j in range(s.shape[1] // LANES)]
    bm = functools.reduce(jnp.maximum, blocks)
    m_new = jnp.maximum(m, jnp.max(bm, axis=-1, keepdims=True))
    alpha = jnp.exp2(m - m_new)
    ps = [jnp.exp2(b - m_new) for b in blocks]
    l_new = alpha * l + functools.reduce(jnp.add, ps)
    p = jnp.concatenate(ps, axis=1).astype(BF16)
    acc_new = alpha * acc + jnp.dot(p, vc, preferred_element_type=F32)
    return m_new, l_new, acc_new


def _attn_lat_kernel(q_ref, k_ref, v_ref, ck_ref, cv_ref, lp_ref, sg_ref, o_ref):
    q = q_ref[...]
    zero = jnp.zeros_like(q)
    map0 = lax.broadcasted_iota(I32, (TQ, LANES), 1) < HEAD_DIM
    qs = (jnp.where(map0, q, zero), jnp.where(map0, zero, q))
    init = (jnp.full((TQ, LANES), -jnp.inf, F32), jnp.zeros((TQ, LANES), F32), jnp.zeros((TQ, LANES), F32))
    states = [init, init]
    chunks = [(ck_ref[...].astype(BF16), cv_ref[...].astype(BF16))]
    chunks += [(k_ref[j * TK:(j + 1) * TK, :], v_ref[j * TK:(j + 1) * TK, :]) for j in range(LAT_LEN // TK)]
    for kc, vc in chunks:
        states = [_softmax_step(qs[c], kc, vc, states[c]) for c in range(2)]
    outs = [acc / jnp.sum(l, axis=-1, keepdims=True) for _, l, acc in states]
    o = outs[0] - _lam(lp_ref) * outs[1]
    o_ref[...] = _head_norm(o, sg_ref[...]).astype(BF16)


def _attn_lat(q, k, v, cache_k, cache_v, lam_p, subln_g):
    nq = LAT_LEN // TQ
    qspec = pl.BlockSpec((TQ, LANES), lambda b, h, i: (b * nq + i, h))
    kspec = pl.BlockSpec((LAT_LEN, LANES), lambda b, h, i: (b, h))
    cspec = pl.BlockSpec((None, CTX_LEN, LANES), lambda b, h, i: (b, 0, h))
    return pl.pallas_call(
        _attn_lat_kernel,
        out_shape=jax.ShapeDtypeStruct((NS, D), BF16),
        grid=(N_LAT_SEQ, N_HEADS, nq),
        in_specs=[qspec, kspec, kspec, cspec, cspec, _resident((4, HEAD_DIM)), _resident((1, V_DIM))],
        out_specs=qspec,
        compiler_params=_params(("arbitrary", "arbitrary", "arbitrary")),
        name="attn_lat",
    )(q, k, v, cache_k, cache_v, lam_p, subln_g)


def _oproj_route_kernel(oc_ref, ol_ref, x_ref, mod_ref, wo_ref, g_ref, wr_ref, tri_ref, below_ref, x3_ref, h_ref,
                        route_ref, cnt_ref):
    x3 = x_ref[...] + mod_ref[2:3, :] * jnp.dot(_pick_stream(TM, oc_ref, ol_ref), wo_ref[...], preferred_element_type=F32)
    x3_ref[...] = x3
    hb = _modulate(x3, g_ref[...], mod_ref[4:5, :], mod_ref[3:4, :]).astype(BF16)
    h_ref[...] = hb

    logits = jnp.dot(hb, wr_ref[...], preferred_element_type=F32)
    lane = lax.broadcasted_iota(I32, (TM, LANES), 1).astype(F32)
    neg = jnp.float32(-jnp.inf)
    lg = jnp.where(lane < N_EXPERTS, logits, neg)
    v1 = jnp.max(lg, axis=-1, keepdims=True)
    i1 = jnp.min(jnp.where(lg == v1, lane, float(LANES)), axis=-1, keepdims=True)
    lg2 = jnp.where(lane == i1, neg, lg)
    v2 = jnp.max(lg2, axis=-1, keepdims=True)
    i2 = jnp.min(jnp.where(lg2 == v2, lane, float(LANES)), axis=-1, keepdims=True)
    e = jnp.exp(v2 - v1)
    g1 = 1.0 / (1.0 + e)
    g2 = e / (1.0 + e)
    sel1 = lane == i1
    sel2 = lane == i2
    onehot = jnp.where(sel1 | sel2, 1.0, 0.0)
    rank = jnp.dot(tri_ref[...], onehot.astype(BF16), preferred_element_type=F32)
    cnt8 = jnp.floor((jnp.sum(onehot, axis=0, keepdims=True) + (SEG_ALIGN - 1)) * (1.0 / SEG_ALIGN))
    base8 = jnp.dot(jnp.broadcast_to(cnt8, (8, LANES)).astype(BF16), below_ref[...], preferred_element_type=F32)
    slot = base8[0:1, :] * float(SEG_ALIGN) + rank
    pos1 = jnp.sum(jnp.where(sel1, slot, 0.0), axis=-1, keepdims=True)
    pos2 = jnp.sum(jnp.where(sel2, slot, 0.0), axis=-1, keepdims=True)
    cnt_ref[...] = cnt8
    route = jnp.where(lane == 0, i1, 0.0)
    route = jnp.where(lane == 1, i2, route)
    route = jnp.where(lane == 2, g1, route)
    route = jnp.where(lane == 3, g2, route)
    route = jnp.where(lane == 4, pos1, route)
    route = jnp.where(lane == 5, pos2, route)
    route_ref[...] = route


def _oproj_route(o_ctx, o_lat, x, mod, w_o, g, w_r, tb):
    tok = pl.BlockSpec((TM, D), lambda i: (i, 0))
    ctx, lat = _two_stream_specs(TM, D)
    return pl.pallas_call(
        _oproj_route_kernel,
        out_shape=(jax.ShapeDtypeStruct((T, D), F32), jax.ShapeDtypeStruct((T, D), BF16),
                   jax.ShapeDtypeStruct((T, LANES), F32), jax.ShapeDtypeStruct((T // TM, 1, LANES), F32)),
        grid=(T // TM,),
        in_specs=[ctx, lat, tok, _mod_spec(1, TM), _resident((D, D)), _resident((1, D)), _resident((D, LANES)),
                  _resident((TM, TM)), _resident((LANES, LANES))],
        out_specs=(tok, tok, pl.BlockSpec((TM, LANES), lambda i: (i, 0)),
                   pl.BlockSpec((None, 1, LANES), lambda i: (i, 0, 0))),
        compiler_params=_params(("arbitrary",)),
        name="oproj_route",
    )(o_ctx, o_lat, x, mod, w_o, g, w_r, tb["tri"], tb["below"])


def _bit_copies(n, src0, dst0, bits, make_copy, op):
    for bit in bits:
        done = n & (-2 * bit)

        @pl.when((n & bit) != 0)
        def _(done=done, bit=bit):
            src = pl.multiple_of((src0 + done) * SEG_ALIGN, SEG_ALIGN)
            dst = pl.multiple_of((dst0 + done) * SEG_ALIGN, SEG_ALIGN)
            op(make_copy(src, dst, bit * SEG_ALIGN))


def _segment_copies(tab_ref, make_copy, op):
    for e in range(N_EXPERTS):
        _bit_copies(tab_ref[0, e], tab_ref[0, N_EXPERTS + e], tab_ref[0, 2 * N_EXPERTS + e], SEG_BITS, make_copy, op)


def _start(copy):
    copy.start()


def _wait(copy):
    copy.wait()


def _dispatch_kernel(tab_ref, fill_ref, h_ref, route_ref, xs_ref, sorted_s, zero_s, sem):
    slots = route_ref[...].T
    row = lax.broadcasted_iota(I32, (L_TILE, TM), 0).astype(F32)
    perm = jnp.where((row == slots[4:5, :]) | (row == slots[5:6, :]), 1.0, 0.0).astype(BF16)
    sorted_s[...] = jnp.dot(perm, h_ref[...], preferred_element_type=F32)

    def seg(src, dst, rows):
        return pltpu.make_async_copy(sorted_s.at[pl.ds(src, rows), :], xs_ref.at[pl.ds(dst, rows), :], sem)

    _segment_copies(tab_ref, seg, _start)
    _segment_copies(tab_ref, seg, _wait)

    @pl.when(pl.program_id(0) == 0)
    def _():
        zero_s[...] = jnp.zeros(zero_s.shape, F32)

        def fill(src, dst, rows):
            del src
            return pltpu.make_async_copy(zero_s.at[pl.ds(0, rows), :], xs_ref.at[pl.ds(dst, rows), :], sem)

        def fills(op):
            for e in range(N_EXPERTS):
                _bit_copies(fill_ref[0, N_EXPERTS + e], 0, fill_ref[0, e], SEG_BITS[1:], fill, op)
            for k in range(MAX_UNUSED_TILES):
                @pl.when(k < fill_ref[0, 2 * N_EXPERTS + 1])
                def _(k=k):
                    dst = pl.multiple_of(fill_ref[0, 2 * N_EXPERTS] * SEG_ALIGN + k * T_MOE, SEG_ALIGN)
                    op(fill(0, dst, T_MOE))

        fills(_start)
        fills(_wait)


def _dispatch(tab, fill, h, route):
    smem_tile = pl.BlockSpec((None, 1, LANES), lambda i: (i, 0, 0), memory_space=pltpu.SMEM)
    return pl.pallas_call(
        _dispatch_kernel,
        out_shape=jax.ShapeDtypeStruct((N_ROWS, D), F32),
        grid=(T // TM,),
        in_specs=[
            smem_tile,
            pl.BlockSpec(memory_space=pltpu.SMEM),
            pl.BlockSpec((TM, D), lambda i: (i, 0)),
            pl.BlockSpec((TM, LANES), lambda i: (i, 0)),
        ],
        out_specs=pl.BlockSpec(memory_space=pl.ANY),
        scratch_shapes=[pltpu.VMEM((L_TILE, D), F32), pltpu.VMEM((T_MOE, D), F32), pltpu.SemaphoreType.DMA(())],
        compiler_params=_params(("arbitrary",)),
        name="moe_dispatch",
    )(tab, fill, h, route)


def _moe_kernel(te_ref, tv_ref, xs_ref, w1_ref, w3_ref, w2_ref, y_ref, xb_s, acc_s):
    del te_ref
    i = pl.program_id(0)
    f = pl.program_id(1)
    valid = tv_ref[i] == 1

    @pl.when(f == 0)
    def _():
        xb_s[...] = xs_ref[...].astype(BF16)
        acc_s[...] = jnp.zeros(acc_s.shape, F32)

    @pl.when(valid)
    def _():
        x = xb_s[...]
        a = jnp.dot(x, w1_ref[...], preferred_element_type=F32)
        b = jnp.dot(x, w3_ref[...], preferred_element_type=F32)
        acc_s[...] += jnp.dot((_silu(a) * b).astype(BF16), w2_ref[...], preferred_element_type=F32)

    @pl.when(f == N_F_MOE - 1)
    def _():
        y_ref[...] = acc_s[...]


def _moe(tile_expert, tile_valid, xs, w1, w3, w2):
    def fidx(i, f, te, tv):
        return jnp.where(tv[i] == 1, f, N_F_MOE - 1)

    grid_spec = pltpu.PrefetchScalarGridSpec(
        num_scalar_prefetch=2,
        grid=(N_MOE_TILES, N_F_MOE),
        in_specs=[
            pl.BlockSpec((T_MOE, D), lambda i, f, te, tv: (i, 0)),
            pl.BlockSpec((None, D, F_MOE), lambda i, f, te, tv: (te[i], 0, fidx(i, f, te, tv))),
            pl.BlockSpec((None, D, F_MOE), lambda i, f, te, tv: (te[i], 0, fidx(i, f, te, tv))),
            pl.BlockSpec((None, F_MOE, D), lambda i, f, te, tv: (te[i], fidx(i, f, te, tv), 0)),
        ],
        out_specs=pl.BlockSpec((T_MOE, D), lambda i, f, te, tv: (i, 0)),
        scratch_shapes=[pltpu.VMEM((T_MOE, D), BF16), pltpu.VMEM((T_MOE, D), F32)],
    )
    return pl.pallas_call(
        _moe_kernel,
        out_shape=jax.ShapeDtypeStruct((N_ROWS, D), F32),
        grid_spec=grid_spec,
        compiler_params=_params(("arbitrary", "arbitrary")),
        name="moe_experts",
    )(tile_expert, tile_valid, xs, w1, w3, w2)


def _combine_kernel(tab_ref, y_ref, x_ref, route_ref, mod_ref, fg_ref, o_ref, buf, sem):
    buf[...] = jnp.zeros(buf.shape, F32)

    def seg(src, dst, rows):
        return pltpu.make_async_copy(y_ref.at[pl.ds(dst, rows), :], buf.at[pl.ds(src, rows), :], sem)

    _segment_copies(tab_ref, seg, _start)
    _segment_copies(tab_ref, seg, _wait)
    yb = buf[...].astype(BF16)
    route = route_ref[...]
    col = lax.broadcasted_iota(I32, (TM, L_TILE), 1).astype(F32)
    picks = [jnp.dot(jnp.where(col == route[:, c:c + 1], 1.0, 0.0).astype(BF16), yb, preferred_element_type=F32)
             for c in (4, 5)]
    moe = route[:, 2:3] * picks[0] + route[:, 3:4] * picks[1]
    x = x_ref[...] + mod_ref[5:6, :] * moe
    ms = jnp.mean(x * x, axis=-1, keepdims=True)
    o_ref[...] = x * lax.rsqrt(ms + EPS) * fg_ref[...]


def _combine(tab, y, x, route, mod, final_g, n_tok, off_tiles):
    tok = lambda w: pl.BlockSpec((TM, w), lambda i: (i + off_tiles, 0))
    return pl.pallas_call(
        _combine_kernel,
        out_shape=jax.ShapeDtypeStruct((n_tok, D), F32),
        grid=(n_tok // TM,),
        in_specs=[
            pl.BlockSpec((None, 1, LANES), lambda i: (i + off_tiles, 0, 0), memory_space=pltpu.SMEM),
            pl.BlockSpec(memory_space=pl.ANY),
            tok(D), tok(LANES), _mod_spec(1, TM, off_tiles), _resident((1, D)),
        ],
        out_specs=pl.BlockSpec((TM, D), lambda i: (i, 0)),
        scratch_shapes=[pltpu.VMEM((L_TILE, D), F32), pltpu.SemaphoreType.DMA(())],
        compiler_params=_params(("arbitrary",)),
        name="moe_combine",
    )(tab, y, x, route, mod, final_g)


def kernel(x_prompt, x_sample, cache_k, cache_v, c, c_ctx, ada_w, ada_b, norm1_g, norm2_g, final_g, mix_w_in, sgu_g, sgu_w, sgu_b, mix_w_out, ffn_w1, ffn_w3, ffn_w2, attn_w_qkv, lam_q1, lam_k1, lam_q2, lam_k2, subln_g, attn_w_o, router_w, moe_w1, moe_w3, moe_w2):
    tb = _device_tables()
    xc = x_prompt.reshape(NP, D)
    xl = x_sample.reshape(NS, D)
    cvec = jnp.concatenate([c, c_ctx[None, :], jnp.zeros((3, D), F32)], axis=0)
    mod = _adaln(cvec, ada_w, ada_b)

    bias = jnp.repeat(sgu_b[0].T, GW, axis=1)
    a, y = _inproj(xc, xl, mod, norm1_g[0][None, :], mix_w_in[0].astype(BF16), sgu_g[0].reshape(1, SGU_W),
                   sgu_w[0].astype(BF16), bias, tb)
    w_out = mix_w_out[0].astype(BF16)
    x1c = _mixout_ctx(y, a, xc, mod, w_out, tb)
    ylat = y[NP:]
    y1r = ylat[:, :FNET_W].reshape(N_LAT_SEQ, DFT_N1, DFT_N2 * FNET_W)
    y2r = ylat[:, FNET_W:].reshape(N_LAT_SEQ, DFT_N1, DFT_N2 * FNET_W)
    br, bi = _dft1(y1r, y2r, tb)
    x1l = _mixout_lat(br, bi, a, xl, mod, w_out, tb)
    x2 = _ffn(x1c, x1l, mod, norm2_g[0][None, :], ffn_w1[0].astype(BF16), ffn_w3[0].astype(BF16), ffn_w2[0].astype(BF16))

    qkv_c, qkv_l, kf, vf = _qkv(x2, mod, norm1_g[1][None, :], attn_w_qkv[0].astype(BF16), tb)
    lam_p = jnp.stack([lam_q1[0], lam_k1[0], lam_q2[0], lam_k2[0]], axis=0)
    sg = subln_g[0][None, :]
    oc = _attn_ctx(*qkv_c, lam_p, sg)
    ol = _attn_lat(*qkv_l, cache_k[:, 0].reshape(N_LAT_SEQ, CTX_LEN, D), cache_v[:, 0].reshape(N_LAT_SEQ, CTX_LEN, D),
                   lam_p, sg)

    w_r = jnp.pad(router_w[0], ((0, 0), (0, LANES - N_EXPERTS))).astype(BF16)
    x3, h2, route, cnt = _oproj_route(oc, ol, x2, mod, attn_w_o[0].astype(BF16), norm2_g[1][None, :], w_r, tb)
    unit_per_tile = T_MOE // SEG_ALIGN
    seg_len = cnt[:, 0, :N_EXPERTS].astype(I32)
    seg_local = jnp.cumsum(seg_len, axis=1) - seg_len
    group_len = jnp.sum(seg_len, axis=0)
    group_tiles = (group_len + unit_per_tile - 1) // unit_per_tile
    tile_ends = jnp.cumsum(group_tiles)
    group_start = (tile_ends - group_tiles) * unit_per_tile
    seg_global = group_start[None, :] + jnp.cumsum(seg_len, axis=0) - seg_len
    tab = jnp.concatenate([seg_len, seg_local, seg_global], axis=1)
    tab = jnp.pad(tab, ((0, 0), (0, LANES - 3 * N_EXPERTS))).reshape(T // TM, 1, LANES)
    fill = jnp.concatenate([group_start + group_len, group_tiles * unit_per_tile - group_len,
                            tile_ends[-1:] * unit_per_tile, N_MOE_TILES - tile_ends[-1:]])
    fill = jnp.pad(fill, (0, LANES - 2 * N_EXPERTS - 2)).reshape(1, LANES)
    tile_idx = jnp.arange(N_MOE_TILES, dtype=I32)
    tile_valid = (tile_idx < tile_ends[-1]).astype(I32)
    tile_expert = jnp.sum((jnp.minimum(tile_idx, tile_ends[-1] - 1)[:, None] >= tile_ends[None, :]).astype(I32), axis=-1)

    xs = _dispatch(tab, fill, h2, route)
    ys = _moe(tile_expert, tile_valid, xs, moe_w1[0].astype(BF16), moe_w3[0].astype(BF16), moe_w2[0].astype(BF16))
    fg = final_g[None, :]
    y_prompt = _combine(tab, ys, x3, route, mod, fg, NP, 0)
    y_sample = _combine(tab, ys, x3, route, mod, fg, NS, NP // TM)

    return (y_prompt.reshape(N_CTX_SEQ, CTX_LEN, D), y_sample.reshape(N_LAT_SEQ, LAT_LEN, D),
            kf.reshape(N_CTX_SEQ, 1, CTX_LEN, N_HEADS, 2 * HEAD_DIM), vf.reshape(N_CTX_SEQ, 1, CTX_LEN, N_HEADS, V_DIM))
```

```python
import functools
import math

import numpy as np
import jax
import jax.numpy as jnp
from jax import lax
from jax.experimental import pallas as pl
from jax.experimental.pallas import tpu as pltpu

F32 = jnp.float32
BF16 = jnp.bfloat16
I32 = jnp.int32

D = 1024
N_CTX_SEQ = 32
CTX_LEN = 256
N_LAT_SEQ = 4
LAT_LEN = 4096
NP = N_CTX_SEQ * CTX_LEN
NS = N_LAT_SEQ * LAT_LEN
T = NP + NS
GRID_W = 64
CHUNK = 128
SGU_W = 512
FNET_W = 512
GW = 128
N_HEADS = 8
HEAD_DIM = 64
V_DIM = 128
ROPE_THETA = 10000.0
D_FF = 2816
N_EXPERTS = 8
D_FF_EXPERT = 3584
EPS = 1e-6
LAM0 = 0.8 - 0.6 * math.exp(-0.3 * 1)
CTX_ROW = 4

LANES = 128
TM = 512
TQ = 512
TK = 1024
Q_SCALE = HEAD_DIM ** -0.5 * math.log2(math.e)
T_MOE = 512
F_MOE = 1792
N_F_MOE = D_FF_EXPERT // F_MOE
SEG_ALIGN = 8
SEG_BITS = (64, 32, 16, 8, 4, 2, 1)
L_TILE = 1152
N_ROWS = -(-(2 * T + (T // TM) * N_EXPERTS * (SEG_ALIGN - 1) + N_EXPERTS * (T_MOE - SEG_ALIGN)) // T_MOE) * T_MOE
N_MOE_TILES = N_ROWS // T_MOE
MAX_UNUSED_TILES = N_MOE_TILES - 2 * T // T_MOE
FFN_CHUNK = 1408
DFT_N1 = 256
DFT_N2 = 16
DFT_ROWS = 32
VMEM_LIMIT = 56 * 2 ** 20


def _params(sem, vmem=VMEM_LIMIT):
    return pltpu.CompilerParams(dimension_semantics=sem, vmem_limit_bytes=vmem)


def _resident(shape):
    nd = len(shape)
    return pl.BlockSpec(shape, lambda *_: (0,) * nd, pipeline_mode=pl.Buffered(1))


def _mod_row(i, tm):
    npt = NP // tm
    return jnp.where(i < npt, CTX_ROW, (i - npt) // (LAT_LEN // tm))


def _mod_spec(layer, tm, offset_tiles=0):
    return pl.BlockSpec((None, None, 6, D), lambda i, *_: (layer, _mod_row(i + offset_tiles, tm), 0, 0))


def _two_stream_specs(tm, width):
    npt = NP // tm
    ctx = pl.BlockSpec((tm, width), lambda i: (jnp.minimum(i, npt - 1), 0))
    lat = pl.BlockSpec((tm, width), lambda i: (jnp.maximum(i - npt, 0), 0))
    return ctx, lat


def _pick_stream(tm, ctx_ref, lat_ref):
    return jnp.where(pl.program_id(0) < NP // tm, ctx_ref[...], lat_ref[...])


def _modulate(x, g, scale, shift):
    ms = jnp.mean(x * x, axis=-1, keepdims=True)
    return x * lax.rsqrt(ms + EPS) * (g * (1.0 + scale)) + shift


def _silu(a):
    return a * jax.nn.sigmoid(a)


def _gelu_tanh(x):
    return 0.5 * x * (1.0 + jnp.tanh(0.7978845608028654 * (x + 0.044715 * (x * x * x))))


@functools.lru_cache(maxsize=None)
def _tables():
    def cs(n):
        k = np.arange(n, dtype=np.int64)
        ang = 2.0 * np.pi * ((k[:, None] * k[None, :]) % n) / n
        return np.cos(ang), np.sin(ang)

    c128, s128 = cs(GW)
    dft_ch = np.concatenate([c128, s128], axis=1)
    c256, s256 = cs(CTX_LEN)
    dft_ctx = np.concatenate([c256, -s256], axis=1)
    k1 = np.arange(DFT_N1, dtype=np.int64)[:, None]
    n2 = np.arange(DFT_N2, dtype=np.int64)[None, :]
    tw = 2.0 * np.pi * ((k1 * n2) % LAT_LEN) / LAT_LEN
    half = HEAD_DIM // 2
    inv_freq = ROPE_THETA ** (-np.arange(0, half, 2, dtype=np.float64) / half)
    t = np.arange(LAT_LEN)
    ang_r = (t // GRID_W).astype(np.float64)[:, None] * inv_freq[None, :]
    ang_c = (t % GRID_W).astype(np.float64)[:, None] * inv_freq[None, :]

    def blk(ang):
        c = np.cos(ang)
        s = np.sin(ang)
        return np.concatenate([c, c], axis=1), np.concatenate([-s, s], axis=1)

    cr, sr = blk(ang_r)
    cc, sc = blk(ang_c)
    cos64 = np.concatenate([cr, cc], axis=1)
    sin64 = np.concatenate([sr, sc], axis=1)
    rope_cos = np.concatenate([cos64, cos64], axis=1).astype(np.float32)
    rope_sin = np.concatenate([sin64, sin64], axis=1).astype(np.float32)
    tri = np.tril(np.ones((TM, TM), np.float32), k=-1)
    below = np.triu(np.ones((LANES, LANES), np.float32), k=1)
    f32 = lambda a: np.asarray(a, np.float32)
    return dict(dft_ch=f32(dft_ch), dft_ctx=f32(dft_ctx), c256=f32(c256), s256=f32(s256),
                tw_cos=f32(np.cos(tw)), tw_sin=f32(np.sin(tw)),
                rope_cos=rope_cos, rope_sin=rope_sin, tri=tri, below=below)


def _device_tables():
    tb = {k: jnp.asarray(v) for k, v in _tables().items()}
    for k in ("dft_ch", "dft_ctx", "c256", "s256", "tri", "below"):
        tb[k] = tb[k].astype(BF16)
    return tb


def _adaln_kernel(c_ref, w_ref, b_ref, o_ref):
    s = _silu(c_ref[...]).astype(BF16)
    o_ref[...] = jnp.dot(s, w_ref[...].astype(BF16), preferred_element_type=F32) + b_ref[...]


def _adaln(cvec, ada_w, ada_b):
    tn = 1536
    out = pl.pallas_call(
        _adaln_kernel,
        out_shape=jax.ShapeDtypeStruct((2, 8, 6 * D), F32),
        grid=(2, 6 * D // tn),
        in_specs=[
            pl.BlockSpec((8, D), lambda l, j: (0, 0)),
            pl.BlockSpec((None, D, tn), lambda l, j: (l, 0, j)),
            pl.BlockSpec((None, 1, tn), lambda l, j: (l, 0, j)),
        ],
        out_specs=pl.BlockSpec((None, 8, tn), lambda l, j: (l, 0, j)),
        compiler_params=_params(("arbitrary", "arbitrary")),
        name="adaln",
    )(cvec, ada_w, ada_b.reshape(2, 1, 6 * D))
    return out.reshape(2, 8, 6, D)


def _inproj_kernel(xc_ref, xl_ref, mod_ref, g_ref, win_ref, sgug_ref, sguw_ref, sgub_ref, dft_ref, a_ref, y_ref):
    x = _pick_stream(TM, xc_ref, xl_ref)
    h = _modulate(x, g_ref[...], mod_ref[1:2, :], mod_ref[0:1, :]).astype(BF16)
    p = jnp.dot(h, win_ref[...], preferred_element_type=F32)
    act = _gelu_tanh(p[:, :2 * SGU_W])
    for g in range(4):
        lo, hi = g * GW, (g + 1) * GW
        u = act[:, lo:hi]
        v = act[:, SGU_W + lo:SGU_W + hi]
        ms = jnp.mean(v * v, axis=-1, keepdims=True)
        vn = (v * lax.rsqrt(ms + EPS) * sgug_ref[:, lo:hi]).astype(BF16)
        w = sguw_ref[g]
        for c in range(TM // CHUNK):
            r0, r1 = c * CHUNK, (c + 1) * CHUNK
            mix = jnp.dot(w, vn[r0:r1, :], preferred_element_type=F32) + sgub_ref[:, lo:hi]
            a_ref[r0:r1, lo:hi] = (u[r0:r1, :] * mix).astype(BF16)
        fg = p[:, 2 * SGU_W + lo:2 * SGU_W + hi].astype(BF16)
        yy = jnp.dot(fg, dft_ref[...], preferred_element_type=F32)
        y_ref[:, lo:hi] = yy[:, :GW].astype(BF16)
        y_ref[:, FNET_W + lo:FNET_W + hi] = yy[:, GW:].astype(BF16)


def _inproj(x_ctx, x_lat, mod, g, w_in, sgu_g, sgu_w, sgu_b, tb):
    ctx, lat = _two_stream_specs(TM, D)
    return pl.pallas_call(
        _inproj_kernel,
        out_shape=(jax.ShapeDtypeStruct((T, SGU_W), BF16), jax.ShapeDtypeStruct((T, 2 * FNET_W), BF16)),
        grid=(T // TM,),
        in_specs=[
            ctx, lat,
            _mod_spec(0, TM),
            _resident((1, D)),
            _resident((D, 3 * SGU_W)),
            _resident((1, SGU_W)),
            _resident((4, CHUNK, CHUNK)),
            _resident((CHUNK, SGU_W)),
            _resident((GW, 2 * GW)),
        ],
        out_specs=(pl.BlockSpec((TM, SGU_W), lambda i: (i, 0)), pl.BlockSpec((TM, 2 * FNET_W), lambda i: (i, 0))),
        compiler_params=_params(("arbitrary",)),
        name="inproj_mix",
    )(x_ctx, x_lat, mod, g, w_in, sgu_g, sgu_w, sgu_b, tb["dft_ch"])


def _mix_out(f, a_ref, x_ref, mod_ref, wout_ref, o_ref):
    mix = (jnp.dot(a_ref[...], wout_ref[:SGU_W, :], preferred_element_type=F32)
           + jnp.dot(f.astype(BF16), wout_ref[SGU_W:, :], preferred_element_type=F32))
    o_ref[...] = x_ref[...] + mod_ref[2:3, :] * mix


def _mixout_ctx_kernel(y_ref, a_ref, x_ref, mod_ref, dft_ref, wout_ref, o_ref):
    f = (jnp.dot(dft_ref[:, :CTX_LEN], y_ref[:, :FNET_W], preferred_element_type=F32)
         + jnp.dot(dft_ref[:, CTX_LEN:], y_ref[:, FNET_W:], preferred_element_type=F32))
    _mix_out(f * (1.0 / math.sqrt(CTX_LEN * GW)), a_ref, x_ref, mod_ref, wout_ref, o_ref)


def _mixout_ctx(y, a, x, mod, w_out, tb):
    return pl.pallas_call(
        _mixout_ctx_kernel,
        out_shape=jax.ShapeDtypeStruct((NP, D), F32),
        grid=(N_CTX_SEQ,),
        in_specs=[
            pl.BlockSpec((CTX_LEN, 2 * FNET_W), lambda i: (i, 0)),
            pl.BlockSpec((CTX_LEN, SGU_W), lambda i: (i, 0)),
            pl.BlockSpec((CTX_LEN, D), lambda i: (i, 0)),
            pl.BlockSpec((None, None, 6, D), lambda i: (0, CTX_ROW, 0, 0)),
            _resident((CTX_LEN, 2 * CTX_LEN)),
            _resident((D, D)),
        ],
        out_specs=pl.BlockSpec((CTX_LEN, D), lambda i: (i, 0)),
        compiler_params=_params(("arbitrary",)),
        name="mixout_ctx",
    )(y, a, x, mod, tb["dft_ctx"], w_out)


def _dft1_kernel(y1_ref, y2_ref, c_ref, s_ref, twc_ref, tws_ref, br_ref, bi_ref):
    y1 = y1_ref[...]
    y2 = y2_ref[...]
    c = c_ref[...]
    s = s_ref[...]
    ar = jnp.dot(c, y1, preferred_element_type=F32) - jnp.dot(s, y2, preferred_element_type=F32)
    ai = -(jnp.dot(c, y2, preferred_element_type=F32) + jnp.dot(s, y1, preferred_element_type=F32))
    n2 = pl.program_id(1)
    lane = lax.broadcasted_iota(I32, (DFT_N1, DFT_N2), 1)
    tc = jnp.sum(jnp.where(lane == n2, twc_ref[...], 0.0), axis=-1, keepdims=True)
    ts = jnp.sum(jnp.where(lane == n2, tws_ref[...], 0.0), axis=-1, keepdims=True)
    br_ref[...] = tc * ar + ts * ai
    bi_ref[...] = tc * ai - ts * ar


def _dft1(y1r, y2r, tb):
    shp = jax.ShapeDtypeStruct((N_LAT_SEQ, DFT_N2, DFT_N1, FNET_W), F32)
    return pl.pallas_call(
        _dft1_kernel,
        out_shape=(shp, shp),
        grid=(N_LAT_SEQ, DFT_N2),
        in_specs=[
            pl.BlockSpec((None, DFT_N1, FNET_W), lambda b, n: (b, 0, n)),
            pl.BlockSpec((None, DFT_N1, FNET_W), lambda b, n: (b, 0, n)),
            _resident((DFT_N1, DFT_N1)),
            _resident((DFT_N1, DFT_N1)),
            _resident((DFT_N1, DFT_N2)),
            _resident((DFT_N1, DFT_N2)),
        ],
        out_specs=(pl.BlockSpec((None, None, DFT_N1, FNET_W), lambda b, n: (b, n, 0, 0)),
                   pl.BlockSpec((None, None, DFT_N1, FNET_W), lambda b, n: (b, n, 0, 0))),
        compiler_params=_params(("arbitrary", "arbitrary")),
        name="dft_stage1",
    )(y1r, y2r, tb["c256"], tb["s256"], tb["tw_cos"], tb["tw_sin"])


def _cmul_const(z, wr, wi):
    re, im = z
    tol = 1e-12
    if abs(wi) < tol:
        return (re, im) if wr > 0 else (-re, -im)
    if abs(wr) < tol:
        return (-im, re) if wi > 0 else (im, -re)
    return (re * wr - im * wi, re * wi + im * wr)


def _fft(zs):
    n = len(zs)
    if n == 1:
        return zs
    even, odd = _fft(zs[0::2]), _fft(zs[1::2])
    out = [None] * n
    for k in range(n // 2):
        ang = 2.0 * math.pi * k / n
        t = _cmul_const(odd[k], math.cos(ang), -math.sin(ang))
        out[k] = (even[k][0] + t[0], even[k][1] + t[1])
        out[k + n // 2] = (even[k][0] - t[0], even[k][1] - t[1])
    return out


def _mixout_lat_kernel(br_ref, bi_ref, a_ref, x_ref, mod_ref, wout_ref, o_ref):
    spec = _fft([(br_ref[n2], bi_ref[n2]) for n2 in range(DFT_N2)])
    f = jnp.concatenate([re for re, _ in spec], axis=0) * (1.0 / math.sqrt(LAT_LEN * GW))
    rows = DFT_N2 * DFT_ROWS
    mix = (jnp.dot(a_ref[...].reshape(rows, SGU_W), wout_ref[:SGU_W, :], preferred_element_type=F32)
           + jnp.dot(f.astype(BF16), wout_ref[SGU_W:, :], preferred_element_type=F32))
    out = x_ref[...].reshape(rows, D) + mod_ref[2:3, :] * mix
    o_ref[...] = out.reshape(DFT_N2, DFT_ROWS, D)


def _mixout_lat(br, bi, a_lat, x, mod, w_out):
    blk = lambda w, off=0: pl.BlockSpec((None, DFT_N2, DFT_ROWS, w), lambda b, r: (b + off, 0, r, 0))
    return pl.pallas_call(
        _mixout_lat_kernel,
        out_shape=jax.ShapeDtypeStruct((N_LAT_SEQ, DFT_N2, DFT_N1, D), F32),
        grid=(N_LAT_SEQ, DFT_N1 // DFT_ROWS),
        in_specs=[blk(FNET_W), blk(FNET_W), blk(SGU_W, NP // LAT_LEN), blk(D),
                  pl.BlockSpec((None, None, 6, D), lambda b, r: (0, b, 0, 0)),
                  _resident((D, D))],
        out_specs=blk(D),
        compiler_params=_params(("arbitrary", "arbitrary")),
        name="mixout_lat",
    )(br, bi, a_lat, x, mod, w_out)


def _ffn_kernel(xc_ref, xl_ref, mod_ref, g_ref, w1_ref, w3_ref, w2_ref, o_ref):
    x = _pick_stream(TM, xc_ref, xl_ref)
    h = _modulate(x, g_ref[...], mod_ref[4:5, :], mod_ref[3:4, :]).astype(BF16)
    acc = None
    for c in range(D_FF // FFN_CHUNK):
        lo, hi = c * FFN_CHUNK, (c + 1) * FFN_CHUNK
        a = jnp.dot(h, w1_ref[:, lo:hi], preferred_element_type=F32)
        b = jnp.dot(h, w3_ref[:, lo:hi], preferred_element_type=F32)
        d = jnp.dot((_silu(a) * b).astype(BF16), w2_ref[lo:hi, :], preferred_element_type=F32)
        acc = d if acc is None else acc + d
    o_ref[...] = x + mod_ref[5:6, :] * acc


def _ffn(x_ctx, x_lat, mod, g, w1, w3, w2):
    ctx, lat = _two_stream_specs(TM, D)
    return pl.pallas_call(
        _ffn_kernel,
        out_shape=jax.ShapeDtypeStruct((T, D), F32),
        grid=(T // TM,),
        in_specs=[
            ctx, lat,
            _mod_spec(0, TM),
            _resident((1, D)),
            _resident((D, D_FF)),
            _resident((D, D_FF)),
            _resident((D_FF, D)),
        ],
        out_specs=pl.BlockSpec((TM, D), lambda i: (i, 0)),
        compiler_params=_params(("arbitrary",)),
        name="ffn",
    )(x_ctx, x_lat, mod, g, w1, w3, w2)


def _qkv_ctx_kernel(x_ref, mod_ref, g_ref, w_ref, q_ref, k_ref, v_ref, kf_ref, vf_ref):
    h = _modulate(x_ref[...], g_ref[...], mod_ref[1:2, :], mod_ref[0:1, :]).astype(BF16)
    qkv = jnp.dot(h, w_ref[...], preferred_element_type=F32)
    k = qkv[:, D:2 * D]
    v = qkv[:, 2 * D:]
    q_ref[...] = (qkv[:, :D] * Q_SCALE).astype(BF16)
    k_ref[...] = k.astype(BF16)
    v_ref[...] = v.astype(BF16)
    kf_ref[...] = k
    vf_ref[...] = v


def _qkv_lat_kernel(x_ref, mod_ref, g_ref, w_ref, cos_ref, sin_ref, q_ref, k_ref, v_ref):
    h = _modulate(x_ref[...], g_ref[...], mod_ref[1:2, :], mod_ref[0:1, :]).astype(BF16)
    qkv = jnp.dot(h, w_ref[...], preferred_element_type=F32)
    cos = cos_ref[...]
    sin = sin_ref[...]
    first = (lax.broadcasted_iota(I32, (TM, LANES), 1) & 31) < 16
    for hh in range(N_HEADS):
        for base, dst, sc in ((0, q_ref, Q_SCALE), (D, k_ref, 1.0)):
            xh = qkv[:, base + hh * LANES:base + (hh + 1) * LANES]
            partner = jnp.where(first, pltpu.roll(xh, LANES - 16, 1), pltpu.roll(xh, 16, 1))
            dst[:, hh * LANES:(hh + 1) * LANES] = ((xh * cos + partner * sin) * sc).astype(BF16)
    v_ref[...] = qkv[:, 2 * D:].astype(BF16)


def _qkv(x, mod, g, w_qkv, tb):
    tok = pl.BlockSpec((TM, D), lambda i: (i, 0))
    bshape = jax.ShapeDtypeStruct((NP, D), BF16)
    lshape = jax.ShapeDtypeStruct((NS, D), BF16)
    fshape = jax.ShapeDtypeStruct((NP, D), F32)
    qc, kc, vc, kf, vf = pl.pallas_call(
        _qkv_ctx_kernel,
        out_shape=(bshape, bshape, bshape, fshape, fshape),
        grid=(NP // TM,),
        in_specs=[tok, _mod_spec(1, TM), _resident((1, D)), _resident((D, 3 * D))],
        out_specs=(tok, tok, tok, tok, tok),
        compiler_params=_params(("arbitrary",)),
        name="qkv_ctx",
    )(x, mod, g, w_qkv)
    off = NP // TM
    ltok = pl.BlockSpec((TM, D), lambda i: (i + off, 0))
    rope = pl.BlockSpec((TM, LANES), lambda i: (i % (LAT_LEN // TM), 0))
    ql, kl, vl = pl.pallas_call(
        _qkv_lat_kernel,
        out_shape=(lshape, lshape, lshape),
        grid=(NS // TM,),
        in_specs=[ltok, _mod_spec(1, TM, off), _resident((1, D)), _resident((D, 3 * D)), rope, rope],
        out_specs=(tok, tok, tok),
        compiler_params=_params(("arbitrary",)),
        name="qkv_lat",
    )(x, mod, g, w_qkv, tb["rope_cos"], tb["rope_sin"])
    return (qc, kc, vc), (ql, kl, vl), kf, vf


def _lam(lp_ref):
    lp = lp_ref[...]
    a = jnp.sum(lp[0:1, :] * lp[1:2, :], axis=-1, keepdims=True)
    b = jnp.sum(lp[2:3, :] * lp[3:4, :], axis=-1, keepdims=True)
    return jnp.exp(a) - jnp.exp(b) + LAM0


def _head_norm(o, sg):
    ms = jnp.mean(o * o, axis=-1, keepdims=True)
    return o * lax.rsqrt(ms + EPS) * (sg * (1.0 - LAM0))


_NT = (((1,), (1,)), ((), ()))


def _attn_ctx_kernel(q_ref, k_ref, v_ref, lp_ref, sg_ref, o_ref):
    lam = _lam(lp_ref)
    map0 = lax.broadcasted_iota(I32, (CTX_LEN, LANES), 1) < HEAD_DIM
    init = _softmax_init(CTX_LEN)
    for hh in range(N_HEADS):
        sl = slice(hh * LANES, (hh + 1) * LANES)
        q = q_ref[:, sl]
        zero = jnp.zeros_like(q)
        outs = []
        for qc in (jnp.where(map0, q, zero), jnp.where(map0, zero, q)):
            _, l, acc = _softmax_step(qc, k_ref[:, sl], v_ref[:, sl], init)
            outs.append(acc / jnp.sum(l, axis=-1, keepdims=True))
        o_ref[:, sl] = _head_norm(outs[0] - lam * outs[1], sg_ref[...]).astype(BF16)


def _attn_ctx(q, k, v, lam_p, subln_g):
    seq = pl.BlockSpec((CTX_LEN, D), lambda i: (i, 0))
    return pl.pallas_call(
        _attn_ctx_kernel,
        out_shape=jax.ShapeDtypeStruct((NP, D), BF16),
        grid=(N_CTX_SEQ,),
        in_specs=[seq, seq, seq, _resident((4, HEAD_DIM)), _resident((1, V_DIM))],
        out_specs=seq,
        compiler_params=_params(("arbitrary",)),
        name="attn_ctx",
    )(q, k, v, lam_p, subln_g)


def _softmax_init(rows):
    return (jnp.full((rows, LANES), -jnp.inf, F32), jnp.zeros((rows, LANES), F32), jnp.zeros((rows, LANES), F32))


def _softmax_step(qc, kc, vc, state):
    m, l, acc = state
    s = lax.dot_general(qc, kc, _NT, preferred_element_type=F32)
    blocks = [s[:, j * LANES:(j + 1) * LANES] for j in range(s.shape[1] // LANES)]
    bm = functools.reduce(jnp.maximum, blocks)
    m_new = jnp.maximum(m, jnp.max(bm, axis=-1, keepdims=True))
    alpha = jnp.exp2(m - m_new)
    ps = [jnp.exp2(b - m_new) for b in blocks]
    l_new = alpha * l + functools.reduce(jnp.add, ps)
    p = jnp.concatenate(ps, axis=1).astype(BF16)
    acc_new = alpha * acc + jnp.dot(p, vc, preferred_element_type=F32)
    return m_new, l_new, acc_new


def _attn_lat_kernel(q_ref, k_ref, v_ref, ck_ref, cv_ref, lp_ref, sg_ref, o_ref):
    q = q_ref[...]
    zero = jnp.zeros_like(q)
    map0 = lax.broadcasted_iota(I32, (TQ, LANES), 1) < HEAD_DIM
    qs = (jnp.where(map0, q, zero), jnp.where(map0, zero, q))
    init = _softmax_init(TQ)
    states = [init, init]
    chunks = [(ck_ref[...].astype(BF16), cv_ref[...].astype(BF16))]
    chunks += [(k_ref[j * TK:(j + 1) * TK, :], v_ref[j * TK:(j + 1) * TK, :]) for j in range(LAT_LEN // TK)]
    for kc, vc in chunks:
        states = [_softmax_step(qs[c], kc, vc, states[c]) for c in range(2)]
    outs = [acc / jnp.sum(l, axis=-1, keepdims=True) for _, l, acc in states]
    o = outs[0] - _lam(lp_ref) * outs[1]
    o_ref[...] = _head_norm(o, sg_ref[...]).astype(BF16)


def _attn_lat(q, k, v, cache_k, cache_v, lam_p, subln_g):
    nq = LAT_LEN // TQ
    qspec = pl.BlockSpec((TQ, LANES), lambda b, h, i: (b * nq + i, h))
    kspec = pl.BlockSpec((LAT_LEN, LANES), lambda b, h, i: (b, h))
    cspec = pl.BlockSpec((None, CTX_LEN, LANES), lambda b, h, i: (b, 0, h))
    return pl.pallas_call(
        _attn_lat_kernel,
        out_shape=jax.ShapeDtypeStruct((NS, D), BF16),
        grid=(N_LAT_SEQ, N_HEADS, nq),
        in_specs=[qspec, kspec, kspec, cspec, cspec, _resident((4, HEAD_DIM)), _resident((1, V_DIM))],
        out_specs=qspec,
        compiler_params=_params(("arbitrary", "arbitrary", "arbitrary")),
        name="attn_lat",
    )(q, k, v, cache_k, cache_v, lam_p, subln_g)


def _oproj_route_kernel(oc_ref, ol_ref, x_ref, mod_ref, wo_ref, g_ref, wr_ref, tri_ref, below_ref, x3_ref, h_ref,
                        route_ref, cnt_ref):
    x3 = x_ref[...] + mod_ref[2:3, :] * jnp.dot(_pick_stream(TM, oc_ref, ol_ref), wo_ref[...], preferred_element_type=F32)
    x3_ref[...] = x3
    hb = _modulate(x3, g_ref[...], mod_ref[4:5, :], mod_ref[3:4, :]).astype(BF16)
    h_ref[...] = hb

    logits = jnp.dot(hb, wr_ref[...], preferred_element_type=F32)
    lane = lax.broadcasted_iota(I32, (TM, LANES), 1).astype(F32)
    neg = jnp.float32(-jnp.inf)
    lg = jnp.where(lane < N_EXPERTS, logits, neg)
    v1 = jnp.max(lg, axis=-1, keepdims=True)
    i1 = jnp.min(jnp.where(lg == v1, lane, float(LANES)), axis=-1, keepdims=True)
    lg2 = jnp.where(lane == i1, neg, lg)
    v2 = jnp.max(lg2, axis=-1, keepdims=True)
    i2 = jnp.min(jnp.where(lg2 == v2, lane, float(LANES)), axis=-1, keepdims=True)
    e = jnp.exp(v2 - v1)
    g1 = 1.0 / (1.0 + e)
    g2 = e / (1.0 + e)
    sel1 = lane == i1
    sel2 = lane == i2
    onehot = jnp.where(sel1 | sel2, 1.0, 0.0)
    rank = jnp.dot(tri_ref[...], onehot.astype(BF16), preferred_element_type=F32)
    cnt8 = jnp.floor((jnp.sum(onehot, axis=0, keepdims=True) + (SEG_ALIGN - 1)) * (1.0 / SEG_ALIGN))
    base8 = jnp.dot(jnp.broadcast_to(cnt8, (8, LANES)).astype(BF16), below_ref[...], preferred_element_type=F32)
    slot = base8[0:1, :] * float(SEG_ALIGN) + rank
    pos1 = jnp.sum(jnp.where(sel1, slot, 0.0), axis=-1, keepdims=True)
    pos2 = jnp.sum(jnp.where(sel2, slot, 0.0), axis=-1, keepdims=True)
    cnt_ref[...] = cnt8
    route = jnp.where(lane == 0, i1, 0.0)
    route = jnp.where(lane == 1, i2, route)
    route = jnp.where(lane == 2, g1, route)
    route = jnp.where(lane == 3, g2, route)
    route = jnp.where(lane == 4, pos1, route)
    route = jnp.where(lane == 5, pos2, route)
    route_ref[...] = route


def _oproj_route(o_ctx, o_lat, x, mod, w_o, g, w_r, tb):
    tok = pl.BlockSpec((TM, D), lambda i: (i, 0))
    ctx, lat = _two_stream_specs(TM, D)
    return pl.pallas_call(
        _oproj_route_kernel,
        out_shape=(jax.ShapeDtypeStruct((T, D), F32), jax.ShapeDtypeStruct((T, D), BF16),
                   jax.ShapeDtypeStruct((T, LANES), F32), jax.ShapeDtypeStruct((T // TM, 1, LANES), F32)),
        grid=(T // TM,),
        in_specs=[ctx, lat, tok, _mod_spec(1, TM), _resident((D, D)), _resident((1, D)), _resident((D, LANES)),
                  _resident((TM, TM)), _resident((LANES, LANES))],
        out_specs=(tok, tok, pl.BlockSpec((TM, LANES), lambda i: (i, 0)),
                   pl.BlockSpec((None, 1, LANES), lambda i: (i, 0, 0))),
        compiler_params=_params(("arbitrary",)),
        name="oproj_route",
    )(o_ctx, o_lat, x, mod, w_o, g, w_r, tb["tri"], tb["below"])


def _bit_copies(n, src0, dst0, bits, make_copy, op):
    for bit in bits:
        done = n & (-2 * bit)

        @pl.when((n & bit) != 0)
        def _(done=done, bit=bit):
            src = pl.multiple_of((src0 + done) * SEG_ALIGN, SEG_ALIGN)
            dst = pl.multiple_of((dst0 + done) * SEG_ALIGN, SEG_ALIGN)
            op(make_copy(src, dst, bit * SEG_ALIGN))


def _segment_copies(tab, make_copy, op):
    for e in range(N_EXPERTS):
        _bit_copies(tab(e), tab(N_EXPERTS + e), tab(2 * N_EXPERTS + e), SEG_BITS, make_copy, op)


def _start(copy):
    copy.start()


def _wait(copy):
    copy.wait()


def _dispatch_kernel(tab_ref, fill_ref, h_ref, route_ref, xs_ref, sorted_s, zero_s, sem):
    slots = route_ref[...].T
    row = lax.broadcasted_iota(I32, (L_TILE, TM), 0).astype(F32)
    perm = jnp.where((row == slots[4:5, :]) | (row == slots[5:6, :]), 1.0, 0.0).astype(BF16)
    sorted_s[...] = jnp.dot(perm, h_ref[...], preferred_element_type=F32)

    def seg(src, dst, rows):
        return pltpu.make_async_copy(sorted_s.at[pl.ds(src, rows), :], xs_ref.at[pl.ds(dst, rows), :], sem)

    tab = lambda j: tab_ref[0, j]
    _segment_copies(tab, seg, _start)
    _segment_copies(tab, seg, _wait)

    @pl.when(pl.program_id(0) == 0)
    def _():
        zero_s[...] = jnp.zeros(zero_s.shape, F32)

        def fill(src, dst, rows):
            del src
            return pltpu.make_async_copy(zero_s.at[pl.ds(0, rows), :], xs_ref.at[pl.ds(dst, rows), :], sem)

        def fills(op):
            for e in range(N_EXPERTS):
                _bit_copies(fill_ref[0, N_EXPERTS + e], 0, fill_ref[0, e], SEG_BITS[1:], fill, op)
            for k in range(MAX_UNUSED_TILES):
                @pl.when(k < fill_ref[0, 2 * N_EXPERTS + 1])
                def _(k=k):
                    dst = pl.multiple_of(fill_ref[0, 2 * N_EXPERTS] * SEG_ALIGN + k * T_MOE, SEG_ALIGN)
                    op(fill(0, dst, T_MOE))

        fills(_start)
        fills(_wait)


def _dispatch(tab, fill, h, route):
    smem_tile = pl.BlockSpec((None, 1, LANES), lambda i: (i, 0, 0), memory_space=pltpu.SMEM)
    return pl.pallas_call(
        _dispatch_kernel,
        out_shape=jax.ShapeDtypeStruct((N_ROWS, D), F32),
        grid=(T // TM,),
        in_specs=[
            smem_tile,
            pl.BlockSpec(memory_space=pltpu.SMEM),
            pl.BlockSpec((TM, D), lambda i: (i, 0)),
            pl.BlockSpec((TM, LANES), lambda i: (i, 0)),
        ],
        out_specs=pl.BlockSpec(memory_space=pl.ANY),
        scratch_shapes=[pltpu.VMEM((L_TILE, D), F32), pltpu.VMEM((T_MOE, D), F32), pltpu.SemaphoreType.DMA(())],
        compiler_params=_params(("arbitrary",)),
        name="moe_dispatch",
    )(tab, fill, h, route)


def _moe_kernel(te_ref, tv_ref, xs_ref, w1_ref, w3_ref, w2_ref, y_ref, xb_s, acc_s):
    del te_ref
    i = pl.program_id(0)
    f = pl.program_id(1)
    valid = tv_ref[i] == 1

    @pl.when(f == 0)
    def _():
        xb_s[...] = xs_ref[...].astype(BF16)
        acc_s[...] = jnp.zeros(acc_s.shape, F32)

    @pl.when(valid)
    def _():
        x = xb_s[...]
        a = jnp.dot(x, w1_ref[...], preferred_element_type=F32)
        b = jnp.dot(x, w3_ref[...], preferred_element_type=F32)
        acc_s[...] += jnp.dot((_silu(a) * b).astype(BF16), w2_ref[...], preferred_element_type=F32)

    @pl.when(f == N_F_MOE - 1)
    def _():
        y_ref[...] = acc_s[...]


def _moe(tile_expert, tile_valid, xs, w1, w3, w2):
    def fidx(i, f, te, tv):
        return jnp.where(tv[i] == 1, f, N_F_MOE - 1)

    grid_spec = pltpu.PrefetchScalarGridSpec(
        num_scalar_prefetch=2,
        grid=(N_MOE_TILES, N_F_MOE),
        in_specs=[
            pl.BlockSpec((T_MOE, D), lambda i, f, te, tv: (i, 0)),
            pl.BlockSpec((None, D, F_MOE), lambda i, f, te, tv: (te[i], 0, fidx(i, f, te, tv))),
            pl.BlockSpec((None, D, F_MOE), lambda i, f, te, tv: (te[i], 0, fidx(i, f, te, tv))),
            pl.BlockSpec((None, F_MOE, D), lambda i, f, te, tv: (te[i], fidx(i, f, te, tv), 0)),
        ],
        out_specs=pl.BlockSpec((T_MOE, D), lambda i, f, te, tv: (i, 0)),
        scratch_shapes=[pltpu.VMEM((T_MOE, D), BF16), pltpu.VMEM((T_MOE, D), F32)],
    )
    return pl.pallas_call(
        _moe_kernel,
        out_shape=jax.ShapeDtypeStruct((N_ROWS, D), F32),
        grid_spec=grid_spec,
        compiler_params=_params(("arbitrary", "arbitrary")),
        name="moe_experts",
    )(tile_expert, tile_valid, xs, w1, w3, w2)


def _combine_kernel(tab_ref, y_ref, x_ref, route_ref, mod_ref, fg_ref, o_ref, buf, sem, *, first_tile):
    i = pl.program_id(0)
    slot = i % 2

    def fetch(tile, s, op):
        def seg(src, dst, rows):
            return pltpu.make_async_copy(y_ref.at[pl.ds(dst, rows), :], buf.at[s, pl.ds(src, rows), :], sem.at[s])

        if op is _start:
            buf[s] = jnp.zeros((L_TILE, D), F32)
        _segment_copies(lambda j: tab_ref[tile, j], seg, op)

    @pl.when(i == 0)
    def _():
        fetch(first_tile, 0, _start)

    @pl.when(i + 1 < pl.num_programs(0))
    def _():
        fetch(first_tile + i + 1, 1 - slot, _start)

    fetch(first_tile + i, slot, _wait)
    yb = buf[slot].astype(BF16)
    route = route_ref[...]
    col = lax.broadcasted_iota(I32, (TM, L_TILE), 1).astype(F32)
    picks = [jnp.dot(jnp.where(col == route[:, c:c + 1], 1.0, 0.0).astype(BF16), yb, preferred_element_type=F32)
             for c in (4, 5)]
    moe = route[:, 2:3] * picks[0] + route[:, 3:4] * picks[1]
    x = x_ref[...] + mod_ref[5:6, :] * moe
    ms = jnp.mean(x * x, axis=-1, keepdims=True)
    o_ref[...] = x * lax.rsqrt(ms + EPS) * fg_ref[...]


def _combine(tab, y, x, route, mod, final_g, n_tok, off_tiles):
    tok = lambda w: pl.BlockSpec((TM, w), lambda i: (i + off_tiles, 0))
    return pl.pallas_call(
        functools.partial(_combine_kernel, first_tile=off_tiles),
        out_shape=jax.ShapeDtypeStruct((n_tok, D), F32),
        grid=(n_tok // TM,),
        in_specs=[
            pl.BlockSpec(memory_space=pltpu.SMEM),
            pl.BlockSpec(memory_space=pl.ANY),
            tok(D), tok(LANES), _mod_spec(1, TM, off_tiles), _resident((1, D)),
        ],
        out_specs=pl.BlockSpec((TM, D), lambda i: (i, 0)),
        scratch_shapes=[pltpu.VMEM((2, L_TILE, D), F32), pltpu.SemaphoreType.DMA((2,))],
        compiler_params=_params(("arbitrary",)),
        name="moe_combine",
    )(tab.reshape(T // TM, LANES), y, x, route, mod, final_g)


def kernel(x_prompt, x_sample, cache_k, cache_v, c, c_ctx, ada_w, ada_b, norm1_g, norm2_g, final_g, mix_w_in, sgu_g, sgu_w, sgu_b, mix_w_out, ffn_w1, ffn_w3, ffn_w2, attn_w_qkv, lam_q1, lam_k1, lam_q2, lam_k2, subln_g, attn_w_o, router_w, moe_w1, moe_w3, moe_w2):
    tb = _device_tables()
    xc = x_prompt.reshape(NP, D)
    xl = x_sample.reshape(NS, D)
    cvec = jnp.concatenate([c, c_ctx[None, :], jnp.zeros((3, D), F32)], axis=0)
    mod = _adaln(cvec, ada_w, ada_b)

    bias = jnp.repeat(sgu_b[0].T, GW, axis=1)
    a, y = _inproj(xc, xl, mod, norm1_g[0][None, :], mix_w_in[0].astype(BF16), sgu_g[0].reshape(1, SGU_W),
                   sgu_w[0].astype(BF16), bias, tb)
    w_out = mix_w_out[0].astype(BF16)
    x1c = _mixout_ctx(y, a, xc, mod, w_out, tb)
    ylat = y[NP:]
    y1r = ylat[:, :FNET_W].reshape(N_LAT_SEQ, DFT_N1, DFT_N2 * FNET_W)
    y2r = ylat[:, FNET_W:].reshape(N_LAT_SEQ, DFT_N1, DFT_N2 * FNET_W)
    br, bi = _dft1(y1r, y2r, tb)
    x1l = _mixout_lat(br, bi, a.reshape(T // LAT_LEN, DFT_N2, DFT_N1, SGU_W),
                      xl.reshape(N_LAT_SEQ, DFT_N2, DFT_N1, D), mod, w_out).reshape(NS, D)
    x2 = _ffn(x1c, x1l, mod, norm2_g[0][None, :], ffn_w1[0].astype(BF16), ffn_w3[0].astype(BF16), ffn_w2[0].astype(BF16))

    qkv_c, qkv_l, kf, vf = _qkv(x2, mod, norm1_g[1][None, :], attn_w_qkv[0].astype(BF16), tb)
    lam_p = jnp.stack([lam_q1[0], lam_k1[0], lam_q2[0], lam_k2[0]], axis=0)
    sg = subln_g[0][None, :]
    oc = _attn_ctx(*qkv_c, lam_p, sg)
    ol = _attn_lat(*qkv_l, cache_k.reshape(N_LAT_SEQ, CTX_LEN, D), cache_v.reshape(N_LAT_SEQ, CTX_LEN, D), lam_p, sg)

    w_r = jnp.pad(router_w[0], ((0, 0), (0, LANES - N_EXPERTS))).astype(BF16)
    x3, h2, route, cnt = _oproj_route(oc, ol, x2, mod, attn_w_o[0].astype(BF16), norm2_g[1][None, :], w_r, tb)
    unit_per_tile = T_MOE // SEG_ALIGN
    seg_len = cnt[:, 0, :N_EXPERTS].astype(I32)
    seg_local = jnp.cumsum(seg_len, axis=1) - seg_len
    group_len = jnp.sum(seg_len, axis=0)
    group_tiles = (group_len + unit_per_tile - 1) // unit_per_tile
    tile_ends = jnp.cumsum(group_tiles)
    group_start = (tile_ends - group_tiles) * unit_per_tile
    seg_global = group_start[None, :] + jnp.cumsum(seg_len, axis=0) - seg_len
    tab = jnp.concatenate([seg_len, seg_local, seg_global], axis=1)
    tab = jnp.pad(tab, ((0, 0), (0, LANES - 3 * N_EXPERTS))).reshape(T // TM, 1, LANES)
    fill = jnp.concatenate([group_start + group_len, group_tiles * unit_per_tile - group_len,
                            tile_ends[-1:] * unit_per_tile, N_MOE_TILES - tile_ends[-1:]])
    fill = jnp.pad(fill, (0, LANES - 2 * N_EXPERTS - 2)).reshape(1, LANES)
    tile_idx = jnp.arange(N_MOE_TILES, dtype=I32)
    tile_valid = (tile_idx < tile_ends[-1]).astype(I32)
    tile_expert = jnp.sum((jnp.minimum(tile_idx, tile_ends[-1] - 1)[:, None] >= tile_ends[None, :]).astype(I32), axis=-1)

    xs = _dispatch(tab, fill, h2, route)
    ys = _moe(tile_expert, tile_valid, xs, moe_w1[0].astype(BF16), moe_w3[0].astype(BF16), moe_w2[0].astype(BF16))
    fg = final_g[None, :]
    y_prompt = _combine(tab, ys, x3, route, mod, fg, NP, 0)
    y_sample = _combine(tab, ys, x3, route, mod, fg, NS, NP // TM)

    return (y_prompt.reshape(N_CTX_SEQ, CTX_LEN, D), y_sample.reshape(N_LAT_SEQ, LAT_LEN, D),
            kf.reshape(N_CTX_SEQ, 1, CTX_LEN, N_HEADS, 2 * HEAD_DIM), vf.reshape(N_CTX_SEQ, 1, CTX_LEN, N_HEADS, V_DIM))
```

```python
import functools
import math

import numpy as np
import jax
import jax.numpy as jnp
from jax import lax
from jax.experimental import pallas as pl
from jax.experimental.pallas import tpu as pltpu

F32 = jnp.float32
BF16 = jnp.bfloat16
I32 = jnp.int32

D = 1024
N_CTX_SEQ = 32
CTX_LEN = 256
N_LAT_SEQ = 4
LAT_LEN = 4096
NP = N_CTX_SEQ * CTX_LEN
NS = N_LAT_SEQ * LAT_LEN
T = NP + NS
GRID_W = 64
CHUNK = 128
SGU_W = 512
FNET_W = 512
GW = 128
N_HEADS = 8
HEAD_DIM = 64
V_DIM = 128
ROPE_THETA = 10000.0
D_FF = 2816
N_EXPERTS = 8
D_FF_EXPERT = 3584
EPS = 1e-6
LAM0 = 0.8 - 0.6 * math.exp(-0.3 * 1)
CTX_ROW = 4

LANES = 128
TM = 512
TQ = 1024
TK = 2048
Q_SCALE = HEAD_DIM ** -0.5 * math.log2(math.e)
T_MOE = 512
F_MOE = 1792
N_F_MOE = D_FF_EXPERT // F_MOE
SEG_ALIGN = 8
SEG_BITS = (64, 32, 16, 8, 4, 2, 1)
L_TILE = 1152
N_ROWS = -(-(2 * T + (T // TM) * N_EXPERTS * (SEG_ALIGN - 1) + N_EXPERTS * (T_MOE - SEG_ALIGN)) // T_MOE) * T_MOE
N_MOE_TILES = N_ROWS // T_MOE
MAX_UNUSED_TILES = N_MOE_TILES - 2 * T // T_MOE
FFN_CHUNK = 1408
DFT_N1 = 256
DFT_N2 = 16
DFT_ROWS = 32
VMEM_LIMIT = 56 * 2 ** 20


def _params(sem, vmem=VMEM_LIMIT):
    return pltpu.CompilerParams(dimension_semantics=sem, vmem_limit_bytes=vmem)


def _resident(shape):
    nd = len(shape)
    return pl.BlockSpec(shape, lambda *_: (0,) * nd, pipeline_mode=pl.Buffered(1))


def _mod_row(i, tm):
    npt = NP // tm
    return jnp.where(i < npt, CTX_ROW, (i - npt) // (LAT_LEN // tm))


def _mod_spec(layer, tm, offset_tiles=0):
    return pl.BlockSpec((None, None, 6, D), lambda i, *_: (layer, _mod_row(i + offset_tiles, tm), 0, 0))


def _two_stream_specs(tm, width):
    npt = NP // tm
    ctx = pl.BlockSpec((tm, width), lambda i: (jnp.minimum(i, npt - 1), 0))
    lat = pl.BlockSpec((tm, width), lambda i: (jnp.maximum(i - npt, 0), 0))
    return ctx, lat


def _pick_stream(tm, ctx_ref, lat_ref):
    return jnp.where(pl.program_id(0) < NP // tm, ctx_ref[...], lat_ref[...])


def _modulate(x, g, scale, shift):
    ms = jnp.mean(x * x, axis=-1, keepdims=True)
    return x * lax.rsqrt(ms + EPS) * (g * (1.0 + scale)) + shift


def _silu(a):
    return a * jax.nn.sigmoid(a)


def _gelu_tanh(x):
    return 0.5 * x * (1.0 + jnp.tanh(0.7978845608028654 * (x + 0.044715 * (x * x * x))))


@functools.lru_cache(maxsize=None)
def _tables():
    def cs(n):
        k = np.arange(n, dtype=np.int64)
        ang = 2.0 * np.pi * ((k[:, None] * k[None, :]) % n) / n
        return np.cos(ang), np.sin(ang)

    c128, s128 = cs(GW)
    dft_ch = np.concatenate([c128, s128], axis=1)
    c256, s256 = cs(CTX_LEN)
    dft_ctx = np.concatenate([c256, -s256], axis=1)
    k1 = np.arange(DFT_N1, dtype=np.int64)[:, None]
    n2 = np.arange(DFT_N2, dtype=np.int64)[None, :]
    tw = 2.0 * np.pi * ((k1 * n2) % LAT_LEN) / LAT_LEN
    half = HEAD_DIM // 2
    inv_freq = ROPE_THETA ** (-np.arange(0, half, 2, dtype=np.float64) / half)
    t = np.arange(LAT_LEN)
    ang_r = (t // GRID_W).astype(np.float64)[:, None] * inv_freq[None, :]
    ang_c = (t % GRID_W).astype(np.float64)[:, None] * inv_freq[None, :]

    def blk(ang):
        c = np.cos(ang)
        s = np.sin(ang)
        return np.concatenate([c, c], axis=1), np.concatenate([-s, s], axis=1)

    cr, sr = blk(ang_r)
    cc, sc = blk(ang_c)
    cos64 = np.concatenate([cr, cc], axis=1)
    sin64 = np.concatenate([sr, sc], axis=1)
    rope_cos = np.concatenate([cos64, cos64], axis=1).astype(np.float32)
    rope_sin = np.concatenate([sin64, sin64], axis=1).astype(np.float32)
    tri = np.tril(np.ones((TM, TM), np.float32), k=-1)
    below = np.triu(np.ones((LANES, LANES), np.float32), k=1)
    f32 = lambda a: np.asarray(a, np.float32)
    return dict(dft_ch=f32(dft_ch), dft_ctx=f32(dft_ctx), c256=f32(c256), s256=f32(s256),
                tw_cos=f32(np.cos(tw)), tw_sin=f32(np.sin(tw)),
                rope_cos=rope_cos, rope_sin=rope_sin, tri=tri, below=below)


def _device_tables():
    tb = {k: jnp.asarray(v) for k, v in _tables().items()}
    for k in ("dft_ch", "dft_ctx", "c256", "s256", "tri", "below"):
        tb[k] = tb[k].astype(BF16)
    return tb


def _adaln_kernel(c_ref, w_ref, b_ref, o_ref):
    s = _silu(c_ref[...]).astype(BF16)
    o_ref[...] = jnp.dot(s, w_ref[...].astype(BF16), preferred_element_type=F32) + b_ref[...]


def _adaln(cvec, ada_w, ada_b):
    tn = 1536
    out = pl.pallas_call(
        _adaln_kernel,
        out_shape=jax.ShapeDtypeStruct((2, 8, 6 * D), F32),
        grid=(2, 6 * D // tn),
        in_specs=[
            pl.BlockSpec((8, D), lambda l, j: (0, 0)),
            pl.BlockSpec((None, D, tn), lambda l, j: (l, 0, j)),
            pl.BlockSpec((None, 1, tn), lambda l, j: (l, 0, j)),
        ],
        out_specs=pl.BlockSpec((None, 8, tn), lambda l, j: (l, 0, j)),
        compiler_params=_params(("arbitrary", "arbitrary")),
        name="adaln",
    )(cvec, ada_w, ada_b.reshape(2, 1, 6 * D))
    return out.reshape(2, 8, 6, D)


def _inproj_kernel(xc_ref, xl_ref, mod_ref, g_ref, win_ref, sgug_ref, sguw_ref, sgub_ref, dft_ref, a_ref, y_ref):
    x = _pick_stream(TM, xc_ref, xl_ref)
    h = _modulate(x, g_ref[...], mod_ref[1:2, :], mod_ref[0:1, :]).astype(BF16)
    p = jnp.dot(h, win_ref[...], preferred_element_type=F32)
    act = _gelu_tanh(p[:, :2 * SGU_W])
    for g in range(4):
        lo, hi = g * GW, (g + 1) * GW
        u = act[:, lo:hi]
        v = act[:, SGU_W + lo:SGU_W + hi]
        ms = jnp.mean(v * v, axis=-1, keepdims=True)
        vn = (v * lax.rsqrt(ms + EPS) * sgug_ref[:, lo:hi]).astype(BF16)
        w = sguw_ref[g]
        for c in range(TM // CHUNK):
            r0, r1 = c * CHUNK, (c + 1) * CHUNK
            mix = jnp.dot(w, vn[r0:r1, :], preferred_element_type=F32) + sgub_ref[:, lo:hi]
            a_ref[r0:r1, lo:hi] = (u[r0:r1, :] * mix).astype(BF16)
        fg = p[:, 2 * SGU_W + lo:2 * SGU_W + hi].astype(BF16)
        yy = jnp.dot(fg, dft_ref[...], preferred_element_type=F32)
        y_ref[:, lo:hi] = yy[:, :GW].astype(BF16)
        y_ref[:, FNET_W + lo:FNET_W + hi] = yy[:, GW:].astype(BF16)


def _inproj(x_ctx, x_lat, mod, g, w_in, sgu_g, sgu_w, sgu_b, tb):
    ctx, lat = _two_stream_specs(TM, D)
    return pl.pallas_call(
        _inproj_kernel,
        out_shape=(jax.ShapeDtypeStruct((T, SGU_W), BF16), jax.ShapeDtypeStruct((T, 2 * FNET_W), BF16)),
        grid=(T // TM,),
        in_specs=[
            ctx, lat,
            _mod_spec(0, TM),
            _resident((1, D)),
            _resident((D, 3 * SGU_W)),
            _resident((1, SGU_W)),
            _resident((4, CHUNK, CHUNK)),
            _resident((CHUNK, SGU_W)),
            _resident((GW, 2 * GW)),
        ],
        out_specs=(pl.BlockSpec((TM, SGU_W), lambda i: (i, 0)), pl.BlockSpec((TM, 2 * FNET_W), lambda i: (i, 0))),
        compiler_params=_params(("arbitrary",)),
        name="inproj_mix",
    )(x_ctx, x_lat, mod, g, w_in, sgu_g, sgu_w, sgu_b, tb["dft_ch"])


def _mix_out(f, a_ref, x_ref, mod_ref, wout_ref, o_ref):
    mix = (jnp.dot(a_ref[...], wout_ref[:SGU_W, :], preferred_element_type=F32)
           + jnp.dot(f.astype(BF16), wout_ref[SGU_W:, :], preferred_element_type=F32))
    o_ref[...] = x_ref[...] + mod_ref[2:3, :] * mix


def _mixout_ctx_kernel(y_ref, a_ref, x_ref, mod_ref, dft_ref, wout_ref, o_ref):
    f = (jnp.dot(dft_ref[:, :CTX_LEN], y_ref[:, :FNET_W], preferred_element_type=F32)
         + jnp.dot(dft_ref[:, CTX_LEN:], y_ref[:, FNET_W:], preferred_element_type=F32))
    _mix_out(f * (1.0 / math.sqrt(CTX_LEN * GW)), a_ref, x_ref, mod_ref, wout_ref, o_ref)


def _mixout_ctx(y, a, x, mod, w_out, tb):
    return pl.pallas_call(
        _mixout_ctx_kernel,
        out_shape=jax.ShapeDtypeStruct((NP, D), F32),
        grid=(N_CTX_SEQ,),
        in_specs=[
            pl.BlockSpec((CTX_LEN, 2 * FNET_W), lambda i: (i, 0)),
            pl.BlockSpec((CTX_LEN, SGU_W), lambda i: (i, 0)),
            pl.BlockSpec((CTX_LEN, D), lambda i: (i, 0)),
            pl.BlockSpec((None, None, 6, D), lambda i: (0, CTX_ROW, 0, 0)),
            _resident((CTX_LEN, 2 * CTX_LEN)),
            _resident((D, D)),
        ],
        out_specs=pl.BlockSpec((CTX_LEN, D), lambda i: (i, 0)),
        compiler_params=_params(("arbitrary",)),
        name="mixout_ctx",
    )(y, a, x, mod, tb["dft_ctx"], w_out)


def _dft1_kernel(y1_ref, y2_ref, c_ref, s_ref, twc_ref, tws_ref, br_ref, bi_ref):
    y1 = y1_ref[...]
    y2 = y2_ref[...]
    c = c_ref[...]
    s = s_ref[...]
    ar = jnp.dot(c, y1, preferred_element_type=F32) - jnp.dot(s, y2, preferred_element_type=F32)
    ai = -(jnp.dot(c, y2, preferred_element_type=F32) + jnp.dot(s, y1, preferred_element_type=F32))
    n2 = pl.program_id(1)
    lane = lax.broadcasted_iota(I32, (DFT_N1, DFT_N2), 1)
    tc = jnp.sum(jnp.where(lane == n2, twc_ref[...], 0.0), axis=-1, keepdims=True)
    ts = jnp.sum(jnp.where(lane == n2, tws_ref[...], 0.0), axis=-1, keepdims=True)
    br_ref[...] = (tc * ar + ts * ai).astype(BF16)
    bi_ref[...] = (tc * ai - ts * ar).astype(BF16)


def _dft1(y1r, y2r, tb):
    shp = jax.ShapeDtypeStruct((N_LAT_SEQ, DFT_N2, DFT_N1, FNET_W), BF16)
    return pl.pallas_call(
        _dft1_kernel,
        out_shape=(shp, shp),
        grid=(N_LAT_SEQ, DFT_N2),
        in_specs=[
            pl.BlockSpec((None, DFT_N1, FNET_W), lambda b, n: (b, 0, n)),
            pl.BlockSpec((None, DFT_N1, FNET_W), lambda b, n: (b, 0, n)),
            _resident((DFT_N1, DFT_N1)),
            _resident((DFT_N1, DFT_N1)),
            _resident((DFT_N1, DFT_N2)),
            _resident((DFT_N1, DFT_N2)),
        ],
        out_specs=(pl.BlockSpec((None, None, DFT_N1, FNET_W), lambda b, n: (b, n, 0, 0)),
                   pl.BlockSpec((None, None, DFT_N1, FNET_W), lambda b, n: (b, n, 0, 0))),
        compiler_params=_params(("arbitrary", "arbitrary")),
        name="dft_stage1",
    )(y1r, y2r, tb["c256"], tb["s256"], tb["tw_cos"], tb["tw_sin"])


def _cmul_const(z, wr, wi):
    re, im = z
    tol = 1e-12
    if abs(wi) < tol:
        return (re, im) if wr > 0 else (-re, -im)
    if abs(wr) < tol:
        return (-im, re) if wi > 0 else (im, -re)
    return (re * wr - im * wi, re * wi + im * wr)


def _fft(zs):
    n = len(zs)
    if n == 1:
        return zs
    even, odd = _fft(zs[0::2]), _fft(zs[1::2])
    out = [None] * n
    for k in range(n // 2):
        ang = 2.0 * math.pi * k / n
        t = _cmul_const(odd[k], math.cos(ang), -math.sin(ang))
        out[k] = (even[k][0] + t[0], even[k][1] + t[1])
        out[k + n // 2] = (even[k][0] - t[0], even[k][1] - t[1])
    return out


def _mixout_lat_kernel(br_ref, bi_ref, a_ref, x_ref, mod_ref, wout_ref, o_ref):
    spec = _fft([(br_ref[n2].astype(F32), bi_ref[n2].astype(F32)) for n2 in range(DFT_N2)])
    f = jnp.concatenate([re for re, _ in spec], axis=0) * (1.0 / math.sqrt(LAT_LEN * GW))
    rows = DFT_N2 * DFT_ROWS
    mix = (jnp.dot(a_ref[...].reshape(rows, SGU_W), wout_ref[:SGU_W, :], preferred_element_type=F32)
           + jnp.dot(f.astype(BF16), wout_ref[SGU_W:, :], preferred_element_type=F32))
    out = x_ref[...].reshape(rows, D) + mod_ref[2:3, :] * mix
    o_ref[...] = out.reshape(DFT_N2, DFT_ROWS, D)


def _mixout_lat(br, bi, a_lat, x, mod, w_out):
    blk = lambda w, off=0: pl.BlockSpec((None, DFT_N2, DFT_ROWS, w), lambda b, r: (b + off, 0, r, 0))
    return pl.pallas_call(
        _mixout_lat_kernel,
        out_shape=jax.ShapeDtypeStruct((N_LAT_SEQ, DFT_N2, DFT_N1, D), F32),
        grid=(N_LAT_SEQ, DFT_N1 // DFT_ROWS),
        in_specs=[blk(FNET_W), blk(FNET_W), blk(SGU_W, NP // LAT_LEN), blk(D),
                  pl.BlockSpec((None, None, 6, D), lambda b, r: (0, b, 0, 0)),
                  _resident((D, D))],
        out_specs=blk(D),
        compiler_params=_params(("arbitrary", "arbitrary")),
        name="mixout_lat",
    )(br, bi, a_lat, x, mod, w_out)


def _ffn_kernel(xc_ref, xl_ref, mod_ref, g_ref, w1_ref, w3_ref, w2_ref, o_ref):
    x = _pick_stream(TM, xc_ref, xl_ref)
    h = _modulate(x, g_ref[...], mod_ref[4:5, :], mod_ref[3:4, :]).astype(BF16)
    acc = None
    for c in range(D_FF // FFN_CHUNK):
        lo, hi = c * FFN_CHUNK, (c + 1) * FFN_CHUNK
        a = jnp.dot(h, w1_ref[:, lo:hi], preferred_element_type=F32)
        b = jnp.dot(h, w3_ref[:, lo:hi], preferred_element_type=F32)
        d = jnp.dot((_silu(a) * b).astype(BF16), w2_ref[lo:hi, :], preferred_element_type=F32)
        acc = d if acc is None else acc + d
    o_ref[...] = x + mod_ref[5:6, :] * acc


def _ffn(x_ctx, x_lat, mod, g, w1, w3, w2):
    ctx, lat = _two_stream_specs(TM, D)
    return pl.pallas_call(
        _ffn_kernel,
        out_shape=jax.ShapeDtypeStruct((T, D), F32),
        grid=(T // TM,),
        in_specs=[
            ctx, lat,
            _mod_spec(0, TM),
            _resident((1, D)),
            _resident((D, D_FF)),
            _resident((D, D_FF)),
            _resident((D_FF, D)),
        ],
        out_specs=pl.BlockSpec((TM, D), lambda i: (i, 0)),
        compiler_params=_params(("arbitrary",)),
        name="ffn",
    )(x_ctx, x_lat, mod, g, w1, w3, w2)


def _qkv_ctx_kernel(x_ref, mod_ref, g_ref, w_ref, q_ref, k_ref, v_ref, kf_ref, vf_ref):
    h = _modulate(x_ref[...], g_ref[...], mod_ref[1:2, :], mod_ref[0:1, :]).astype(BF16)
    qkv = jnp.dot(h, w_ref[...], preferred_element_type=F32)
    k = qkv[:, D:2 * D]
    v = qkv[:, 2 * D:]
    q_ref[...] = (qkv[:, :D] * Q_SCALE).astype(BF16)
    k_ref[...] = k.astype(BF16)
    v_ref[...] = v.astype(BF16)
    kf_ref[...] = k
    vf_ref[...] = v


def _qkv_lat_kernel(x_ref, mod_ref, g_ref, w_ref, cos_ref, sin_ref, q_ref, k_ref, v_ref):
    h = _modulate(x_ref[...], g_ref[...], mod_ref[1:2, :], mod_ref[0:1, :]).astype(BF16)
    qkv = jnp.dot(h, w_ref[...], preferred_element_type=F32)
    cos = cos_ref[...]
    sin = sin_ref[...]
    first = (lax.broadcasted_iota(I32, (TM, LANES), 1) & 31) < 16

    def rotate(xh):
        partner = jnp.where(first, pltpu.roll(xh, LANES - 16, 1), pltpu.roll(xh, 16, 1))
        return xh * cos + partner * sin

    for hh in range(N_HEADS):
        sl = slice(hh * LANES, (hh + 1) * LANES)
        q_ref[:, sl] = (rotate(qkv[:, sl]) * Q_SCALE).astype(BF16)
        k_ref[:, sl] = rotate(qkv[:, D + hh * LANES:D + (hh + 1) * LANES]).astype(BF16)
    v_ref[...] = qkv[:, 2 * D:].astype(BF16)


def _qkv(x, mod, g, w_qkv, tb):
    tok = pl.BlockSpec((TM, D), lambda i: (i, 0))
    bshape = jax.ShapeDtypeStruct((NP, D), BF16)
    lshape = jax.ShapeDtypeStruct((NS, D), BF16)
    fshape = jax.ShapeDtypeStruct((NP, D), F32)
    qc, kc, vc, kf, vf = pl.pallas_call(
        _qkv_ctx_kernel,
        out_shape=(bshape, bshape, bshape, fshape, fshape),
        grid=(NP // TM,),
        in_specs=[tok, _mod_spec(1, TM), _resident((1, D)), _resident((D, 3 * D))],
        out_specs=(tok, tok, tok, tok, tok),
        compiler_params=_params(("arbitrary",)),
        name="qkv_ctx",
    )(x, mod, g, w_qkv)
    off = NP // TM
    ltok = pl.BlockSpec((TM, D), lambda i: (i + off, 0))
    rope = pl.BlockSpec((TM, LANES), lambda i: (i % (LAT_LEN // TM), 0))
    ql, kl, vl = pl.pallas_call(
        _qkv_lat_kernel,
        out_shape=(lshape, lshape, lshape),
        grid=(NS // TM,),
        in_specs=[ltok, _mod_spec(1, TM, off), _resident((1, D)), _resident((D, 3 * D)), rope, rope],
        out_specs=(tok, tok, tok),
        compiler_params=_params(("arbitrary",)),
        name="qkv_lat",
    )(x, mod, g, w_qkv, tb["rope_cos"], tb["rope_sin"])
    return (qc, kc, vc), (ql, kl, vl), kf, vf


def _lam(lp_ref):
    lp = lp_ref[...]
    a = jnp.sum(lp[0:1, :] * lp[1:2, :], axis=-1, keepdims=True)
    b = jnp.sum(lp[2:3, :] * lp[3:4, :], axis=-1, keepdims=True)
    return jnp.exp(a) - jnp.exp(b) + LAM0


def _head_norm(o, sg):
    ms = jnp.mean(o * o, axis=-1, keepdims=True)
    return o * lax.rsqrt(ms + EPS) * (sg * (1.0 - LAM0))


_NT = (((1,), (1,)), ((), ()))


def _attn_ctx_kernel(q_ref, k_ref, v_ref, lp_ref, sg_ref, o_ref):
    lam = _lam(lp_ref)
    map0 = lax.broadcasted_iota(I32, (CTX_LEN, LANES), 1) < HEAD_DIM
    init = _softmax_init(CTX_LEN)
    for hh in range(N_HEADS):
        sl = slice(hh * LANES, (hh + 1) * LANES)
        q = q_ref[:, sl]
        zero = jnp.zeros_like(q)
        outs = []
        for qc in (jnp.where(map0, q, zero), jnp.where(map0, zero, q)):
            _, l, acc = _softmax_step(qc, k_ref[:, sl], v_ref[:, sl], init)
            outs.append(acc / jnp.sum(l, axis=-1, keepdims=True))
        o_ref[:, sl] = _head_norm(outs[0] - lam * outs[1], sg_ref[...]).astype(BF16)


def _attn_ctx(q, k, v, lam_p, subln_g):
    seq = pl.BlockSpec((CTX_LEN, D), lambda i: (i, 0))
    return pl.pallas_call(
        _attn_ctx_kernel,
        out_shape=jax.ShapeDtypeStruct((NP, D), BF16),
        grid=(N_CTX_SEQ,),
        in_specs=[seq, seq, seq, _resident((4, HEAD_DIM)), _resident((1, V_DIM))],
        out_specs=seq,
        compiler_params=_params(("arbitrary",)),
        name="attn_ctx",
    )(q, k, v, lam_p, subln_g)


def _softmax_init(rows):
    return (jnp.full((rows, LANES), -jnp.inf, F32), jnp.zeros((rows, LANES), F32), jnp.zeros((rows, LANES), F32))


def _softmax_step(qc, kc, vc, state):
    m, l, acc = state
    s = lax.dot_general(qc, kc, _NT, preferred_element_type=F32)
    blocks = [s[:, j * LANES:(j + 1) * LANES] for j in range(s.shape[1] // LANES)]
    bm = functools.reduce(jnp.maximum, blocks)
    m_new = jnp.maximum(m, jnp.max(bm, axis=-1, keepdims=True))
    alpha = jnp.exp2(m - m_new)
    ps = [jnp.exp2(b - m_new) for b in blocks]
    l_new = alpha * l + functools.reduce(jnp.add, ps)
    p = jnp.concatenate(ps, axis=1).astype(BF16)
    acc_new = alpha * acc + jnp.dot(p, vc, preferred_element_type=F32)
    return m_new, l_new, acc_new


def _attn_lat_kernel(q_ref, k_ref, v_ref, ck_ref, cv_ref, lp_ref, sg_ref, o_ref):
    q = q_ref[...]
    zero = jnp.zeros_like(q)
    map0 = lax.broadcasted_iota(I32, (TQ, LANES), 1) < HEAD_DIM
    qs = (jnp.where(map0, q, zero), jnp.where(map0, zero, q))
    init = _softmax_init(TQ)
    states = [init, init]
    chunks = [(ck_ref[...].astype(BF16), cv_ref[...].astype(BF16))]
    chunks += [(k_ref[j * TK:(j + 1) * TK, :], v_ref[j * TK:(j + 1) * TK, :]) for j in range(LAT_LEN // TK)]
    for kc, vc in chunks:
        states = [_softmax_step(qs[c], kc, vc, states[c]) for c in range(2)]
    outs = [acc / jnp.sum(l, axis=-1, keepdims=True) for _, l, acc in states]
    o = outs[0] - _lam(lp_ref) * outs[1]
    o_ref[...] = _head_norm(o, sg_ref[...]).astype(BF16)


def _attn_lat(q, k, v, cache_k, cache_v, lam_p, subln_g):
    nq = LAT_LEN // TQ
    qspec = pl.BlockSpec((TQ, LANES), lambda b, h, i: (b * nq + i, h))
    kspec = pl.BlockSpec((LAT_LEN, LANES), lambda b, h, i: (b, h))
    cspec = pl.BlockSpec((None, CTX_LEN, LANES), lambda b, h, i: (b, 0, h))
    return pl.pallas_call(
        _attn_lat_kernel,
        out_shape=jax.ShapeDtypeStruct((NS, D), BF16),
        grid=(N_LAT_SEQ, N_HEADS, nq),
        in_specs=[qspec, kspec, kspec, cspec, cspec, _resident((4, HEAD_DIM)), _resident((1, V_DIM))],
        out_specs=qspec,
        compiler_params=_params(("arbitrary", "arbitrary", "arbitrary")),
        name="attn_lat",
    )(q, k, v, cache_k, cache_v, lam_p, subln_g)


def _oproj_route_kernel(oc_ref, ol_ref, x_ref, mod_ref, wo_ref, g_ref, wr_ref, tri_ref, below_ref, x3_ref, h_ref,
                        route_ref, cnt_ref):
    x3 = x_ref[...] + mod_ref[2:3, :] * jnp.dot(_pick_stream(TM, oc_ref, ol_ref), wo_ref[...], preferred_element_type=F32)
    x3_ref[...] = x3
    hb = _modulate(x3, g_ref[...], mod_ref[4:5, :], mod_ref[3:4, :]).astype(BF16)
    h_ref[...] = hb

    logits = jnp.dot(hb, wr_ref[...], preferred_element_type=F32)
    lane = lax.broadcasted_iota(I32, (TM, LANES), 1).astype(F32)
    neg = jnp.float32(-jnp.inf)
    lg = jnp.where(lane < N_EXPERTS, logits, neg)
    v1 = jnp.max(lg, axis=-1, keepdims=True)
    i1 = jnp.min(jnp.where(lg == v1, lane, float(LANES)), axis=-1, keepdims=True)
    lg2 = jnp.where(lane == i1, neg, lg)
    v2 = jnp.max(lg2, axis=-1, keepdims=True)
    i2 = jnp.min(jnp.where(lg2 == v2, lane, float(LANES)), axis=-1, keepdims=True)
    e = jnp.exp(v2 - v1)
    g1 = 1.0 / (1.0 + e)
    g2 = e / (1.0 + e)
    sel1 = lane == i1
    sel2 = lane == i2
    onehot = jnp.where(sel1 | sel2, 1.0, 0.0)
    rank = jnp.dot(tri_ref[...], onehot.astype(BF16), preferred_element_type=F32)
    cnt8 = jnp.floor((jnp.sum(onehot, axis=0, keepdims=True) + (SEG_ALIGN - 1)) * (1.0 / SEG_ALIGN))
    base8 = jnp.dot(jnp.broadcast_to(cnt8, (8, LANES)).astype(BF16), below_ref[...], preferred_element_type=F32)
    slot = base8[0:1, :] * float(SEG_ALIGN) + rank
    pos1 = jnp.sum(jnp.where(sel1, slot, 0.0), axis=-1, keepdims=True)
    pos2 = jnp.sum(jnp.where(sel2, slot, 0.0), axis=-1, keepdims=True)
    cnt_ref[...] = cnt8
    route = jnp.where(lane == 0, i1, 0.0)
    route = jnp.where(lane == 1, i2, route)
    route = jnp.where(lane == 2, g1, route)
    route = jnp.where(lane == 3, g2, route)
    route = jnp.where(lane == 4, pos1, route)
    route = jnp.where(lane == 5, pos2, route)
    route_ref[...] = route


def _oproj_route(o_ctx, o_lat, x, mod, w_o, g, w_r, tb):
    tok = pl.BlockSpec((TM, D), lambda i: (i, 0))
    ctx, lat = _two_stream_specs(TM, D)
    return pl.pallas_call(
        _oproj_route_kernel,
        out_shape=(jax.ShapeDtypeStruct((T, D), F32), jax.ShapeDtypeStruct((T, D), BF16),
                   jax.ShapeDtypeStruct((T, LANES), F32), jax.ShapeDtypeStruct((T // TM, 1, LANES), F32)),
        grid=(T // TM,),
        in_specs=[ctx, lat, tok, _mod_spec(1, TM), _resident((D, D)), _resident((1, D)), _resident((D, LANES)),
                  _resident((TM, TM)), _resident((LANES, LANES))],
        out_specs=(tok, tok, pl.BlockSpec((TM, LANES), lambda i: (i, 0)),
                   pl.BlockSpec((None, 1, LANES), lambda i: (i, 0, 0))),
        compiler_params=_params(("arbitrary",)),
        name="oproj_route",
    )(o_ctx, o_lat, x, mod, w_o, g, w_r, tb["tri"], tb["below"])


def _bit_copies(n, src0, dst0, bits, make_copy, op):
    for bit in bits:
        done = n & (-2 * bit)

        @pl.when((n & bit) != 0)
        def _(done=done, bit=bit):
            src = pl.multiple_of((src0 + done) * SEG_ALIGN, SEG_ALIGN)
            dst = pl.multiple_of((dst0 + done) * SEG_ALIGN, SEG_ALIGN)
            op(make_copy(src, dst, bit * SEG_ALIGN))


def _segment_copies(tab, make_copy, op):
    for e in range(N_EXPERTS):
        _bit_copies(tab(e), tab(N_EXPERTS + e), tab(2 * N_EXPERTS + e), SEG_BITS, make_copy, op)


def _start(copy):
    copy.start()


def _wait(copy):
    copy.wait()


def _dispatch_kernel(tab_ref, fill_ref, h_ref, route_ref, xs_ref, sorted_s, zero_s, sem, fill_sem):
    i = pl.program_id(0)
    slot = i % 2
    slots = route_ref[...].T
    row = lax.broadcasted_iota(I32, (L_TILE, TM), 0).astype(F32)
    perm = jnp.where((row == slots[4:5, :]) | (row == slots[5:6, :]), 1.0, 0.0).astype(BF16)
    sorted_s[slot] = jnp.dot(perm, h_ref[...], preferred_element_type=F32)

    def copies(tile, s, op):
        def seg(src, dst, rows):
            return pltpu.make_async_copy(sorted_s.at[s, pl.ds(src, rows), :], xs_ref.at[pl.ds(dst, rows), :], sem.at[s])

        _segment_copies(lambda j: tab_ref[tile, j], seg, op)

    copies(i, slot, _start)

    @pl.when(i > 0)
    def _():
        copies(i - 1, 1 - slot, _wait)

    @pl.when(i == pl.num_programs(0) - 1)
    def _():
        copies(i, slot, _wait)

    @pl.when(i == 0)
    def _():
        zero_s[...] = jnp.zeros(zero_s.shape, F32)

        def fill(src, dst, rows):
            del src
            return pltpu.make_async_copy(zero_s.at[pl.ds(0, rows), :], xs_ref.at[pl.ds(dst, rows), :], fill_sem)

        def fills(op):
            for e in range(N_EXPERTS):
                _bit_copies(fill_ref[0, N_EXPERTS + e], 0, fill_ref[0, e], SEG_BITS[1:], fill, op)
            for k in range(MAX_UNUSED_TILES):
                @pl.when(k < fill_ref[0, 2 * N_EXPERTS + 1])
                def _(k=k):
                    dst = pl.multiple_of(fill_ref[0, 2 * N_EXPERTS] * SEG_ALIGN + k * T_MOE, SEG_ALIGN)
                    op(fill(0, dst, T_MOE))

        fills(_start)
        fills(_wait)


def _dispatch(tab, fill, h, route):
    smem = pl.BlockSpec(memory_space=pltpu.SMEM)
    return pl.pallas_call(
        _dispatch_kernel,
        out_shape=jax.ShapeDtypeStruct((N_ROWS, D), F32),
        grid=(T // TM,),
        in_specs=[smem, smem, pl.BlockSpec((TM, D), lambda i: (i, 0)), pl.BlockSpec((TM, LANES), lambda i: (i, 0))],
        out_specs=pl.BlockSpec(memory_space=pl.ANY),
        scratch_shapes=[pltpu.VMEM((2, L_TILE, D), F32), pltpu.VMEM((T_MOE, D), F32),
                        pltpu.SemaphoreType.DMA((2,)), pltpu.SemaphoreType.DMA(())],
        compiler_params=_params(("arbitrary",)),
        name="moe_dispatch",
    )(tab, fill, h, route)


def _moe_kernel(te_ref, tv_ref, xs_ref, w1_ref, w3_ref, w2_ref, y_ref, xb_s, acc_s):
    del te_ref
    i = pl.program_id(0)
    f = pl.program_id(1)
    valid = tv_ref[i] == 1

    @pl.when(f == 0)
    def _():
        xb_s[...] = xs_ref[...].astype(BF16)
        acc_s[...] = jnp.zeros(acc_s.shape, F32)

    @pl.when(valid)
    def _():
        x = xb_s[...]
        a = jnp.dot(x, w1_ref[...], preferred_element_type=F32)
        b = jnp.dot(x, w3_ref[...], preferred_element_type=F32)
        acc_s[...] += jnp.dot((_silu(a) * b).astype(BF16), w2_ref[...], preferred_element_type=F32)

    @pl.when(f == N_F_MOE - 1)
    def _():
        y_ref[...] = acc_s[...]


def _moe(tile_expert, tile_valid, xs, w1, w3, w2):
    def fidx(i, f, te, tv):
        return jnp.where(tv[i] == 1, f, N_F_MOE - 1)

    grid_spec = pltpu.PrefetchScalarGridSpec(
        num_scalar_prefetch=2,
        grid=(N_MOE_TILES, N_F_MOE),
        in_specs=[
            pl.BlockSpec((T_MOE, D), lambda i, f, te, tv: (i, 0)),
            pl.BlockSpec((None, D, F_MOE), lambda i, f, te, tv: (te[i], 0, fidx(i, f, te, tv))),
            pl.BlockSpec((None, D, F_MOE), lambda i, f, te, tv: (te[i], 0, fidx(i, f, te, tv))),
            pl.BlockSpec((None, F_MOE, D), lambda i, f, te, tv: (te[i], fidx(i, f, te, tv), 0)),
        ],
        out_specs=pl.BlockSpec((T_MOE, D), lambda i, f, te, tv: (i, 0)),
        scratch_shapes=[pltpu.VMEM((T_MOE, D), BF16), pltpu.VMEM((T_MOE, D), F32)],
    )
    return pl.pallas_call(
        _moe_kernel,
        out_shape=jax.ShapeDtypeStruct((N_ROWS, D), F32),
        grid_spec=grid_spec,
        compiler_params=_params(("arbitrary", "arbitrary")),
        name="moe_experts",
    )(tile_expert, tile_valid, xs, w1, w3, w2)


def _combine_kernel(tab_ref, y_ref, x_ref, route_ref, mod_ref, fg_ref, o_ref, buf, sem, *, first_tile):
    i = pl.program_id(0)
    slot = i % 2

    def fetch(tile, s, op):
        def seg(src, dst, rows):
            return pltpu.make_async_copy(y_ref.at[pl.ds(dst, rows), :], buf.at[s, pl.ds(src, rows), :], sem.at[s])

        if op is _start:
            buf[s] = jnp.zeros((L_TILE, D), F32)
        _segment_copies(lambda j: tab_ref[tile, j], seg, op)

    @pl.when(i == 0)
    def _():
        fetch(first_tile, 0, _start)

    @pl.when(i + 1 < pl.num_programs(0))
    def _():
        fetch(first_tile + i + 1, 1 - slot, _start)

    fetch(first_tile + i, slot, _wait)
    yb = buf[slot].astype(BF16)
    route = route_ref[...]
    col = lax.broadcasted_iota(I32, (TM, L_TILE), 1).astype(F32)
    picks = [jnp.dot(jnp.where(col == route[:, c:c + 1], 1.0, 0.0).astype(BF16), yb, preferred_element_type=F32)
             for c in (4, 5)]
    moe = route[:, 2:3] * picks[0] + route[:, 3:4] * picks[1]
    x = x_ref[...] + mod_ref[5:6, :] * moe
    ms = jnp.mean(x * x, axis=-1, keepdims=True)
    o_ref[...] = x * lax.rsqrt(ms + EPS) * fg_ref[...]


def _combine(tab, y, x, route, mod, final_g, n_tok, off_tiles):
    tok = lambda w: pl.BlockSpec((TM, w), lambda i: (i + off_tiles, 0))
    return pl.pallas_call(
        functools.partial(_combine_kernel, first_tile=off_tiles),
        out_shape=jax.ShapeDtypeStruct((n_tok, D), F32),
        grid=(n_tok // TM,),
        in_specs=[
            pl.BlockSpec(memory_space=pltpu.SMEM),
            pl.BlockSpec(memory_space=pl.ANY),
            tok(D), tok(LANES), _mod_spec(1, TM, off_tiles), _resident((1, D)),
        ],
        out_specs=pl.BlockSpec((TM, D), lambda i: (i, 0)),
        scratch_shapes=[pltpu.VMEM((2, L_TILE, D), F32), pltpu.SemaphoreType.DMA((2,))],
        compiler_params=_params(("arbitrary",)),
        name="moe_combine",
    )(tab, y, x, route, mod, final_g)


def kernel(x_prompt, x_sample, cache_k, cache_v, c, c_ctx, ada_w, ada_b, norm1_g, norm2_g, final_g, mix_w_in, sgu_g, sgu_w, sgu_b, mix_w_out, ffn_w1, ffn_w3, ffn_w2, attn_w_qkv, lam_q1, lam_k1, lam_q2, lam_k2, subln_g, attn_w_o, router_w, moe_w1, moe_w3, moe_w2):
    tb = _device_tables()
    xc = x_prompt.reshape(NP, D)
    xl = x_sample.reshape(NS, D)
    cvec = jnp.concatenate([c, c_ctx[None, :], jnp.zeros((3, D), F32)], axis=0)
    mod = _adaln(cvec, ada_w, ada_b)

    bias = jnp.repeat(sgu_b[0].T, GW, axis=1)
    a, y = _inproj(xc, xl, mod, norm1_g[0][None, :], mix_w_in[0].astype(BF16), sgu_g[0].reshape(1, SGU_W),
                   sgu_w[0].astype(BF16), bias, tb)
    w_out = mix_w_out[0].astype(BF16)
    x1c = _mixout_ctx(y, a, xc, mod, w_out, tb)
    ylat = y[NP:]
    y1r = ylat[:, :FNET_W].reshape(N_LAT_SEQ, DFT_N1, DFT_N2 * FNET_W)
    y2r = ylat[:, FNET_W:].reshape(N_LAT_SEQ, DFT_N1, DFT_N2 * FNET_W)
    br, bi = _dft1(y1r, y2r, tb)
    x1l = _mixout_lat(br, bi, a.reshape(T // LAT_LEN, DFT_N2, DFT_N1, SGU_W),
                      xl.reshape(N_LAT_SEQ, DFT_N2, DFT_N1, D), mod, w_out).reshape(NS, D)
    x2 = _ffn(x1c, x1l, mod, norm2_g[0][None, :], ffn_w1[0].astype(BF16), ffn_w3[0].astype(BF16), ffn_w2[0].astype(BF16))

    qkv_c, qkv_l, kf, vf = _qkv(x2, mod, norm1_g[1][None, :], attn_w_qkv[0].astype(BF16), tb)
    lam_p = jnp.stack([lam_q1[0], lam_k1[0], lam_q2[0], lam_k2[0]], axis=0)
    sg = subln_g[0][None, :]
    oc = _attn_ctx(*qkv_c, lam_p, sg)
    ol = _attn_lat(*qkv_l, cache_k.reshape(N_LAT_SEQ, CTX_LEN, D), cache_v.reshape(N_LAT_SEQ, CTX_LEN, D), lam_p, sg)

    w_r = jnp.pad(router_w[0], ((0, 0), (0, LANES - N_EXPERTS))).astype(BF16)
    x3, h2, route, cnt = _oproj_route(oc, ol, x2, mod, attn_w_o[0].astype(BF16), norm2_g[1][None, :], w_r, tb)
    unit_per_tile = T_MOE // SEG_ALIGN
    seg_len = cnt[:, 0, :N_EXPERTS].astype(I32)
    seg_local = jnp.cumsum(seg_len, axis=1) - seg_len
    group_len = jnp.sum(seg_len, axis=0)
    group_tiles = (group_len + unit_per_tile - 1) // unit_per_tile
    tile_ends = jnp.cumsum(group_tiles)
    group_start = (tile_ends - group_tiles) * unit_per_tile
    seg_global = group_start[None, :] + jnp.cumsum(seg_len, axis=0) - seg_len
    tab = jnp.concatenate([seg_len, seg_local, seg_global], axis=1)
    tab = jnp.pad(tab, ((0, 0), (0, LANES - 3 * N_EXPERTS)))
    fill = jnp.concatenate([group_start + group_len, group_tiles * unit_per_tile - group_len,
                            tile_ends[-1:] * unit_per_tile, N_MOE_TILES - tile_ends[-1:]])
    fill = jnp.pad(fill, (0, LANES - 2 * N_EXPERTS - 2)).reshape(1, LANES)
    tile_idx = jnp.arange(N_MOE_TILES, dtype=I32)
    tile_valid = (tile_idx < tile_ends[-1]).astype(I32)
    tile_expert = jnp.sum((jnp.minimum(tile_idx, tile_ends[-1] - 1)[:, None] >= tile_ends[None, :]).astype(I32), axis=-1)

    xs = _dispatch(tab, fill, h2, route)
    ys = _moe(tile_expert, tile_valid, xs, moe_w1[0].astype(BF16), moe_w3[0].astype(BF16), moe_w2[0].astype(BF16))
    fg = final_g[None, :]
    y_prompt = _combine(tab, ys, x3, route, mod, fg, NP, 0)
    y_sample = _combine(tab, ys, x3, route, mod, fg, NS, NP // TM)

    return (y_prompt.reshape(N_CTX_SEQ, CTX_LEN, D), y_sample.reshape(N_LAT_SEQ, LAT_LEN, D),
            kf.reshape(N_CTX_SEQ, 1, CTX_LEN, N_HEADS, 2 * HEAD_DIM), vf.reshape(N_CTX_SEQ, 1, CTX_LEN, N_HEADS, V_DIM))
```

```python
import functools
import math

import numpy as np
import jax
import jax.numpy as jnp
from jax import lax
from jax.experimental import pallas as pl
from jax.experimental.pallas import tpu as pltpu

F32 = jnp.float32
BF16 = jnp.bfloat16
I32 = jnp.int32

D = 1024
N_CTX_SEQ = 32
CTX_LEN = 256
N_LAT_SEQ = 4
LAT_LEN = 4096
NP = N_CTX_SEQ * CTX_LEN
NS = N_LAT_SEQ * LAT_LEN
T = NP + NS
GRID_W = 64
CHUNK = 128
SGU_W = 512
FNET_W = 512
GW = 128
N_HEADS = 8
HEAD_DIM = 64
V_DIM = 128
ROPE_THETA = 10000.0
D_FF = 2816
N_EXPERTS = 8
D_FF_EXPERT = 3584
EPS = 1e-6
LAM0 = 0.8 - 0.6 * math.exp(-0.3 * 1)
CTX_ROW = 4

LANES = 128
TM = 512
TQ = 1024
TK = 2048
Q_SCALE = HEAD_DIM ** -0.5 * math.log2(math.e)
T_MOE = 512
F_MOE = 1792
N_F_MOE = D_FF_EXPERT // F_MOE
SEG_ALIGN = 8
SEG_BITS = (64, 32, 16, 8, 4, 2, 1)
L_TILE = 1152
N_ROWS = -(-(2 * T + (T // TM) * N_EXPERTS * (SEG_ALIGN - 1) + N_EXPERTS * (T_MOE - SEG_ALIGN)) // T_MOE) * T_MOE
N_MOE_TILES = N_ROWS // T_MOE
MAX_UNUSED_TILES = N_MOE_TILES - 2 * T // T_MOE
FFN_CHUNK = 2816
DFT_N1 = 256
DFT_N2 = 16
DFT_ROWS = 32
DFT1_GROUP = 4
CTX_PER_STEP = 2
ROUTE_SUB = 2
VMEM_LIMIT = 56 * 2 ** 20


def _params(sem, vmem=VMEM_LIMIT):
    return pltpu.CompilerParams(dimension_semantics=sem, vmem_limit_bytes=vmem)


def _resident(shape):
    nd = len(shape)
    return pl.BlockSpec(shape, lambda *_: (0,) * nd, pipeline_mode=pl.Buffered(1))


def _mod_row(i, tm):
    npt = NP // tm
    return jnp.where(i < npt, CTX_ROW, (i - npt) // (LAT_LEN // tm))


def _mod_spec(layer, tm, offset_tiles=0):
    return pl.BlockSpec((None, None, 6, D), lambda i, *_: (layer, _mod_row(i + offset_tiles, tm), 0, 0))


def _two_stream_specs(tm, width):
    npt = NP // tm
    ctx = pl.BlockSpec((tm, width), lambda i: (jnp.minimum(i, npt - 1), 0))
    lat = pl.BlockSpec((tm, width), lambda i: (jnp.maximum(i - npt, 0), 0))
    return ctx, lat


def _pick_stream(tm, ctx_ref, lat_ref):
    return jnp.where(pl.program_id(0) < NP // tm, ctx_ref[...], lat_ref[...])


def _modulate(x, g, scale, shift):
    ms = jnp.mean(x * x, axis=-1, keepdims=True)
    return x * lax.rsqrt(ms + EPS) * (g * (1.0 + scale)) + shift


def _silu(a):
    return a * jax.nn.sigmoid(a)


def _gelu_tanh(x):
    return 0.5 * x * (1.0 + jnp.tanh(0.7978845608028654 * (x + 0.044715 * (x * x * x))))


@functools.lru_cache(maxsize=None)
def _tables():
    def cs(n):
        k = np.arange(n, dtype=np.int64)
        ang = 2.0 * np.pi * ((k[:, None] * k[None, :]) % n) / n
        return np.cos(ang), np.sin(ang)

    c128, s128 = cs(GW)
    dft_ch = np.concatenate([c128, s128], axis=1)
    c256, s256 = cs(CTX_LEN)
    dft_ctx = np.concatenate([c256, -s256], axis=1)
    k1 = np.arange(DFT_N1, dtype=np.int64)[:, None]
    n2 = np.arange(DFT_N2, dtype=np.int64)[None, :]
    tw = 2.0 * np.pi * ((k1 * n2) % LAT_LEN) / LAT_LEN
    half = HEAD_DIM // 2
    inv_freq = ROPE_THETA ** (-np.arange(0, half, 2, dtype=np.float64) / half)
    t = np.arange(LAT_LEN)
    ang_r = (t // GRID_W).astype(np.float64)[:, None] * inv_freq[None, :]
    ang_c = (t % GRID_W).astype(np.float64)[:, None] * inv_freq[None, :]

    def blk(ang):
        c = np.cos(ang)
        s = np.sin(ang)
        return np.concatenate([c, c], axis=1), np.concatenate([-s, s], axis=1)

    cr, sr = blk(ang_r)
    cc, sc = blk(ang_c)
    cos64 = np.concatenate([cr, cc], axis=1)
    sin64 = np.concatenate([sr, sc], axis=1)
    rope_cos = np.concatenate([cos64, cos64], axis=1).astype(np.float32)
    rope_sin = np.concatenate([sin64, sin64], axis=1).astype(np.float32)
    tri = np.tril(np.ones((TM, TM), np.float32), k=-1)
    below = np.triu(np.ones((LANES, LANES), np.float32), k=1)
    f32 = lambda a: np.asarray(a, np.float32)
    return dict(dft_ch=f32(dft_ch), dft_ctx=f32(dft_ctx), c256=f32(c256), s256=f32(s256),
                tw_cos=f32(np.cos(tw)), tw_sin=f32(np.sin(tw)),
                rope_cos=rope_cos, rope_sin=rope_sin, tri=tri, below=below)


def _device_tables():
    tb = {k: jnp.asarray(v) for k, v in _tables().items()}
    for k in ("dft_ch", "dft_ctx", "c256", "s256", "tri", "below"):
        tb[k] = tb[k].astype(BF16)
    return tb


def _adaln_kernel(c_ref, w_ref, b_ref, o_ref):
    s = _silu(c_ref[...]).astype(BF16)
    o_ref[...] = jnp.dot(s, w_ref[...].astype(BF16), preferred_element_type=F32) + b_ref[...]


def _adaln(cvec, ada_w, ada_b):
    tn = 1536
    out = pl.pallas_call(
        _adaln_kernel,
        out_shape=jax.ShapeDtypeStruct((2, 8, 6 * D), F32),
        grid=(2, 6 * D // tn),
        in_specs=[
            pl.BlockSpec((8, D), lambda l, j: (0, 0)),
            pl.BlockSpec((None, D, tn), lambda l, j: (l, 0, j)),
            pl.BlockSpec((None, 1, tn), lambda l, j: (l, 0, j)),
        ],
        out_specs=pl.BlockSpec((None, 8, tn), lambda l, j: (l, 0, j)),
        compiler_params=_params(("arbitrary", "arbitrary")),
        name="adaln",
    )(cvec, ada_w, ada_b.reshape(2, 1, 6 * D))
    return out.reshape(2, 8, 6, D)


def _inproj_kernel(xc_ref, xl_ref, mod_ref, g_ref, win_ref, sgug_ref, sguw_ref, sgub_ref, dft_ref, a_ref, y_ref):
    x = _pick_stream(TM, xc_ref, xl_ref)
    h = _modulate(x, g_ref[...], mod_ref[1:2, :], mod_ref[0:1, :]).astype(BF16)
    p = jnp.dot(h, win_ref[...], preferred_element_type=F32)
    act = _gelu_tanh(p[:, :2 * SGU_W])
    for g in range(4):
        lo, hi = g * GW, (g + 1) * GW
        u = act[:, lo:hi]
        v = act[:, SGU_W + lo:SGU_W + hi]
        ms = jnp.mean(v * v, axis=-1, keepdims=True)
        vn = (v * lax.rsqrt(ms + EPS) * sgug_ref[:, lo:hi]).astype(BF16)
        w = sguw_ref[g]
        for c in range(TM // CHUNK):
            r0, r1 = c * CHUNK, (c + 1) * CHUNK
            mix = jnp.dot(w, vn[r0:r1, :], preferred_element_type=F32) + sgub_ref[:, lo:hi]
            a_ref[r0:r1, lo:hi] = (u[r0:r1, :] * mix).astype(BF16)
        fg = p[:, 2 * SGU_W + lo:2 * SGU_W + hi].astype(BF16)
        yy = jnp.dot(fg, dft_ref[...], preferred_element_type=F32)
        y_ref[:, lo:hi] = yy[:, :GW].astype(BF16)
        y_ref[:, FNET_W + lo:FNET_W + hi] = yy[:, GW:].astype(BF16)


def _inproj(x_ctx, x_lat, mod, g, w_in, sgu_g, sgu_w, sgu_b, tb):
    ctx, lat = _two_stream_specs(TM, D)
    return pl.pallas_call(
        _inproj_kernel,
        out_shape=(jax.ShapeDtypeStruct((T, SGU_W), BF16), jax.ShapeDtypeStruct((T, 2 * FNET_W), BF16)),
        grid=(T // TM,),
        in_specs=[
            ctx, lat,
            _mod_spec(0, TM),
            _resident((1, D)),
            _resident((D, 3 * SGU_W)),
            _resident((1, SGU_W)),
            _resident((4, CHUNK, CHUNK)),
            _resident((CHUNK, SGU_W)),
            _resident((GW, 2 * GW)),
        ],
        out_specs=(pl.BlockSpec((TM, SGU_W), lambda i: (i, 0)), pl.BlockSpec((TM, 2 * FNET_W), lambda i: (i, 0))),
        compiler_params=_params(("arbitrary",)),
        name="inproj_mix",
    )(x_ctx, x_lat, mod, g, w_in, sgu_g, sgu_w, sgu_b, tb["dft_ch"])


def _mix_out(f, a_ref, x_ref, mod_ref, wout_ref, o_ref):
    mix = (jnp.dot(a_ref[...], wout_ref[:SGU_W, :], preferred_element_type=F32)
           + jnp.dot(f.astype(BF16), wout_ref[SGU_W:, :], preferred_element_type=F32))
    o_ref[...] = x_ref[...] + mod_ref[2:3, :] * mix


def _mixout_ctx_kernel(y_ref, a_ref, x_ref, mod_ref, dft_ref, wout_ref, o_ref):
    fs = []
    for s in range(CTX_PER_STEP):
        rows = slice(s * CTX_LEN, (s + 1) * CTX_LEN)
        fs.append(jnp.dot(dft_ref[:, :CTX_LEN], y_ref[rows, :FNET_W], preferred_element_type=F32)
                  + jnp.dot(dft_ref[:, CTX_LEN:], y_ref[rows, FNET_W:], preferred_element_type=F32))
    f = jnp.concatenate(fs, axis=0) * (1.0 / math.sqrt(CTX_LEN * GW))
    _mix_out(f, a_ref, x_ref, mod_ref, wout_ref, o_ref)


def _mixout_ctx(y, a, x, mod, w_out, tb):
    rows = CTX_PER_STEP * CTX_LEN
    tok = lambda w: pl.BlockSpec((rows, w), lambda i: (i, 0))
    return pl.pallas_call(
        _mixout_ctx_kernel,
        out_shape=jax.ShapeDtypeStruct((NP, D), F32),
        grid=(N_CTX_SEQ // CTX_PER_STEP,),
        in_specs=[tok(2 * FNET_W), tok(SGU_W), tok(D),
                  pl.BlockSpec((None, None, 6, D), lambda i: (0, CTX_ROW, 0, 0)),
                  _resident((CTX_LEN, 2 * CTX_LEN)), _resident((D, D))],
        out_specs=tok(D),
        compiler_params=_params(("arbitrary",)),
        name="mixout_ctx",
    )(y, a, x, mod, tb["dft_ctx"], w_out)


def _dft1_kernel(y1_ref, y2_ref, c_ref, s_ref, twc_ref, tws_ref, br_ref, bi_ref):
    y1 = y1_ref[...]
    y2 = y2_ref[...]
    c = c_ref[...]
    s = s_ref[...]
    ar = jnp.dot(c, y1, preferred_element_type=F32) - jnp.dot(s, y2, preferred_element_type=F32)
    ai = -(jnp.dot(c, y2, preferred_element_type=F32) + jnp.dot(s, y1, preferred_element_type=F32))
    lane = lax.broadcasted_iota(I32, (DFT_N1, DFT_N2), 1)
    for r in range(DFT1_GROUP):
        n2 = pl.program_id(1) * DFT1_GROUP + r
        cols = slice(r * FNET_W, (r + 1) * FNET_W)
        tc = jnp.sum(jnp.where(lane == n2, twc_ref[...], 0.0), axis=-1, keepdims=True)
        ts = jnp.sum(jnp.where(lane == n2, tws_ref[...], 0.0), axis=-1, keepdims=True)
        br_ref[r] = (tc * ar[:, cols] + ts * ai[:, cols]).astype(BF16)
        bi_ref[r] = (tc * ai[:, cols] - ts * ar[:, cols]).astype(BF16)


def _dft1(y1r, y2r, tb):
    shp = jax.ShapeDtypeStruct((N_LAT_SEQ, DFT_N2, DFT_N1, FNET_W), BF16)
    src = pl.BlockSpec((None, DFT_N1, DFT1_GROUP * FNET_W), lambda b, n: (b, 0, n))
    dst = pl.BlockSpec((None, DFT1_GROUP, DFT_N1, FNET_W), lambda b, n: (b, n, 0, 0))
    return pl.pallas_call(
        _dft1_kernel,
        out_shape=(shp, shp),
        grid=(N_LAT_SEQ, DFT_N2 // DFT1_GROUP),
        in_specs=[src, src, _resident((DFT_N1, DFT_N1)), _resident((DFT_N1, DFT_N1)),
                  _resident((DFT_N1, DFT_N2)), _resident((DFT_N1, DFT_N2))],
        out_specs=(dst, dst),
        compiler_params=_params(("arbitrary", "arbitrary")),
        name="dft_stage1",
    )(y1r, y2r, tb["c256"], tb["s256"], tb["tw_cos"], tb["tw_sin"])


def _cmul_const(z, wr, wi):
    re, im = z
    tol = 1e-12
    if abs(wi) < tol:
        return (re, im) if wr > 0 else (-re, -im)
    if abs(wr) < tol:
        return (-im, re) if wi > 0 else (im, -re)
    return (re * wr - im * wi, re * wi + im * wr)


def _fft(zs):
    n = len(zs)
    if n == 1:
        return zs
    even, odd = _fft(zs[0::2]), _fft(zs[1::2])
    out = [None] * n
    for k in range(n // 2):
        ang = 2.0 * math.pi * k / n
        t = _cmul_const(odd[k], math.cos(ang), -math.sin(ang))
        out[k] = (even[k][0] + t[0], even[k][1] + t[1])
        out[k + n // 2] = (even[k][0] - t[0], even[k][1] - t[1])
    return out


def _mixout_lat_kernel(br_ref, bi_ref, a_ref, x_ref, mod_ref, wout_ref, o_ref):
    spec = _fft([(br_ref[n2].astype(F32), bi_ref[n2].astype(F32)) for n2 in range(DFT_N2)])
    f = jnp.concatenate([re for re, _ in spec], axis=0) * (1.0 / math.sqrt(LAT_LEN * GW))
    rows = DFT_N2 * DFT_ROWS
    mix = (jnp.dot(a_ref[...].reshape(rows, SGU_W), wout_ref[:SGU_W, :], preferred_element_type=F32)
           + jnp.dot(f.astype(BF16), wout_ref[SGU_W:, :], preferred_element_type=F32))
    out = x_ref[...].reshape(rows, D) + mod_ref[2:3, :] * mix
    o_ref[...] = out.reshape(DFT_N2, DFT_ROWS, D)


def _mixout_lat(br, bi, a_lat, x, mod, w_out):
    blk = lambda w, off=0: pl.BlockSpec((None, DFT_N2, DFT_ROWS, w), lambda b, r: (b + off, 0, r, 0))
    return pl.pallas_call(
        _mixout_lat_kernel,
        out_shape=jax.ShapeDtypeStruct((N_LAT_SEQ, DFT_N2, DFT_N1, D), F32),
        grid=(N_LAT_SEQ, DFT_N1 // DFT_ROWS),
        in_specs=[blk(FNET_W), blk(FNET_W), blk(SGU_W, NP // LAT_LEN), blk(D),
                  pl.BlockSpec((None, None, 6, D), lambda b, r: (0, b, 0, 0)),
                  _resident((D, D))],
        out_specs=blk(D),
        compiler_params=_params(("arbitrary", "arbitrary")),
        name="mixout_lat",
    )(br, bi, a_lat, x, mod, w_out)


def _ffn_kernel(xc_ref, xl_ref, mod_ref, g_ref, w1_ref, w3_ref, w2_ref, o_ref):
    x = _pick_stream(TM, xc_ref, xl_ref)
    h = _modulate(x, g_ref[...], mod_ref[4:5, :], mod_ref[3:4, :]).astype(BF16)
    acc = None
    for c in range(D_FF // FFN_CHUNK):
        lo, hi = c * FFN_CHUNK, (c + 1) * FFN_CHUNK
        a = jnp.dot(h, w1_ref[:, lo:hi], preferred_element_type=F32)
        b = jnp.dot(h, w3_ref[:, lo:hi], preferred_element_type=F32)
        d = jnp.dot((_silu(a) * b).astype(BF16), w2_ref[lo:hi, :], preferred_element_type=F32)
        acc = d if acc is None else acc + d
    o_ref[...] = x + mod_ref[5:6, :] * acc


def _ffn(x_ctx, x_lat, mod, g, w1, w3, w2):
    ctx, lat = _two_stream_specs(TM, D)
    return pl.pallas_call(
        _ffn_kernel,
        out_shape=jax.ShapeDtypeStruct((T, D), F32),
        grid=(T // TM,),
        in_specs=[
            ctx, lat,
            _mod_spec(0, TM),
            _resident((1, D)),
            _resident((D, D_FF)),
            _resident((D, D_FF)),
            _resident((D_FF, D)),
        ],
        out_specs=pl.BlockSpec((TM, D), lambda i: (i, 0)),
        compiler_params=_params(("arbitrary",)),
        name="ffn",
    )(x_ctx, x_lat, mod, g, w1, w3, w2)


def _qkv_ctx_kernel(x_ref, mod_ref, g_ref, w_ref, q_ref, k_ref, v_ref, kf_ref, vf_ref):
    h = _modulate(x_ref[...], g_ref[...], mod_ref[1:2, :], mod_ref[0:1, :]).astype(BF16)
    qkv = jnp.dot(h, w_ref[...], preferred_element_type=F32)
    k = qkv[:, D:2 * D]
    v = qkv[:, 2 * D:]
    q_ref[...] = (qkv[:, :D] * Q_SCALE).astype(BF16)
    k_ref[...] = k.astype(BF16)
    v_ref[...] = v.astype(BF16)
    kf_ref[...] = k
    vf_ref[...] = v


def _qkv_lat_kernel(x_ref, mod_ref, g_ref, w_ref, cos_ref, sin_ref, q_ref, k_ref, v_ref):
    h = _modulate(x_ref[...], g_ref[...], mod_ref[1:2, :], mod_ref[0:1, :]).astype(BF16)
    qkv = jnp.dot(h, w_ref[...], preferred_element_type=F32)
    cos = cos_ref[...]
    sin = sin_ref[...]
    first = (lax.broadcasted_iota(I32, (TM, LANES), 1) & 31) < 16

    def rotate(xh):
        partner = jnp.where(first, pltpu.roll(xh, LANES - 16, 1), pltpu.roll(xh, 16, 1))
        return xh * cos + partner * sin

    for hh in range(N_HEADS):
        sl = slice(hh * LANES, (hh + 1) * LANES)
        q_ref[:, sl] = (rotate(qkv[:, sl]) * Q_SCALE).astype(BF16)
        k_ref[:, sl] = rotate(qkv[:, D + hh * LANES:D + (hh + 1) * LANES]).astype(BF16)
    v_ref[...] = qkv[:, 2 * D:].astype(BF16)


def _qkv(x, mod, g, w_qkv, tb):
    tok = pl.BlockSpec((TM, D), lambda i: (i, 0))
    bshape = jax.ShapeDtypeStruct((NP, D), BF16)
    lshape = jax.ShapeDtypeStruct((NS, D), BF16)
    fshape = jax.ShapeDtypeStruct((NP, D), F32)
    qc, kc, vc, kf, vf = pl.pallas_call(
        _qkv_ctx_kernel,
        out_shape=(bshape, bshape, bshape, fshape, fshape),
        grid=(NP // TM,),
        in_specs=[tok, _mod_spec(1, TM), _resident((1, D)), _resident((D, 3 * D))],
        out_specs=(tok, tok, tok, tok, tok),
        compiler_params=_params(("arbitrary",)),
        name="qkv_ctx",
    )(x, mod, g, w_qkv)
    off = NP // TM
    ltok = pl.BlockSpec((TM, D), lambda i: (i + off, 0))
    rope = pl.BlockSpec((TM, LANES), lambda i: (i % (LAT_LEN // TM), 0))
    ql, kl, vl = pl.pallas_call(
        _qkv_lat_kernel,
        out_shape=(lshape, lshape, lshape),
        grid=(NS // TM,),
        in_specs=[ltok, _mod_spec(1, TM, off), _resident((1, D)), _resident((D, 3 * D)), rope, rope],
        out_specs=(tok, tok, tok),
        compiler_params=_params(("arbitrary",)),
        name="qkv_lat",
    )(x, mod, g, w_qkv, tb["rope_cos"], tb["rope_sin"])
    return (qc, kc, vc), (ql, kl, vl), kf, vf


def _lam(lp_ref):
    lp = lp_ref[...]
    a = jnp.sum(lp[0:1, :] * lp[1:2, :], axis=-1, keepdims=True)
    b = jnp.sum(lp[2:3, :] * lp[3:4, :], axis=-1, keepdims=True)
    return jnp.exp(a) - jnp.exp(b) + LAM0


def _head_norm(o, sg):
    ms = jnp.mean(o * o, axis=-1, keepdims=True)
    return o * lax.rsqrt(ms + EPS) * (sg * (1.0 - LAM0))


_NT = (((1,), (1,)), ((), ()))


def _attn_ctx_kernel(q_ref, k_ref, v_ref, lp_ref, sg_ref, o_ref):
    lam = _lam(lp_ref)
    map0 = lax.broadcasted_iota(I32, (CTX_LEN, LANES), 1) < HEAD_DIM
    init = _softmax_init(CTX_LEN)
    for hh in range(N_HEADS):
        sl = slice(hh * LANES, (hh + 1) * LANES)
        q = q_ref[:, sl]
        zero = jnp.zeros_like(q)
        outs = []
        for qc in (jnp.where(map0, q, zero), jnp.where(map0, zero, q)):
            _, l, acc = _softmax_step(qc, k_ref[:, sl], v_ref[:, sl], init)
            outs.append(acc / jnp.sum(l, axis=-1, keepdims=True))
        o_ref[:, sl] = _head_norm(outs[0] - lam * outs[1], sg_ref[...]).astype(BF16)


def _attn_ctx(q, k, v, lam_p, subln_g):
    seq = pl.BlockSpec((CTX_LEN, D), lambda i: (i, 0))
    return pl.pallas_call(
        _attn_ctx_kernel,
        out_shape=jax.ShapeDtypeStruct((NP, D), BF16),
        grid=(N_CTX_SEQ,),
        in_specs=[seq, seq, seq, _resident((4, HEAD_DIM)), _resident((1, V_DIM))],
        out_specs=seq,
        compiler_params=_params(("arbitrary",)),
        name="attn_ctx",
    )(q, k, v, lam_p, subln_g)


def _softmax_init(rows):
    return (jnp.full((rows, LANES), -jnp.inf, F32), jnp.zeros((rows, LANES), F32), jnp.zeros((rows, LANES), F32))


def _softmax_step(qc, kc, vc, state):
    m, l, acc = state
    s = lax.dot_general(qc, kc, _NT, preferred_element_type=F32)
    blocks = [s[:, j * LANES:(j + 1) * LANES] for j in range(s.shape[1] // LANES)]
    bm = functools.reduce(jnp.maximum, blocks)
    m_new = jnp.maximum(m, jnp.max(bm, axis=-1, keepdims=True))
    alpha = jnp.exp2(m - m_new)
    ps = [jnp.exp2(b - m_new) for b in blocks]
    l_new = alpha * l + functools.reduce(jnp.add, ps)
    p = jnp.concatenate(ps, axis=1).astype(BF16)
    acc_new = alpha * acc + jnp.dot(p, vc, preferred_element_type=F32)
    return m_new, l_new, acc_new


def _attn_lat_kernel(q_ref, k_ref, v_ref, ck_ref, cv_ref, lp_ref, sg_ref, o_ref):
    q = q_ref[...]
    zero = jnp.zeros_like(q)
    map0 = lax.broadcasted_iota(I32, (TQ, LANES), 1) < HEAD_DIM
    qs = (jnp.where(map0, q, zero), jnp.where(map0, zero, q))
    init = _softmax_init(TQ)
    states = [init, init]
    chunks = [(ck_ref[...].astype(BF16), cv_ref[...].astype(BF16))]
    chunks += [(k_ref[j * TK:(j + 1) * TK, :], v_ref[j * TK:(j + 1) * TK, :]) for j in range(LAT_LEN // TK)]
    for kc, vc in chunks:
        states = [_softmax_step(qs[c], kc, vc, states[c]) for c in range(2)]
    outs = [acc / jnp.sum(l, axis=-1, keepdims=True) for _, l, acc in states]
    o = outs[0] - _lam(lp_ref) * outs[1]
    o_ref[...] = _head_norm(o, sg_ref[...]).astype(BF16)


def _attn_lat(q, k, v, cache_k, cache_v, lam_p, subln_g):
    nq = LAT_LEN // TQ
    qspec = pl.BlockSpec((TQ, LANES), lambda b, h, i: (b * nq + i, h))
    kspec = pl.BlockSpec((LAT_LEN, LANES), lambda b, h, i: (b, h))
    cspec = pl.BlockSpec((None, CTX_LEN, LANES), lambda b, h, i: (b, 0, h))
    return pl.pallas_call(
        _attn_lat_kernel,
        out_shape=jax.ShapeDtypeStruct((NS, D), BF16),
        grid=(N_LAT_SEQ, N_HEADS, nq),
        in_specs=[qspec, kspec, kspec, cspec, cspec, _resident((4, HEAD_DIM)), _resident((1, V_DIM))],
        out_specs=qspec,
        compiler_params=_params(("arbitrary", "arbitrary", "arbitrary")),
        name="attn_lat",
    )(q, k, v, cache_k, cache_v, lam_p, subln_g)


def _oproj_route_kernel(oc_ref, ol_ref, x_ref, mod_ref, wo_ref, g_ref, wr_ref, tri_ref, below_ref, x3_ref, h_ref,
                        route_ref, cnt_ref):
    is_ctx = pl.program_id(0) < NP // (ROUTE_SUB * TM)
    for sub in range(ROUTE_SUB):
        rows = slice(sub * TM, (sub + 1) * TM)
        o = jnp.where(is_ctx, oc_ref[rows, :], ol_ref[rows, :])
        _route_tile(o, x_ref[rows, :], mod_ref, wo_ref, g_ref, wr_ref, tri_ref, below_ref,
                    x3_ref.at[rows, :], h_ref.at[rows, :], route_ref.at[rows, :], cnt_ref.at[sub])


def _route_tile(o, x, mod_ref, wo_ref, g_ref, wr_ref, tri_ref, below_ref, x3_ref, h_ref, route_ref, cnt_ref):
    x3 = x + mod_ref[2:3, :] * jnp.dot(o, wo_ref[...], preferred_element_type=F32)
    x3_ref[...] = x3
    hb = _modulate(x3, g_ref[...], mod_ref[4:5, :], mod_ref[3:4, :]).astype(BF16)
    h_ref[...] = hb

    logits = jnp.dot(hb, wr_ref[...], preferred_element_type=F32)
    lane = lax.broadcasted_iota(I32, (TM, LANES), 1).astype(F32)
    neg = jnp.float32(-jnp.inf)
    lg = jnp.where(lane < N_EXPERTS, logits, neg)
    v1 = jnp.max(lg, axis=-1, keepdims=True)
    i1 = jnp.min(jnp.where(lg == v1, lane, float(LANES)), axis=-1, keepdims=True)
    lg2 = jnp.where(lane == i1, neg, lg)
    v2 = jnp.max(lg2, axis=-1, keepdims=True)
    i2 = jnp.min(jnp.where(lg2 == v2, lane, float(LANES)), axis=-1, keepdims=True)
    e = jnp.exp(v2 - v1)
    g1 = 1.0 / (1.0 + e)
    g2 = e / (1.0 + e)
    sel1 = lane == i1
    sel2 = lane == i2
    onehot = jnp.where(sel1 | sel2, 1.0, 0.0)
    rank = jnp.dot(tri_ref[...], onehot.astype(BF16), preferred_element_type=F32)
    cnt8 = jnp.floor((jnp.sum(onehot, axis=0, keepdims=True) + (SEG_ALIGN - 1)) * (1.0 / SEG_ALIGN))
    base8 = jnp.dot(jnp.broadcast_to(cnt8, (8, LANES)).astype(BF16), below_ref[...], preferred_element_type=F32)
    slot = base8[0:1, :] * float(SEG_ALIGN) + rank
    pos1 = jnp.sum(jnp.where(sel1, slot, 0.0), axis=-1, keepdims=True)
    pos2 = jnp.sum(jnp.where(sel2, slot, 0.0), axis=-1, keepdims=True)
    cnt_ref[...] = cnt8
    route = jnp.where(lane == 0, i1, 0.0)
    route = jnp.where(lane == 1, i2, route)
    route = jnp.where(lane == 2, g1, route)
    route = jnp.where(lane == 3, g2, route)
    route = jnp.where(lane == 4, pos1, route)
    route = jnp.where(lane == 5, pos2, route)
    route_ref[...] = route


def _oproj_route(o_ctx, o_lat, x, mod, w_o, g, w_r, tb):
    rows = ROUTE_SUB * TM
    tok = pl.BlockSpec((rows, D), lambda i: (i, 0))
    ctx, lat = _two_stream_specs(rows, D)
    return pl.pallas_call(
        _oproj_route_kernel,
        out_shape=(jax.ShapeDtypeStruct((T, D), F32), jax.ShapeDtypeStruct((T, D), BF16),
                   jax.ShapeDtypeStruct((T, LANES), F32), jax.ShapeDtypeStruct((T // TM, 1, LANES), F32)),
        grid=(T // rows,),
        in_specs=[ctx, lat, tok, _mod_spec(1, rows), _resident((D, D)), _resident((1, D)), _resident((D, LANES)),
                  _resident((TM, TM)), _resident((LANES, LANES))],
        out_specs=(tok, tok, pl.BlockSpec((rows, LANES), lambda i: (i, 0)),
                   pl.BlockSpec((ROUTE_SUB, 1, LANES), lambda i: (i, 0, 0))),
        compiler_params=_params(("arbitrary",)),
        name="oproj_route",
    )(o_ctx, o_lat, x, mod, w_o, g, w_r, tb["tri"], tb["below"])


def _bit_copies(n, src0, dst0, bits, make_copy, op):
    for bit in bits:
        done = n & (-2 * bit)

        @pl.when((n & bit) != 0)
        def _(done=done, bit=bit):
            src = pl.multiple_of((src0 + done) * SEG_ALIGN, SEG_ALIGN)
            dst = pl.multiple_of((dst0 + done) * SEG_ALIGN, SEG_ALIGN)
            op(make_copy(src, dst, bit * SEG_ALIGN))


def _segment_copies(tab, make_copy, op):
    for e in range(N_EXPERTS):
        _bit_copies(tab(e), tab(N_EXPERTS + e), tab(2 * N_EXPERTS + e), SEG_BITS, make_copy, op)


def _start(copy):
    copy.start()


def _wait(copy):
    copy.wait()


def _dispatch_kernel(tab_ref, fill_ref, h_ref, route_ref, xs_ref, sorted_s, zero_s, sem, fill_sem):
    i = pl.program_id(0)
    slot = i % 2
    slots = route_ref[...].T
    row = lax.broadcasted_iota(I32, (L_TILE, TM), 0).astype(F32)
    perm = jnp.where((row == slots[4:5, :]) | (row == slots[5:6, :]), 1.0, 0.0).astype(BF16)
    sorted_s[slot] = jnp.dot(perm, h_ref[...], preferred_element_type=F32)

    def copies(tile, s, op):
        def seg(src, dst, rows):
            return pltpu.make_async_copy(sorted_s.at[s, pl.ds(src, rows), :], xs_ref.at[pl.ds(dst, rows), :], sem.at[s])

        _segment_copies(lambda j: tab_ref[tile, j], seg, op)

    copies(i, slot, _start)

    @pl.when(i > 0)
    def _():
        copies(i - 1, 1 - slot, _wait)

    @pl.when(i == pl.num_programs(0) - 1)
    def _():
        copies(i, slot, _wait)

    @pl.when(i == 0)
    def _():
        zero_s[...] = jnp.zeros(zero_s.shape, F32)

        def fill(src, dst, rows):
            del src
            return pltpu.make_async_copy(zero_s.at[pl.ds(0, rows), :], xs_ref.at[pl.ds(dst, rows), :], fill_sem)

        def fills(op):
            for e in range(N_EXPERTS):
                _bit_copies(fill_ref[0, N_EXPERTS + e], 0, fill_ref[0, e], SEG_BITS[1:], fill, op)
            for k in range(MAX_UNUSED_TILES):
                @pl.when(k < fill_ref[0, 2 * N_EXPERTS + 1])
                def _(k=k):
                    dst = pl.multiple_of(fill_ref[0, 2 * N_EXPERTS] * SEG_ALIGN + k * T_MOE, SEG_ALIGN)
                    op(fill(0, dst, T_MOE))

        fills(_start)
        fills(_wait)


def _dispatch(tab, fill, h, route):
    smem = pl.BlockSpec(memory_space=pltpu.SMEM)
    return pl.pallas_call(
        _dispatch_kernel,
        out_shape=jax.ShapeDtypeStruct((N_ROWS, D), F32),
        grid=(T // TM,),
        in_specs=[smem, smem, pl.BlockSpec((TM, D), lambda i: (i, 0)), pl.BlockSpec((TM, LANES), lambda i: (i, 0))],
        out_specs=pl.BlockSpec(memory_space=pl.ANY),
        scratch_shapes=[pltpu.VMEM((2, L_TILE, D), F32), pltpu.VMEM((T_MOE, D), F32),
                        pltpu.SemaphoreType.DMA((2,)), pltpu.SemaphoreType.DMA(())],
        compiler_params=_params(("arbitrary",)),
        name="moe_dispatch",
    )(tab, fill, h, route)


def _moe_kernel(te_ref, tv_ref, xs_ref, w1_ref, w3_ref, w2_ref, y_ref, xb_s, acc_s):
    del te_ref
    i = pl.program_id(0)
    f = pl.program_id(1)
    valid = tv_ref[i] == 1

    @pl.when(f == 0)
    def _():
        xb_s[...] = xs_ref[...].astype(BF16)
        acc_s[...] = jnp.zeros(acc_s.shape, F32)

    @pl.when(valid)
    def _():
        x = xb_s[...]
        a = jnp.dot(x, w1_ref[...], preferred_element_type=F32)
        b = jnp.dot(x, w3_ref[...], preferred_element_type=F32)
        acc_s[...] += jnp.dot((_silu(a) * b).astype(BF16), w2_ref[...], preferred_element_type=F32)

    @pl.when(f == N_F_MOE - 1)
    def _():
        y_ref[...] = acc_s[...]


def _moe(tile_expert, tile_valid, xs, w1, w3, w2):
    def fidx(i, f, te, tv):
        return jnp.where(tv[i] == 1, f, N_F_MOE - 1)

    grid_spec = pltpu.PrefetchScalarGridSpec(
        num_scalar_prefetch=2,
        grid=(N_MOE_TILES, N_F_MOE),
        in_specs=[
            pl.BlockSpec((T_MOE, D), lambda i, f, te, tv: (i, 0)),
            pl.BlockSpec((None, D, F_MOE), lambda i, f, te, tv: (te[i], 0, fidx(i, f, te, tv))),
            pl.BlockSpec((None, D, F_MOE), lambda i, f, te, tv: (te[i], 0, fidx(i, f, te, tv))),
            pl.BlockSpec((None, F_MOE, D), lambda i, f, te, tv: (te[i], fidx(i, f, te, tv), 0)),
        ],
        out_specs=pl.BlockSpec((T_MOE, D), lambda i, f, te, tv: (i, 0)),
        scratch_shapes=[pltpu.VMEM((T_MOE, D), BF16), pltpu.VMEM((T_MOE, D), F32)],
    )
    return pl.pallas_call(
        _moe_kernel,
        out_shape=jax.ShapeDtypeStruct((N_ROWS, D), F32),
        grid_spec=grid_spec,
        compiler_params=_params(("arbitrary", "arbitrary")),
        name="moe_experts",
    )(tile_expert, tile_valid, xs, w1, w3, w2)


def _combine_kernel(tab_ref, y_ref, x_ref, route_ref, mod_ref, fg_ref, o_ref, buf, sem, *, first_tile):
    i = pl.program_id(0)
    slot = i % 2

    def fetch(tile, s, op):
        def seg(src, dst, rows):
            return pltpu.make_async_copy(y_ref.at[pl.ds(dst, rows), :], buf.at[s, pl.ds(src, rows), :], sem.at[s])

        if op is _start:
            buf[s] = jnp.zeros((L_TILE, D), F32)
        _segment_copies(lambda j: tab_ref[tile, j], seg, op)

    @pl.when(i == 0)
    def _():
        fetch(first_tile, 0, _start)

    @pl.when(i + 1 < pl.num_programs(0))
    def _():
        fetch(first_tile + i + 1, 1 - slot, _start)

    fetch(first_tile + i, slot, _wait)
    yb = buf[slot].astype(BF16)
    route = route_ref[...]
    col = lax.broadcasted_iota(I32, (TM, L_TILE), 1).astype(F32)
    picks = [jnp.dot(jnp.where(col == route[:, c:c + 1], 1.0, 0.0).astype(BF16), yb, preferred_element_type=F32)
             for c in (4, 5)]
    moe = route[:, 2:3] * picks[0] + route[:, 3:4] * picks[1]
    x = x_ref[...] + mod_ref[5:6, :] * moe
    ms = jnp.mean(x * x, axis=-1, keepdims=True)
    o_ref[...] = x * lax.rsqrt(ms + EPS) * fg_ref[...]


def _combine(tab, y, x, route, mod, final_g, n_tok, off_tiles):
    tok = lambda w: pl.BlockSpec((TM, w), lambda i: (i + off_tiles, 0))
    return pl.pallas_call(
        functools.partial(_combine_kernel, first_tile=off_tiles),
        out_shape=jax.ShapeDtypeStruct((n_tok, D), F32),
        grid=(n_tok // TM,),
        in_specs=[
            pl.BlockSpec(memory_space=pltpu.SMEM),
            pl.BlockSpec(memory_space=pl.ANY),
            tok(D), tok(LANES), _mod_spec(1, TM, off_tiles), _resident((1, D)),
        ],
        out_specs=pl.BlockSpec((TM, D), lambda i: (i, 0)),
        scratch_shapes=[pltpu.VMEM((2, L_TILE, D), F32), pltpu.SemaphoreType.DMA((2,))],
        compiler_params=_params(("arbitrary",)),
        name="moe_combine",
    )(tab, y, x, route, mod, final_g)


def kernel(x_prompt, x_sample, cache_k, cache_v, c, c_ctx, ada_w, ada_b, norm1_g, norm2_g, final_g, mix_w_in, sgu_g, sgu_w, sgu_b, mix_w_out, ffn_w1, ffn_w3, ffn_w2, attn_w_qkv, lam_q1, lam_k1, lam_q2, lam_k2, subln_g, attn_w_o, router_w, moe_w1, moe_w3, moe_w2):
    tb = _device_tables()
    xc = x_prompt.reshape(NP, D)
    xl = x_sample.reshape(NS, D)
    cvec = jnp.concatenate([c, c_ctx[None, :], jnp.zeros((3, D), F32)], axis=0)
    mod = _adaln(cvec, ada_w, ada_b)

    bias = jnp.repeat(sgu_b[0].T, GW, axis=1)
    a, y = _inproj(xc, xl, mod, norm1_g[0][None, :], mix_w_in[0].astype(BF16), sgu_g[0].reshape(1, SGU_W),
                   sgu_w[0].astype(BF16), bias, tb)
    w_out = mix_w_out[0].astype(BF16)
    x1c = _mixout_ctx(y, a, xc, mod, w_out, tb)
    ylat = y[NP:]
    y1r = ylat[:, :FNET_W].reshape(N_LAT_SEQ, DFT_N1, DFT_N2 * FNET_W)
    y2r = ylat[:, FNET_W:].reshape(N_LAT_SEQ, DFT_N1, DFT_N2 * FNET_W)
    br, bi = _dft1(y1r, y2r, tb)
    x1l = _mixout_lat(br, bi, a.reshape(T // LAT_LEN, DFT_N2, DFT_N1, SGU_W),
                      xl.reshape(N_LAT_SEQ, DFT_N2, DFT_N1, D), mod, w_out).reshape(NS, D)
    x2 = _ffn(x1c, x1l, mod, norm2_g[0][None, :], ffn_w1[0].astype(BF16), ffn_w3[0].astype(BF16), ffn_w2[0].astype(BF16))

    qkv_c, qkv_l, kf, vf = _qkv(x2, mod, norm1_g[1][None, :], attn_w_qkv[0].astype(BF16), tb)
    lam_p = jnp.stack([lam_q1[0], lam_k1[0], lam_q2[0], lam_k2[0]], axis=0)
    sg = subln_g[0][None, :]
    oc = _attn_ctx(*qkv_c, lam_p, sg)
    ol = _attn_lat(*qkv_l, cache_k.reshape(N_LAT_SEQ, CTX_LEN, D), cache_v.reshape(N_LAT_SEQ, CTX_LEN, D), lam_p, sg)

    w_r = jnp.pad(router_w[0], ((0, 0), (0, LANES - N_EXPERTS))).astype(BF16)
    x3, h2, route, cnt = _oproj_route(oc, ol, x2, mod, attn_w_o[0].astype(BF16), norm2_g[1][None, :], w_r, tb)
    unit_per_tile = T_MOE // SEG_ALIGN
    seg_len = cnt[:, 0, :N_EXPERTS].astype(I32)
    seg_local = jnp.cumsum(seg_len, axis=1) - seg_len
    group_len = jnp.sum(seg_len, axis=0)
    group_tiles = (group_len + unit_per_tile - 1) // unit_per_tile
    tile_ends = jnp.cumsum(group_tiles)
    group_start = (tile_ends - group_tiles) * unit_per_tile
    seg_global = group_start[None, :] + jnp.cumsum(seg_len, axis=0) - seg_len
    tab = jnp.concatenate([seg_len, seg_local, seg_global], axis=1)
    tab = jnp.pad(tab, ((0, 0), (0, LANES - 3 * N_EXPERTS)))
    fill = jnp.concatenate([group_start + group_len, group_tiles * unit_per_tile - group_len,
                            tile_ends[-1:] * unit_per_tile, N_MOE_TILES - tile_ends[-1:]])
    fill = jnp.pad(fill, (0, LANES - 2 * N_EXPERTS - 2)).reshape(1, LANES)
    tile_idx = jnp.arange(N_MOE_TILES, dtype=I32)
    tile_valid = (tile_idx < tile_ends[-1]).astype(I32)
    tile_expert = jnp.sum((jnp.minimum(tile_idx, tile_ends[-1] - 1)[:, None] >= tile_ends[None, :]).astype(I32), axis=-1)

    xs = _dispatch(tab, fill, h2, route)
    ys = _moe(tile_expert, tile_valid, xs, moe_w1[0].astype(BF16), moe_w3[0].astype(BF16), moe_w2[0].astype(BF16))
    fg = final_g[None, :]
    y_prompt = _combine(tab, ys, x3, route, mod, fg, NP, 0)
    y_sample = _combine(tab, ys, x3, route, mod, fg, NS, NP // TM)

    return (y_prompt.reshape(N_CTX_SEQ, CTX_LEN, D), y_sample.reshape(N_LAT_SEQ, LAT_LEN, D),
            kf.reshape(N_CTX_SEQ, 1, CTX_LEN, N_HEADS, 2 * HEAD_DIM), vf.reshape(N_CTX_SEQ, 1, CTX_LEN, N_HEADS, V_DIM))
```

```python
import functools
import math

import numpy as np
import jax
import jax.numpy as jnp
from jax import lax
from jax.experimental import pallas as pl
from jax.experimental.pallas import tpu as pltpu

F32 = jnp.float32
BF16 = jnp.bfloat16
I32 = jnp.int32

D = 1024
N_CTX_SEQ = 32
CTX_LEN = 256
N_LAT_SEQ = 4
LAT_LEN = 4096
NP = N_CTX_SEQ * CTX_LEN
NS = N_LAT_SEQ * LAT_LEN
T = NP + NS
GRID_W = 64
CHUNK = 128
SGU_W = 512
FNET_W = 512
GW = 128
N_HEADS = 8
HEAD_DIM = 64
V_DIM = 128
ROPE_THETA = 10000.0
D_FF = 2816
N_EXPERTS = 8
D_FF_EXPERT = 3584
EPS = 1e-6
LAM0 = 0.8 - 0.6 * math.exp(-0.3 * 1)
CTX_ROW = 4

LANES = 128
TM = 512
TQ = 1024
TK = 2048
Q_SCALE = HEAD_DIM ** -0.5 * math.log2(math.e)
T_MOE = 512
F_MOE = 1792
N_F_MOE = D_FF_EXPERT // F_MOE
SEG_ALIGN = 8
SEG_BITS = (64, 32, 16, 8, 4, 2, 1)
L_TILE = 1152
N_ROWS = -(-(2 * T + (T // TM) * N_EXPERTS * (SEG_ALIGN - 1) + N_EXPERTS * (T_MOE - SEG_ALIGN)) // T_MOE) * T_MOE
N_MOE_TILES = N_ROWS // T_MOE
MAX_UNUSED_TILES = N_MOE_TILES - 2 * T // T_MOE
FFN_CHUNK = 2816
DFT_N1 = 256
DFT_N2 = 16
DFT_ROWS = 32
DFT1_GROUP = 4
CTX_PER_STEP = 2
ROUTE_SUB = 2
VMEM_LIMIT = 56 * 2 ** 20


def _params(sem, vmem=VMEM_LIMIT):
    return pltpu.CompilerParams(dimension_semantics=sem, vmem_limit_bytes=vmem)


def _resident(shape):
    nd = len(shape)
    return pl.BlockSpec(shape, lambda *_: (0,) * nd, pipeline_mode=pl.Buffered(1))


def _mod_row(i, tm):
    npt = NP // tm
    return jnp.where(i < npt, CTX_ROW, (i - npt) // (LAT_LEN // tm))


def _mod_spec(layer, tm, offset_tiles=0):
    return pl.BlockSpec((None, None, 6, D), lambda i, *_: (layer, _mod_row(i + offset_tiles, tm), 0, 0))


def _two_stream_specs(tm, width):
    npt = NP // tm
    ctx = pl.BlockSpec((tm, width), lambda i: (jnp.minimum(i, npt - 1), 0))
    lat = pl.BlockSpec((tm, width), lambda i: (jnp.maximum(i - npt, 0), 0))
    return ctx, lat


def _pick_stream(tm, ctx_ref, lat_ref):
    return jnp.where(pl.program_id(0) < NP // tm, ctx_ref[...], lat_ref[...])


def _modulate(x, g, scale, shift):
    ms = jnp.mean(x * x, axis=-1, keepdims=True)
    return x * lax.rsqrt(ms + EPS) * (g * (1.0 + scale)) + shift


def _silu(a):
    return a * jax.nn.sigmoid(a)


def _gelu_tanh(x):
    return 0.5 * x * (1.0 + jnp.tanh(0.7978845608028654 * (x + 0.044715 * (x * x * x))))


@functools.lru_cache(maxsize=None)
def _tables():
    def cs(n):
        k = np.arange(n, dtype=np.int64)
        ang = 2.0 * np.pi * ((k[:, None] * k[None, :]) % n) / n
        return np.cos(ang), np.sin(ang)

    c128, s128 = cs(GW)
    dft_ch = np.concatenate([c128, s128], axis=1)
    c256, s256 = cs(CTX_LEN)
    dft_ctx = np.concatenate([c256, -s256], axis=1)
    k1 = np.arange(DFT_N1, dtype=np.int64)[:, None]
    n2 = np.arange(DFT_N2, dtype=np.int64)[None, :]
    tw = 2.0 * np.pi * ((k1 * n2) % LAT_LEN) / LAT_LEN
    half = HEAD_DIM // 2
    inv_freq = ROPE_THETA ** (-np.arange(0, half, 2, dtype=np.float64) / half)
    t = np.arange(LAT_LEN)
    ang_r = (t // GRID_W).astype(np.float64)[:, None] * inv_freq[None, :]
    ang_c = (t % GRID_W).astype(np.float64)[:, None] * inv_freq[None, :]

    def blk(ang):
        c = np.cos(ang)
        s = np.sin(ang)
        return np.concatenate([c, c], axis=1), np.concatenate([-s, s], axis=1)

    cr, sr = blk(ang_r)
    cc, sc = blk(ang_c)
    cos64 = np.concatenate([cr, cc], axis=1)
    sin64 = np.concatenate([sr, sc], axis=1)
    rope_cos = np.concatenate([cos64, cos64], axis=1).astype(np.float32)
    rope_sin = np.concatenate([sin64, sin64], axis=1).astype(np.float32)
    tri = np.tril(np.ones((TM, TM), np.float32), k=-1)
    below = np.triu(np.ones((LANES, LANES), np.float32), k=1)
    f32 = lambda a: np.asarray(a, np.float32)
    return dict(dft_ch=f32(dft_ch), dft_ctx=f32(dft_ctx), c256=f32(c256), s256=f32(s256),
                tw_cos=f32(np.cos(tw)), tw_sin=f32(np.sin(tw)),
                rope_cos=rope_cos, rope_sin=rope_sin, tri=tri, below=below)


def _device_tables():
    tb = {k: jnp.asarray(v) for k, v in _tables().items()}
    for k in ("dft_ch", "dft_ctx", "c256", "s256", "tri", "below"):
        tb[k] = tb[k].astype(BF16)
    return tb


def _adaln_kernel(c_ref, w_ref, b_ref, o_ref):
    s = _silu(c_ref[...]).astype(BF16)
    o_ref[...] = jnp.dot(s, w_ref[...].astype(BF16), preferred_element_type=F32) + b_ref[...]


def _adaln(cvec, ada_w, ada_b):
    tn = 1536
    out = pl.pallas_call(
        _adaln_kernel,
        out_shape=jax.ShapeDtypeStruct((2, 8, 6 * D), F32),
        grid=(2, 6 * D // tn),
        in_specs=[
            pl.BlockSpec((8, D), lambda l, j: (0, 0)),
            pl.BlockSpec((None, D, tn), lambda l, j: (l, 0, j)),
            pl.BlockSpec((None, 1, tn), lambda l, j: (l, 0, j)),
        ],
        out_specs=pl.BlockSpec((None, 8, tn), lambda l, j: (l, 0, j)),
        compiler_params=_params(("arbitrary", "arbitrary")),
        name="adaln",
    )(cvec, ada_w, ada_b.reshape(2, 1, 6 * D))
    return out.reshape(2, 8, 6, D)


def _inproj_kernel(xc_ref, xl_ref, mod_ref, g_ref, win_ref, sgug_ref, sguw_ref, sgub_ref, dft_ref, a_ref, y_ref,
                   y1r_ref, y2r_ref, slab_s):
    is_ctx = pl.program_id(0) < NP // TM
    x = _pick_stream(TM, xc_ref, xl_ref)
    h = _modulate(x, g_ref[...], mod_ref[1:2, :], mod_ref[0:1, :]).astype(BF16)
    p = jnp.dot(h, win_ref[...], preferred_element_type=F32)
    act = _gelu_tanh(p[:, :2 * SGU_W])
    spectra = []
    for g in range(4):
        lo, hi = g * GW, (g + 1) * GW
        u = act[:, lo:hi]
        v = act[:, SGU_W + lo:SGU_W + hi]
        ms = jnp.mean(v * v, axis=-1, keepdims=True)
        vn = (v * lax.rsqrt(ms + EPS) * sgug_ref[:, lo:hi]).astype(BF16)
        w = sguw_ref[g]
        for c in range(TM // CHUNK):
            r0, r1 = c * CHUNK, (c + 1) * CHUNK
            mix = jnp.dot(w, vn[r0:r1, :], preferred_element_type=F32) + sgub_ref[:, lo:hi]
            a_ref[r0:r1, lo:hi] = (u[r0:r1, :] * mix).astype(BF16)
        fg = p[:, 2 * SGU_W + lo:2 * SGU_W + hi].astype(BF16)
        spectra.append(jnp.dot(fg, dft_ref[...], preferred_element_type=F32))

    @pl.when(is_ctx)
    def _():
        for g, yy in enumerate(spectra):
            y_ref[:, g * GW:(g + 1) * GW] = yy[:, :GW].astype(BF16)
            y_ref[:, FNET_W + g * GW:FNET_W + (g + 1) * GW] = yy[:, GW:].astype(BF16)

    @pl.when(jnp.logical_not(is_ctx))
    def _():
        for g, yy in enumerate(spectra):
            for half, dst in ((0, y1r_ref), (1, y2r_ref)):
                slab = slab_s.at[2 * g + half]
                slab[...] = yy[:, half * GW:(half + 1) * GW]
                for n2 in range(DFT_N2):
                    cols = slice(n2 * FNET_W + g * GW, n2 * FNET_W + (g + 1) * GW)
                    dst[:, cols] = slab[pl.ds(n2, TM // DFT_N2, stride=DFT_N2), :].astype(BF16)


def _inproj(x_ctx, x_lat, mod, g, w_in, sgu_g, sgu_w, sgu_b, tb):
    ctx, lat = _two_stream_specs(TM, D)
    npt = NP // TM
    rows_r = TM // DFT_N2
    yr_shape = jax.ShapeDtypeStruct((NS // DFT_N2, DFT_N2 * FNET_W), BF16)
    yr_spec = pl.BlockSpec((rows_r, DFT_N2 * FNET_W), lambda i: (jnp.maximum(i - npt, 0), 0))
    return pl.pallas_call(
        _inproj_kernel,
        out_shape=(jax.ShapeDtypeStruct((T, SGU_W), BF16), jax.ShapeDtypeStruct((NP, 2 * FNET_W), BF16),
                   yr_shape, yr_shape),
        grid=(T // TM,),
        in_specs=[
            ctx, lat,
            _mod_spec(0, TM),
            _resident((1, D)),
            _resident((D, 3 * SGU_W)),
            _resident((1, SGU_W)),
            _resident((4, CHUNK, CHUNK)),
            _resident((CHUNK, SGU_W)),
            _resident((GW, 2 * GW)),
        ],
        out_specs=(pl.BlockSpec((TM, SGU_W), lambda i: (i, 0)),
                   pl.BlockSpec((TM, 2 * FNET_W), lambda i: (jnp.minimum(i, npt - 1), 0)), yr_spec, yr_spec),
        scratch_shapes=[pltpu.VMEM((8, TM, LANES), F32)],
        compiler_params=_params(("arbitrary",)),
        name="inproj_mix",
    )(x_ctx, x_lat, mod, g, w_in, sgu_g, sgu_w, sgu_b, tb["dft_ch"])


def _mix_out(f, a_ref, x_ref, mod_ref, wout_ref, o_ref):
    mix = (jnp.dot(a_ref[...], wout_ref[:SGU_W, :], preferred_element_type=F32)
           + jnp.dot(f.astype(BF16), wout_ref[SGU_W:, :], preferred_element_type=F32))
    o_ref[...] = x_ref[...] + mod_ref[2:3, :] * mix


def _mixout_ctx_kernel(y_ref, a_ref, x_ref, mod_ref, dft_ref, wout_ref, o_ref):
    fs = []
    for s in range(CTX_PER_STEP):
        rows = slice(s * CTX_LEN, (s + 1) * CTX_LEN)
        fs.append(jnp.dot(dft_ref[:, :CTX_LEN], y_ref[rows, :FNET_W], preferred_element_type=F32)
                  + jnp.dot(dft_ref[:, CTX_LEN:], y_ref[rows, FNET_W:], preferred_element_type=F32))
    f = jnp.concatenate(fs, axis=0) * (1.0 / math.sqrt(CTX_LEN * GW))
    _mix_out(f, a_ref, x_ref, mod_ref, wout_ref, o_ref)


def _mixout_ctx(y, a, x, mod, w_out, tb):
    rows = CTX_PER_STEP * CTX_LEN
    tok = lambda w: pl.BlockSpec((rows, w), lambda i: (i, 0))
    return pl.pallas_call(
        _mixout_ctx_kernel,
        out_shape=jax.ShapeDtypeStruct((NP, D), F32),
        grid=(N_CTX_SEQ // CTX_PER_STEP,),
        in_specs=[tok(2 * FNET_W), tok(SGU_W), tok(D),
                  pl.BlockSpec((None, None, 6, D), lambda i: (0, CTX_ROW, 0, 0)),
                  _resident((CTX_LEN, 2 * CTX_LEN)), _resident((D, D))],
        out_specs=tok(D),
        compiler_params=_params(("arbitrary",)),
        name="mixout_ctx",
    )(y, a, x, mod, tb["dft_ctx"], w_out)


def _dft1_kernel(y1_ref, y2_ref, c_ref, s_ref, twc_ref, tws_ref, br_ref, bi_ref):
    y1 = y1_ref[...]
    y2 = y2_ref[...]
    c = c_ref[...]
    s = s_ref[...]
    ar = jnp.dot(c, y1, preferred_element_type=F32) - jnp.dot(s, y2, preferred_element_type=F32)
    ai = -(jnp.dot(c, y2, preferred_element_type=F32) + jnp.dot(s, y1, preferred_element_type=F32))
    lane = lax.broadcasted_iota(I32, (DFT_N1, DFT_N2), 1)
    for r in range(DFT1_GROUP):
        n2 = pl.program_id(1) * DFT1_GROUP + r
        cols = slice(r * FNET_W, (r + 1) * FNET_W)
        tc = jnp.sum(jnp.where(lane == n2, twc_ref[...], 0.0), axis=-1, keepdims=True)
        ts = jnp.sum(jnp.where(lane == n2, tws_ref[...], 0.0), axis=-1, keepdims=True)
        br_ref[r] = (tc * ar[:, cols] + ts * ai[:, cols]).astype(BF16)
        bi_ref[r] = (tc * ai[:, cols] - ts * ar[:, cols]).astype(BF16)


def _dft1(y1r, y2r, tb):
    shp = jax.ShapeDtypeStruct((N_LAT_SEQ, DFT_N2, DFT_N1, FNET_W), BF16)
    src = pl.BlockSpec((None, DFT_N1, DFT1_GROUP * FNET_W), lambda b, n: (b, 0, n))
    dst = pl.BlockSpec((None, DFT1_GROUP, DFT_N1, FNET_W), lambda b, n: (b, n, 0, 0))
    return pl.pallas_call(
        _dft1_kernel,
        out_shape=(shp, shp),
        grid=(N_LAT_SEQ, DFT_N2 // DFT1_GROUP),
        in_specs=[src, src, _resident((DFT_N1, DFT_N1)), _resident((DFT_N1, DFT_N1)),
                  _resident((DFT_N1, DFT_N2)), _resident((DFT_N1, DFT_N2))],
        out_specs=(dst, dst),
        compiler_params=_params(("arbitrary", "arbitrary")),
        name="dft_stage1",
    )(y1r, y2r, tb["c256"], tb["s256"], tb["tw_cos"], tb["tw_sin"])


def _cmul_const(z, wr, wi):
    re, im = z
    tol = 1e-12
    if abs(wi) < tol:
        return (re, im) if wr > 0 else (-re, -im)
    if abs(wr) < tol:
        return (-im, re) if wi > 0 else (im, -re)
    return (re * wr - im * wi, re * wi + im * wr)


def _fft(zs):
    n = len(zs)
    if n == 1:
        return zs
    even, odd = _fft(zs[0::2]), _fft(zs[1::2])
    out = [None] * n
    for k in range(n // 2):
        ang = 2.0 * math.pi * k / n
        t = _cmul_const(odd[k], math.cos(ang), -math.sin(ang))
        out[k] = (even[k][0] + t[0], even[k][1] + t[1])
        out[k + n // 2] = (even[k][0] - t[0], even[k][1] - t[1])
    return out


def _mixout_lat_kernel(br_ref, bi_ref, a_ref, x_ref, mod_ref, wout_ref, o_ref):
    spec = _fft([(br_ref[n2].astype(F32), bi_ref[n2].astype(F32)) for n2 in range(DFT_N2)])
    f = jnp.concatenate([re for re, _ in spec], axis=0) * (1.0 / math.sqrt(LAT_LEN * GW))
    rows = DFT_N2 * DFT_ROWS
    mix = (jnp.dot(a_ref[...].reshape(rows, SGU_W), wout_ref[:SGU_W, :], preferred_element_type=F32)
           + jnp.dot(f.astype(BF16), wout_ref[SGU_W:, :], preferred_element_type=F32))
    out = x_ref[...].reshape(rows, D) + mod_ref[2:3, :] * mix
    o_ref[...] = out.reshape(DFT_N2, DFT_ROWS, D)


def _mixout_lat(br, bi, a_lat, x, mod, w_out):
    blk = lambda w, off=0: pl.BlockSpec((None, DFT_N2, DFT_ROWS, w), lambda b, r: (b + off, 0, r, 0))
    return pl.pallas_call(
        _mixout_lat_kernel,
        out_shape=jax.ShapeDtypeStruct((N_LAT_SEQ, DFT_N2, DFT_N1, D), F32),
        grid=(N_LAT_SEQ, DFT_N1 // DFT_ROWS),
        in_specs=[blk(FNET_W), blk(FNET_W), blk(SGU_W, NP // LAT_LEN), blk(D),
                  pl.BlockSpec((None, None, 6, D), lambda b, r: (0, b, 0, 0)),
                  _resident((D, D))],
        out_specs=blk(D),
        compiler_params=_params(("arbitrary", "arbitrary")),
        name="mixout_lat",
    )(br, bi, a_lat, x, mod, w_out)


def _ffn_kernel(xc_ref, xl_ref, mod_ref, g_ref, w1_ref, w3_ref, w2_ref, o_ref):
    x = _pick_stream(TM, xc_ref, xl_ref)
    h = _modulate(x, g_ref[...], mod_ref[4:5, :], mod_ref[3:4, :]).astype(BF16)
    acc = None
    for c in range(D_FF // FFN_CHUNK):
        lo, hi = c * FFN_CHUNK, (c + 1) * FFN_CHUNK
        a = jnp.dot(h, w1_ref[:, lo:hi], preferred_element_type=F32)
        b = jnp.dot(h, w3_ref[:, lo:hi], preferred_element_type=F32)
        d = jnp.dot((_silu(a) * b).astype(BF16), w2_ref[lo:hi, :], preferred_element_type=F32)
        acc = d if acc is None else acc + d
    o_ref[...] = x + mod_ref[5:6, :] * acc


def _ffn(x_ctx, x_lat, mod, g, w1, w3, w2):
    ctx, lat = _two_stream_specs(TM, D)
    return pl.pallas_call(
        _ffn_kernel,
        out_shape=jax.ShapeDtypeStruct((T, D), F32),
        grid=(T // TM,),
        in_specs=[
            ctx, lat,
            _mod_spec(0, TM),
            _resident((1, D)),
            _resident((D, D_FF)),
            _resident((D, D_FF)),
            _resident((D_FF, D)),
        ],
        out_specs=pl.BlockSpec((TM, D), lambda i: (i, 0)),
        compiler_params=_params(("arbitrary",)),
        name="ffn",
    )(x_ctx, x_lat, mod, g, w1, w3, w2)


def _qkv_ctx_kernel(x_ref, mod_ref, g_ref, w_ref, q_ref, k_ref, v_ref, kf_ref, vf_ref):
    h = _modulate(x_ref[...], g_ref[...], mod_ref[1:2, :], mod_ref[0:1, :]).astype(BF16)
    qkv = jnp.dot(h, w_ref[...], preferred_element_type=F32)
    k = qkv[:, D:2 * D]
    v = qkv[:, 2 * D:]
    q_ref[...] = (qkv[:, :D] * Q_SCALE).astype(BF16)
    k_ref[...] = k.astype(BF16)
    v_ref[...] = v.astype(BF16)
    kf_ref[...] = k
    vf_ref[...] = v


def _qkv_lat_kernel(x_ref, mod_ref, g_ref, w_ref, cos_ref, sin_ref, q_ref, k_ref, v_ref):
    h = _modulate(x_ref[...], g_ref[...], mod_ref[1:2, :], mod_ref[0:1, :]).astype(BF16)
    qkv = jnp.dot(h, w_ref[...], preferred_element_type=F32)
    cos = cos_ref[...]
    sin = sin_ref[...]
    first = (lax.broadcasted_iota(I32, (TM, LANES), 1) & 31) < 16

    def rotate(xh):
        partner = jnp.where(first, pltpu.roll(xh, LANES - 16, 1), pltpu.roll(xh, 16, 1))
        return xh * cos + partner * sin

    for hh in range(N_HEADS):
        sl = slice(hh * LANES, (hh + 1) * LANES)
        q_ref[:, sl] = (rotate(qkv[:, sl]) * Q_SCALE).astype(BF16)
        k_ref[:, sl] = rotate(qkv[:, D + hh * LANES:D + (hh + 1) * LANES]).astype(BF16)
    v_ref[...] = qkv[:, 2 * D:].astype(BF16)


def _qkv(x, mod, g, w_qkv, tb):
    tok = pl.BlockSpec((TM, D), lambda i: (i, 0))
    bshape = jax.ShapeDtypeStruct((NP, D), BF16)
    lshape = jax.ShapeDtypeStruct((NS, D), BF16)
    fshape = jax.ShapeDtypeStruct((NP, D), F32)
    qc, kc, vc, kf, vf = pl.pallas_call(
        _qkv_ctx_kernel,
        out_shape=(bshape, bshape, bshape, fshape, fshape),
        grid=(NP // TM,),
        in_specs=[tok, _mod_spec(1, TM), _resident((1, D)), _resident((D, 3 * D))],
        out_specs=(tok, tok, tok, tok, tok),
        compiler_params=_params(("arbitrary",)),
        name="qkv_ctx",
    )(x, mod, g, w_qkv)
    off = NP // TM
    ltok = pl.BlockSpec((TM, D), lambda i: (i + off, 0))
    rope = pl.BlockSpec((TM, LANES), lambda i: (i % (LAT_LEN // TM), 0))
    ql, kl, vl = pl.pallas_call(
        _qkv_lat_kernel,
        out_shape=(lshape, lshape, lshape),
        grid=(NS // TM,),
        in_specs=[ltok, _mod_spec(1, TM, off), _resident((1, D)), _resident((D, 3 * D)), rope, rope],
        out_specs=(tok, tok, tok),
        compiler_params=_params(("arbitrary",)),
        name="qkv_lat",
    )(x, mod, g, w_qkv, tb["rope_cos"], tb["rope_sin"])
    return (qc, kc, vc), (ql, kl, vl), kf, vf


def _lam(lp_ref):
    lp = lp_ref[...]
    a = jnp.sum(lp[0:1, :] * lp[1:2, :], axis=-1, keepdims=True)
    b = jnp.sum(lp[2:3, :] * lp[3:4, :], axis=-1, keepdims=True)
    return jnp.exp(a) - jnp.exp(b) + LAM0


def _head_norm(o, sg):
    ms = jnp.mean(o * o, axis=-1, keepdims=True)
    return o * lax.rsqrt(ms + EPS) * (sg * (1.0 - LAM0))


_NT = (((1,), (1,)), ((), ()))


def _attn_ctx_kernel(q_ref, k_ref, v_ref, lp_ref, sg_ref, o_ref):
    lam = _lam(lp_ref)
    map0 = lax.broadcasted_iota(I32, (CTX_LEN, LANES), 1) < HEAD_DIM
    init = _softmax_init(CTX_LEN)
    for hh in range(N_HEADS):
        sl = slice(hh * LANES, (hh + 1) * LANES)
        q = q_ref[:, sl]
        zero = jnp.zeros_like(q)
        outs = []
        for qc in (jnp.where(map0, q, zero), jnp.where(map0, zero, q)):
            _, l, acc = _softmax_step(qc, k_ref[:, sl], v_ref[:, sl], init)
            outs.append(acc / jnp.sum(l, axis=-1, keepdims=True))
        o_ref[:, sl] = _head_norm(outs[0] - lam * outs[1], sg_ref[...]).astype(BF16)


def _attn_ctx(q, k, v, lam_p, subln_g):
    seq = pl.BlockSpec((CTX_LEN, D), lambda i: (i, 0))
    return pl.pallas_call(
        _attn_ctx_kernel,
        out_shape=jax.ShapeDtypeStruct((NP, D), BF16),
        grid=(N_CTX_SEQ,),
        in_specs=[seq, seq, seq, _resident((4, HEAD_DIM)), _resident((1, V_DIM))],
        out_specs=seq,
        compiler_params=_params(("arbitrary",)),
        name="attn_ctx",
    )(q, k, v, lam_p, subln_g)


def _softmax_init(rows):
    return (jnp.full((rows, LANES), -jnp.inf, F32), jnp.zeros((rows, LANES), F32), jnp.zeros((rows, LANES), F32))


def _softmax_step(qc, kc, vc, state):
    m, l, acc = state
    s = lax.dot_general(qc, kc, _NT, preferred_element_type=F32)
    blocks = [s[:, j * LANES:(j + 1) * LANES] for j in range(s.shape[1] // LANES)]
    bm = functools.reduce(jnp.maximum, blocks)
    m_new = jnp.maximum(m, jnp.max(bm, axis=-1, keepdims=True))
    alpha = jnp.exp2(m - m_new)
    ps = [jnp.exp2(b - m_new) for b in blocks]
    l_new = alpha * l + functools.reduce(jnp.add, ps)
    p = jnp.concatenate(ps, axis=1).astype(BF16)
    acc_new = alpha * acc + jnp.dot(p, vc, preferred_element_type=F32)
    return m_new, l_new, acc_new


def _attn_lat_kernel(q_ref, k_ref, v_ref, ck_ref, cv_ref, lp_ref, sg_ref, o_ref):
    q = q_ref[...]
    zero = jnp.zeros_like(q)
    map0 = lax.broadcasted_iota(I32, (TQ, LANES), 1) < HEAD_DIM
    qs = (jnp.where(map0, q, zero), jnp.where(map0, zero, q))
    init = _softmax_init(TQ)
    states = [init, init]
    head_rows = pl.ds(pl.program_id(1), CTX_LEN, stride=N_HEADS)
    chunks = [(ck_ref[head_rows, :].astype(BF16), cv_ref[head_rows, :].astype(BF16))]
    chunks += [(k_ref[j * TK:(j + 1) * TK, :], v_ref[j * TK:(j + 1) * TK, :]) for j in range(LAT_LEN // TK)]
    for kc, vc in chunks:
        states = [_softmax_step(qs[c], kc, vc, states[c]) for c in range(2)]
    outs = [acc / jnp.sum(l, axis=-1, keepdims=True) for _, l, acc in states]
    o = outs[0] - _lam(lp_ref) * outs[1]
    o_ref[...] = _head_norm(o, sg_ref[...]).astype(BF16)


def _attn_lat(q, k, v, cache_k, cache_v, lam_p, subln_g):
    nq = LAT_LEN // TQ
    qspec = pl.BlockSpec((TQ, LANES), lambda b, h, i: (b * nq + i, h))
    kspec = pl.BlockSpec((LAT_LEN, LANES), lambda b, h, i: (b, h))
    cspec = pl.BlockSpec((None, CTX_LEN * N_HEADS, LANES), lambda b, h, i: (b, 0, 0))
    return pl.pallas_call(
        _attn_lat_kernel,
        out_shape=jax.ShapeDtypeStruct((NS, D), BF16),
        grid=(N_LAT_SEQ, N_HEADS, nq),
        in_specs=[qspec, kspec, kspec, cspec, cspec, _resident((4, HEAD_DIM)), _resident((1, V_DIM))],
        out_specs=qspec,
        compiler_params=_params(("arbitrary", "arbitrary", "arbitrary")),
        name="attn_lat",
    )(q, k, v, cache_k, cache_v, lam_p, subln_g)


def _oproj_route_kernel(oc_ref, ol_ref, x_ref, mod_ref, wo_ref, g_ref, wr_ref, tri_ref, below_ref, x3_ref, h_ref,
                        route_ref, cnt_ref):
    is_ctx = pl.program_id(0) < NP // (ROUTE_SUB * TM)
    for sub in range(ROUTE_SUB):
        rows = slice(sub * TM, (sub + 1) * TM)
        o = jnp.where(is_ctx, oc_ref[rows, :], ol_ref[rows, :])
        _route_tile(o, x_ref[rows, :], mod_ref, wo_ref, g_ref, wr_ref, tri_ref, below_ref,
                    x3_ref.at[rows, :], h_ref.at[rows, :], route_ref.at[rows, :], cnt_ref.at[sub])


def _route_tile(o, x, mod_ref, wo_ref, g_ref, wr_ref, tri_ref, below_ref, x3_ref, h_ref, route_ref, cnt_ref):
    x3 = x + mod_ref[2:3, :] * jnp.dot(o, wo_ref[...], preferred_element_type=F32)
    x3_ref[...] = x3
    hb = _modulate(x3, g_ref[...], mod_ref[4:5, :], mod_ref[3:4, :]).astype(BF16)
    h_ref[...] = hb

    logits = jnp.dot(hb, wr_ref[...], preferred_element_type=F32)
    lane = lax.broadcasted_iota(I32, (TM, LANES), 1).astype(F32)
    neg = jnp.float32(-jnp.inf)
    lg = jnp.where(lane < N_EXPERTS, logits, neg)
    v1 = jnp.max(lg, axis=-1, keepdims=True)
    i1 = jnp.min(jnp.where(lg == v1, lane, float(LANES)), axis=-1, keepdims=True)
    lg2 = jnp.where(lane == i1, neg, lg)
    v2 = jnp.max(lg2, axis=-1, keepdims=True)
    i2 = jnp.min(jnp.where(lg2 == v2, lane, float(LANES)), axis=-1, keepdims=True)
    e = jnp.exp(v2 - v1)
    g1 = 1.0 / (1.0 + e)
    g2 = e / (1.0 + e)
    sel1 = lane == i1
    sel2 = lane == i2
    onehot = jnp.where(sel1 | sel2, 1.0, 0.0)
    rank = jnp.dot(tri_ref[...], onehot.astype(BF16), preferred_element_type=F32)
    cnt8 = jnp.floor((jnp.sum(onehot, axis=0, keepdims=True) + (SEG_ALIGN - 1)) * (1.0 / SEG_ALIGN))
    base8 = jnp.dot(jnp.broadcast_to(cnt8, (8, LANES)).astype(BF16), below_ref[...], preferred_element_type=F32)
    slot = base8[0:1, :] * float(SEG_ALIGN) + rank
    pos1 = jnp.sum(jnp.where(sel1, slot, 0.0), axis=-1, keepdims=True)
    pos2 = jnp.sum(jnp.where(sel2, slot, 0.0), axis=-1, keepdims=True)
    cnt_ref[...] = cnt8
    route = jnp.where(lane == 0, i1, 0.0)
    route = jnp.where(lane == 1, i2, route)
    route = jnp.where(lane == 2, g1, route)
    route = jnp.where(lane == 3, g2, route)
    route = jnp.where(lane == 4, pos1, route)
    route = jnp.where(lane == 5, pos2, route)
    route_ref[...] = route


def _oproj_route(o_ctx, o_lat, x, mod, w_o, g, w_r, tb):
    rows = ROUTE_SUB * TM
    tok = pl.BlockSpec((rows, D), lambda i: (i, 0))
    ctx, lat = _two_stream_specs(rows, D)
    return pl.pallas_call(
        _oproj_route_kernel,
        out_shape=(jax.ShapeDtypeStruct((T, D), F32), jax.ShapeDtypeStruct((T, D), BF16),
                   jax.ShapeDtypeStruct((T, LANES), F32), jax.ShapeDtypeStruct((T // TM, 1, LANES), F32)),
        grid=(T // rows,),
        in_specs=[ctx, lat, tok, _mod_spec(1, rows), _resident((D, D)), _resident((1, D)), _resident((D, LANES)),
                  _resident((TM, TM)), _resident((LANES, LANES))],
        out_specs=(tok, tok, pl.BlockSpec((rows, LANES), lambda i: (i, 0)),
                   pl.BlockSpec((ROUTE_SUB, 1, LANES), lambda i: (i, 0, 0))),
        compiler_params=_params(("arbitrary",)),
        name="oproj_route",
    )(o_ctx, o_lat, x, mod, w_o, g, w_r, tb["tri"], tb["below"])


def _bit_copies(n, src0, dst0, bits, make_copy, op):
    for bit in bits:
        done = n & (-2 * bit)

        @pl.when((n & bit) != 0)
        def _(done=done, bit=bit):
            src = pl.multiple_of((src0 + done) * SEG_ALIGN, SEG_ALIGN)
            dst = pl.multiple_of((dst0 + done) * SEG_ALIGN, SEG_ALIGN)
            op(make_copy(src, dst, bit * SEG_ALIGN))


def _segment_copies(tab, make_copy, op):
    for e in range(N_EXPERTS):
        _bit_copies(tab(e), tab(N_EXPERTS + e), tab(2 * N_EXPERTS + e), SEG_BITS, make_copy, op)


def _start(copy):
    copy.start()


def _wait(copy):
    copy.wait()


def _dispatch_kernel(tab_ref, fill_ref, h_ref, route_ref, xs_ref, sorted_s, zero_s, sem, fill_sem):
    i = pl.program_id(0)
    slot = i % 2
    slots = route_ref[...].T
    row = lax.broadcasted_iota(I32, (L_TILE, TM), 0).astype(F32)
    perm = jnp.where((row == slots[4:5, :]) | (row == slots[5:6, :]), 1.0, 0.0).astype(BF16)
    sorted_s[slot] = jnp.dot(perm, h_ref[...], preferred_element_type=F32)

    def copies(tile, s, op):
        def seg(src, dst, rows):
            return pltpu.make_async_copy(sorted_s.at[s, pl.ds(src, rows), :], xs_ref.at[pl.ds(dst, rows), :], sem.at[s])

        _segment_copies(lambda j: tab_ref[tile, j], seg, op)

    copies(i, slot, _start)

    @pl.when(i > 0)
    def _():
        copies(i - 1, 1 - slot, _wait)

    @pl.when(i == pl.num_programs(0) - 1)
    def _():
        copies(i, slot, _wait)

    @pl.when(i == 0)
    def _():
        zero_s[...] = jnp.zeros(zero_s.shape, F32)

        def fill(src, dst, rows):
            del src
            return pltpu.make_async_copy(zero_s.at[pl.ds(0, rows), :], xs_ref.at[pl.ds(dst, rows), :], fill_sem)

        def fills(op):
            for e in range(N_EXPERTS):
                _bit_copies(fill_ref[0, N_EXPERTS + e], 0, fill_ref[0, e], SEG_BITS[1:], fill, op)
            for k in range(MAX_UNUSED_TILES):
                @pl.when(k < fill_ref[0, 2 * N_EXPERTS + 1])
                def _(k=k):
                    dst = pl.multiple_of(fill_ref[0, 2 * N_EXPERTS] * SEG_ALIGN + k * T_MOE, SEG_ALIGN)
                    op(fill(0, dst, T_MOE))

        fills(_start)
        fills(_wait)


def _dispatch(tab, fill, h, route):
    smem = pl.BlockSpec(memory_space=pltpu.SMEM)
    return pl.pallas_call(
        _dispatch_kernel,
        out_shape=jax.ShapeDtypeStruct((N_ROWS, D), F32),
        grid=(T // TM,),
        in_specs=[smem, smem, pl.BlockSpec((TM, D), lambda i: (i, 0)), pl.BlockSpec((TM, LANES), lambda i: (i, 0))],
        out_specs=pl.BlockSpec(memory_space=pl.ANY),
        scratch_shapes=[pltpu.VMEM((2, L_TILE, D), F32), pltpu.VMEM((T_MOE, D), F32),
                        pltpu.SemaphoreType.DMA((2,)), pltpu.SemaphoreType.DMA(())],
        compiler_params=_params(("arbitrary",)),
        name="moe_dispatch",
    )(tab, fill, h, route)


def _moe_kernel(te_ref, tv_ref, xs_ref, w1_ref, w3_ref, w2_ref, y_ref, xb_s, acc_s):
    del te_ref
    i = pl.program_id(0)
    f = pl.program_id(1)
    valid = tv_ref[i] == 1

    @pl.when(f == 0)
    def _():
        xb_s[...] = xs_ref[...].astype(BF16)
        acc_s[...] = jnp.zeros(acc_s.shape, F32)

    @pl.when(valid)
    def _():
        x = xb_s[...]
        a = jnp.dot(x, w1_ref[...], preferred_element_type=F32)
        b = jnp.dot(x, w3_ref[...], preferred_element_type=F32)
        acc_s[...] += jnp.dot((_silu(a) * b).astype(BF16), w2_ref[...], preferred_element_type=F32)

    @pl.when(f == N_F_MOE - 1)
    def _():
        y_ref[...] = acc_s[...]


def _moe(tile_expert, tile_valid, xs, w1, w3, w2):
    def fidx(i, f, te, tv):
        return jnp.where(tv[i] == 1, f, N_F_MOE - 1)

    grid_spec = pltpu.PrefetchScalarGridSpec(
        num_scalar_prefetch=2,
        grid=(N_MOE_TILES, N_F_MOE),
        in_specs=[
            pl.BlockSpec((T_MOE, D), lambda i, f, te, tv: (i, 0)),
            pl.BlockSpec((None, D, F_MOE), lambda i, f, te, tv: (te[i], 0, fidx(i, f, te, tv))),
            pl.BlockSpec((None, D, F_MOE), lambda i, f, te, tv: (te[i], 0, fidx(i, f, te, tv))),
            pl.BlockSpec((None, F_MOE, D), lambda i, f, te, tv: (te[i], fidx(i, f, te, tv), 0)),
        ],
        out_specs=pl.BlockSpec((T_MOE, D), lambda i, f, te, tv: (i, 0)),
        scratch_shapes=[pltpu.VMEM((T_MOE, D), BF16), pltpu.VMEM((T_MOE, D), F32)],
    )
    return pl.pallas_call(
        _moe_kernel,
        out_shape=jax.ShapeDtypeStruct((N_ROWS, D), F32),
        grid_spec=grid_spec,
        compiler_params=_params(("arbitrary", "arbitrary")),
        name="moe_experts",
    )(tile_expert, tile_valid, xs, w1, w3, w2)


def _combine_kernel(tab_ref, y_ref, x_ref, route_ref, mod_ref, fg_ref, o_ref, buf, sem, *, first_tile):
    i = pl.program_id(0)
    slot = i % 2

    def fetch(tile, s, op):
        def seg(src, dst, rows):
            return pltpu.make_async_copy(y_ref.at[pl.ds(dst, rows), :], buf.at[s, pl.ds(src, rows), :], sem.at[s])

        if op is _start:
            buf[s] = jnp.zeros((L_TILE, D), F32)
        _segment_copies(lambda j: tab_ref[tile, j], seg, op)

    @pl.when(i == 0)
    def _():
        fetch(first_tile, 0, _start)

    @pl.when(i + 1 < pl.num_programs(0))
    def _():
        fetch(first_tile + i + 1, 1 - slot, _start)

    fetch(first_tile + i, slot, _wait)
    yb = buf[slot].astype(BF16)
    route = route_ref[...]
    col = lax.broadcasted_iota(I32, (TM, L_TILE), 1).astype(F32)
    picks = [jnp.dot(jnp.where(col == route[:, c:c + 1], 1.0, 0.0).astype(BF16), yb, preferred_element_type=F32)
             for c in (4, 5)]
    moe = route[:, 2:3] * picks[0] + route[:, 3:4] * picks[1]
    x = x_ref[...] + mod_ref[5:6, :] * moe
    ms = jnp.mean(x * x, axis=-1, keepdims=True)
    o_ref[...] = x * lax.rsqrt(ms + EPS) * fg_ref[...]


def _combine(tab, y, x, route, mod, final_g, n_tok, off_tiles):
    tok = lambda w: pl.BlockSpec((TM, w), lambda i: (i + off_tiles, 0))
    return pl.pallas_call(
        functools.partial(_combine_kernel, first_tile=off_tiles),
        out_shape=jax.ShapeDtypeStruct((n_tok, D), F32),
        grid=(n_tok // TM,),
        in_specs=[
            pl.BlockSpec(memory_space=pltpu.SMEM),
            pl.BlockSpec(memory_space=pl.ANY),
            tok(D), tok(LANES), _mod_spec(1, TM, off_tiles), _resident((1, D)),
        ],
        out_specs=pl.BlockSpec((TM, D), lambda i: (i, 0)),
        scratch_shapes=[pltpu.VMEM((2, L_TILE, D), F32), pltpu.SemaphoreType.DMA((2,))],
        compiler_params=_params(("arbitrary",)),
        name="moe_combine",
    )(tab, y, x, route, mod, final_g)


def kernel(x_prompt, x_sample, cache_k, cache_v, c, c_ctx, ada_w, ada_b, norm1_g, norm2_g, final_g, mix_w_in, sgu_g, sgu_w, sgu_b, mix_w_out, ffn_w1, ffn_w3, ffn_w2, attn_w_qkv, lam_q1, lam_k1, lam_q2, lam_k2, subln_g, attn_w_o, router_w, moe_w1, moe_w3, moe_w2):
    tb = _device_tables()
    xc = x_prompt.reshape(NP, D)
    xl = x_sample.reshape(NS, D)
    cvec = jnp.concatenate([c, c_ctx[None, :], jnp.zeros((3, D), F32)], axis=0)
    mod = _adaln(cvec, ada_w, ada_b)

    bias = jnp.repeat(sgu_b[0].T, GW, axis=1)
    a, y, y1r, y2r = _inproj(xc, xl, mod, norm1_g[0][None, :], mix_w_in[0].astype(BF16), sgu_g[0].reshape(1, SGU_W),
                             sgu_w[0].astype(BF16), bias, tb)
    w_out = mix_w_out[0].astype(BF16)
    x1c = _mixout_ctx(y, a, xc, mod, w_out, tb)
    br, bi = _dft1(y1r.reshape(N_LAT_SEQ, DFT_N1, DFT_N2 * FNET_W), y2r.reshape(N_LAT_SEQ, DFT_N1, DFT_N2 * FNET_W), tb)
    x1l = _mixout_lat(br, bi, a.reshape(T // LAT_LEN, DFT_N2, DFT_N1, SGU_W),
                      xl.reshape(N_LAT_SEQ, DFT_N2, DFT_N1, D), mod, w_out).reshape(NS, D)
    x2 = _ffn(x1c, x1l, mod, norm2_g[0][None, :], ffn_w1[0].astype(BF16), ffn_w3[0].astype(BF16), ffn_w2[0].astype(BF16))

    qkv_c, qkv_l, kf, vf = _qkv(x2, mod, norm1_g[1][None, :], attn_w_qkv[0].astype(BF16), tb)
    lam_p = jnp.stack([lam_q1[0], lam_k1[0], lam_q2[0], lam_k2[0]], axis=0)
    sg = subln_g[0][None, :]
    oc = _attn_ctx(*qkv_c, lam_p, sg)
    cache_rows = (N_LAT_SEQ, CTX_LEN * N_HEADS, LANES)
    ol = _attn_lat(*qkv_l, cache_k.reshape(cache_rows), cache_v.reshape(cache_rows), lam_p, sg)

    w_r = jnp.pad(router_w[0], ((0, 0), (0, LANES - N_EXPERTS))).astype(BF16)
    x3, h2, route, cnt = _oproj_route(oc, ol, x2, mod, attn_w_o[0].astype(BF16), norm2_g[1][None, :], w_r, tb)
    unit_per_tile = T_MOE // SEG_ALIGN
    seg_len = cnt[:, 0, :N_EXPERTS].astype(I32)
    seg_local = jnp.cumsum(seg_len, axis=1) - seg_len
    group_len = jnp.sum(seg_len, axis=0)
    group_tiles = (group_len + unit_per_tile - 1) // unit_per_tile
    tile_ends = jnp.cumsum(group_tiles)
    group_start = (tile_ends - group_tiles) * unit_per_tile
    seg_global = group_start[None, :] + jnp.cumsum(seg_len, axis=0) - seg_len
    tab = jnp.concatenate([seg_len, seg_local, seg_global], axis=1)
    tab = jnp.pad(tab, ((0, 0), (0, LANES - 3 * N_EXPERTS)))
    fill = jnp.concatenate([group_start + group_len, group_tiles * unit_per_tile - group_len,
                            tile_ends[-1:] * unit_per_tile, N_MOE_TILES - tile_ends[-1:]])
    fill = jnp.pad(fill, (0, LANES - 2 * N_EXPERTS - 2)).reshape(1, LANES)
    tile_idx = jnp.arange(N_MOE_TILES, dtype=I32)
    tile_valid = (tile_idx < tile_ends[-1]).astype(I32)
    tile_expert = jnp.sum((jnp.minimum(tile_idx, tile_ends[-1] - 1)[:, None] >= tile_ends[None, :]).astype(I32), axis=-1)

    xs = _dispatch(tab, fill, h2, route)
    ys = _moe(tile_expert, tile_valid, xs, moe_w1[0].astype(BF16), moe_w3[0].astype(BF16), moe_w2[0].astype(BF16))
    fg = final_g[None, :]
    y_prompt = _combine(tab, ys, x3, route, mod, fg, NP, 0)
    y_sample = _combine(tab, ys, x3, route, mod, fg, NS, NP // TM)

    return (y_prompt.reshape(N_CTX_SEQ, CTX_LEN, D), y_sample.reshape(N_LAT_SEQ, LAT_LEN, D),
            kf.reshape(N_CTX_SEQ, 1, CTX_LEN, N_HEADS, 2 * HEAD_DIM), vf.reshape(N_CTX_SEQ, 1, CTX_LEN, N_HEADS, V_DIM))
```

```python
import functools
import math

import numpy as np
import jax
import jax.numpy as jnp
from jax import lax
from jax.experimental import pallas as pl
from jax.experimental.pallas import tpu as pltpu

F32 = jnp.float32
BF16 = jnp.bfloat16
I32 = jnp.int32

D = 1024
N_CTX_SEQ = 32
CTX_LEN = 256
N_LAT_SEQ = 4
LAT_LEN = 4096
NP = N_CTX_SEQ * CTX_LEN
NS = N_LAT_SEQ * LAT_LEN
T = NP + NS
GRID_W = 64
CHUNK = 128
SGU_W = 512
FNET_W = 512
GW = 128
N_HEADS = 8
HEAD_DIM = 64
V_DIM = 128
ROPE_THETA = 10000.0
D_FF = 2816
N_EXPERTS = 8
D_FF_EXPERT = 3584
EPS = 1e-6
LAM0 = 0.8 - 0.6 * math.exp(-0.3 * 1)
CTX_ROW = 4

LANES = 128
TM = 512
TQ = 1024
TK = 2048
Q_SCALE = HEAD_DIM ** -0.5 * math.log2(math.e)
T_MOE = 512
F_MOE = 1792
N_F_MOE = D_FF_EXPERT // F_MOE
assert N_F_MOE == 2
SEG_ALIGN = 8
SEG_BITS = (64, 32, 16, 8, 4, 2, 1)
L_TILE = 1152
N_ROWS = -(-(2 * T + (T // TM) * N_EXPERTS * (SEG_ALIGN - 1) + N_EXPERTS * (T_MOE - SEG_ALIGN)) // T_MOE) * T_MOE
N_MOE_TILES = N_ROWS // T_MOE
MAX_UNUSED_TILES = N_MOE_TILES - 2 * T // T_MOE
FFN_CHUNK = 2816
DFT_N1 = 256
DFT_N2 = 16
DFT_ROWS = 32
DFT1_GROUP = 4
CTX_PER_STEP = 2
ROUTE_SUB = 2
VMEM_LIMIT = 56 * 2 ** 20


def _params(sem, vmem=VMEM_LIMIT):
    return pltpu.CompilerParams(dimension_semantics=sem, vmem_limit_bytes=vmem)


def _resident(shape):
    nd = len(shape)
    return pl.BlockSpec(shape, lambda *_: (0,) * nd, pipeline_mode=pl.Buffered(1))


def _mod_row(i, tm):
    npt = NP // tm
    return jnp.where(i < npt, CTX_ROW, (i - npt) // (LAT_LEN // tm))


def _mod_spec(layer, tm, offset_tiles=0):
    return pl.BlockSpec((None, None, 6, D), lambda i, *_: (layer, _mod_row(i + offset_tiles, tm), 0, 0))


def _two_stream_specs(tm, width):
    npt = NP // tm
    ctx = pl.BlockSpec((tm, width), lambda i: (jnp.minimum(i, npt - 1), 0))
    lat = pl.BlockSpec((tm, width), lambda i: (jnp.maximum(i - npt, 0), 0))
    return ctx, lat


def _pick_stream(tm, ctx_ref, lat_ref):
    return jnp.where(pl.program_id(0) < NP // tm, ctx_ref[...], lat_ref[...])


def _modulate(x, g, scale, shift):
    ms = jnp.mean(x * x, axis=-1, keepdims=True)
    return x * lax.rsqrt(ms + EPS) * (g * (1.0 + scale)) + shift


def _silu(a):
    return a * jax.nn.sigmoid(a)


def _gelu_tanh(x):
    return 0.5 * x * (1.0 + jnp.tanh(0.7978845608028654 * (x + 0.044715 * (x * x * x))))


@functools.lru_cache(maxsize=None)
def _tables():
    def cs(n):
        k = np.arange(n, dtype=np.int64)
        ang = 2.0 * np.pi * ((k[:, None] * k[None, :]) % n) / n
        return np.cos(ang), np.sin(ang)

    c128, s128 = cs(GW)
    dft_ch = np.concatenate([c128, s128], axis=1)
    c256, s256 = cs(CTX_LEN)
    dft_ctx = np.concatenate([c256, -s256], axis=1)
    k1 = np.arange(DFT_N1, dtype=np.int64)[:, None]
    n2 = np.arange(DFT_N2, dtype=np.int64)[None, :]
    tw = 2.0 * np.pi * ((k1 * n2) % LAT_LEN) / LAT_LEN
    half = HEAD_DIM // 2
    inv_freq = ROPE_THETA ** (-np.arange(0, half, 2, dtype=np.float64) / half)
    t = np.arange(LAT_LEN)
    ang_r = (t // GRID_W).astype(np.float64)[:, None] * inv_freq[None, :]
    ang_c = (t % GRID_W).astype(np.float64)[:, None] * inv_freq[None, :]

    def blk(ang):
        c = np.cos(ang)
        s = np.sin(ang)
        return np.concatenate([c, c], axis=1), np.concatenate([-s, s], axis=1)

    cr, sr = blk(ang_r)
    cc, sc = blk(ang_c)
    cos64 = np.concatenate([cr, cc], axis=1)
    sin64 = np.concatenate([sr, sc], axis=1)
    rope_cos = np.concatenate([cos64, cos64], axis=1).astype(np.float32)
    rope_sin = np.concatenate([sin64, sin64], axis=1).astype(np.float32)
    tri = np.tril(np.ones((TM, TM), np.float32), k=-1)
    below = np.triu(np.ones((LANES, LANES), np.float32), k=1)
    f32 = lambda a: np.asarray(a, np.float32)
    return dict(dft_ch=f32(dft_ch), dft_ctx=f32(dft_ctx), c256=f32(c256), s256=f32(s256),
                tw_cos=f32(np.cos(tw)), tw_sin=f32(np.sin(tw)),
                rope_cos=rope_cos, rope_sin=rope_sin, tri=tri, below=below)


def _device_tables():
    tb = {k: jnp.asarray(v) for k, v in _tables().items()}
    for k in ("dft_ch", "dft_ctx", "c256", "s256", "tri", "below"):
        tb[k] = tb[k].astype(BF16)
    return tb


def _adaln_kernel(c_ref, w_ref, b_ref, o_ref):
    s = _silu(c_ref[...]).astype(BF16)
    o_ref[...] = jnp.dot(s, w_ref[...].astype(BF16), preferred_element_type=F32) + b_ref[...]


def _adaln(cvec, ada_w, ada_b):
    tn = 1536
    out = pl.pallas_call(
        _adaln_kernel,
        out_shape=jax.ShapeDtypeStruct((2, 8, 6 * D), F32),
        grid=(2, 6 * D // tn),
        in_specs=[
            pl.BlockSpec((8, D), lambda l, j: (0, 0)),
            pl.BlockSpec((None, D, tn), lambda l, j: (l, 0, j)),
            pl.BlockSpec((None, 1, tn), lambda l, j: (l, 0, j)),
        ],
        out_specs=pl.BlockSpec((None, 8, tn), lambda l, j: (l, 0, j)),
        compiler_params=_params(("arbitrary", "arbitrary")),
        name="adaln",
    )(cvec, ada_w, ada_b.reshape(2, 1, 6 * D))
    return out.reshape(2, 8, 6, D)


def _inproj_kernel(xc_ref, xl_ref, mod_ref, g_ref, win_ref, sgug_ref, sguw_ref, sgub_ref, dft_ref, a_ref, y_ref,
                   y1r_ref, y2r_ref, slab_s):
    is_ctx = pl.program_id(0) < NP // TM
    x = _pick_stream(TM, xc_ref, xl_ref)
    h = _modulate(x, g_ref[...], mod_ref[1:2, :], mod_ref[0:1, :]).astype(BF16)
    p = jnp.dot(h, win_ref[...], preferred_element_type=F32)
    act = _gelu_tanh(p[:, :2 * SGU_W])
    spectra = []
    for g in range(4):
        lo, hi = g * GW, (g + 1) * GW
        u = act[:, lo:hi]
        v = act[:, SGU_W + lo:SGU_W + hi]
        ms = jnp.mean(v * v, axis=-1, keepdims=True)
        vn = (v * lax.rsqrt(ms + EPS) * sgug_ref[:, lo:hi]).astype(BF16)
        w = sguw_ref[g]
        for c in range(TM // CHUNK):
            r0, r1 = c * CHUNK, (c + 1) * CHUNK
            mix = jnp.dot(w, vn[r0:r1, :], preferred_element_type=F32) + sgub_ref[:, lo:hi]
            a_ref[r0:r1, lo:hi] = (u[r0:r1, :] * mix).astype(BF16)
        fg = p[:, 2 * SGU_W + lo:2 * SGU_W + hi].astype(BF16)
        spectra.append(jnp.dot(fg, dft_ref[...], preferred_element_type=F32))

    @pl.when(is_ctx)
    def _():
        for g, yy in enumerate(spectra):
            y_ref[:, g * GW:(g + 1) * GW] = yy[:, :GW].astype(BF16)
            y_ref[:, FNET_W + g * GW:FNET_W + (g + 1) * GW] = yy[:, GW:].astype(BF16)

    @pl.when(jnp.logical_not(is_ctx))
    def _():
        for g, yy in enumerate(spectra):
            for half, dst in ((0, y1r_ref), (1, y2r_ref)):
                slab = slab_s.at[2 * g + half]
                slab[...] = yy[:, half * GW:(half + 1) * GW]
                for n2 in range(DFT_N2):
                    cols = slice(n2 * FNET_W + g * GW, n2 * FNET_W + (g + 1) * GW)
                    dst[:, cols] = slab[pl.ds(n2, TM // DFT_N2, stride=DFT_N2), :].astype(BF16)


def _inproj(x_ctx, x_lat, mod, g, w_in, sgu_g, sgu_w, sgu_b, tb):
    ctx, lat = _two_stream_specs(TM, D)
    npt = NP // TM
    rows_r = TM // DFT_N2
    yr_shape = jax.ShapeDtypeStruct((NS // DFT_N2, DFT_N2 * FNET_W), BF16)
    yr_spec = pl.BlockSpec((rows_r, DFT_N2 * FNET_W), lambda i: (jnp.maximum(i - npt, 0), 0))
    return pl.pallas_call(
        _inproj_kernel,
        out_shape=(jax.ShapeDtypeStruct((T, SGU_W), BF16), jax.ShapeDtypeStruct((NP, 2 * FNET_W), BF16),
                   yr_shape, yr_shape),
        grid=(T // TM,),
        in_specs=[
            ctx, lat,
            _mod_spec(0, TM),
            _resident((1, D)),
            _resident((D, 3 * SGU_W)),
            _resident((1, SGU_W)),
            _resident((4, CHUNK, CHUNK)),
            _resident((CHUNK, SGU_W)),
            _resident((GW, 2 * GW)),
        ],
        out_specs=(pl.BlockSpec((TM, SGU_W), lambda i: (i, 0)),
                   pl.BlockSpec((TM, 2 * FNET_W), lambda i: (jnp.minimum(i, npt - 1), 0)), yr_spec, yr_spec),
        scratch_shapes=[pltpu.VMEM((8, TM, LANES), F32)],
        compiler_params=_params(("arbitrary",)),
        name="inproj_mix",
    )(x_ctx, x_lat, mod, g, w_in, sgu_g, sgu_w, sgu_b, tb["dft_ch"])


def _mix_out(f, a_ref, x_ref, mod_ref, wout_ref, o_ref):
    mix = (jnp.dot(a_ref[...], wout_ref[:SGU_W, :], preferred_element_type=F32)
           + jnp.dot(f.astype(BF16), wout_ref[SGU_W:, :], preferred_element_type=F32))
    o_ref[...] = x_ref[...] + mod_ref[2:3, :] * mix


def _mixout_ctx_kernel(y_ref, a_ref, x_ref, mod_ref, dft_ref, wout_ref, o_ref):
    fs = []
    for s in range(CTX_PER_STEP):
        rows = slice(s * CTX_LEN, (s + 1) * CTX_LEN)
        fs.append(jnp.dot(dft_ref[:, :CTX_LEN], y_ref[rows, :FNET_W], preferred_element_type=F32)
                  + jnp.dot(dft_ref[:, CTX_LEN:], y_ref[rows, FNET_W:], preferred_element_type=F32))
    f = jnp.concatenate(fs, axis=0) * (1.0 / math.sqrt(CTX_LEN * GW))
    _mix_out(f, a_ref, x_ref, mod_ref, wout_ref, o_ref)


def _mixout_ctx(y, a, x, mod, w_out, tb):
    rows = CTX_PER_STEP * CTX_LEN
    tok = lambda w: pl.BlockSpec((rows, w), lambda i: (i, 0))
    return pl.pallas_call(
        _mixout_ctx_kernel,
        out_shape=jax.ShapeDtypeStruct((NP, D), F32),
        grid=(N_CTX_SEQ // CTX_PER_STEP,),
        in_specs=[tok(2 * FNET_W), tok(SGU_W), tok(D),
                  pl.BlockSpec((None, None, 6, D), lambda i: (0, CTX_ROW, 0, 0)),
                  _resident((CTX_LEN, 2 * CTX_LEN)), _resident((D, D))],
        out_specs=tok(D),
        compiler_params=_params(("arbitrary",)),
        name="mixout_ctx",
    )(y, a, x, mod, tb["dft_ctx"], w_out)


def _dft1_kernel(y1_ref, y2_ref, c_ref, s_ref, twc_ref, tws_ref, br_ref, bi_ref):
    y1 = y1_ref[...]
    y2 = y2_ref[...]
    c = c_ref[...]
    s = s_ref[...]
    ar = jnp.dot(c, y1, preferred_element_type=F32) - jnp.dot(s, y2, preferred_element_type=F32)
    ai = -(jnp.dot(c, y2, preferred_element_type=F32) + jnp.dot(s, y1, preferred_element_type=F32))
    lane = lax.broadcasted_iota(I32, (DFT_N1, DFT_N2), 1)
    for r in range(DFT1_GROUP):
        n2 = pl.program_id(1) * DFT1_GROUP + r
        cols = slice(r * FNET_W, (r + 1) * FNET_W)
        tc = jnp.sum(jnp.where(lane == n2, twc_ref[...], 0.0), axis=-1, keepdims=True)
        ts = jnp.sum(jnp.where(lane == n2, tws_ref[...], 0.0), axis=-1, keepdims=True)
        br_ref[r] = (tc * ar[:, cols] + ts * ai[:, cols]).astype(BF16)
        bi_ref[r] = (tc * ai[:, cols] - ts * ar[:, cols]).astype(BF16)


def _dft1(y1r, y2r, tb):
    shp = jax.ShapeDtypeStruct((N_LAT_SEQ, DFT_N2, DFT_N1, FNET_W), BF16)
    src = pl.BlockSpec((None, DFT_N1, DFT1_GROUP * FNET_W), lambda b, n: (b, 0, n))
    dst = pl.BlockSpec((None, DFT1_GROUP, DFT_N1, FNET_W), lambda b, n: (b, n, 0, 0))
    return pl.pallas_call(
        _dft1_kernel,
        out_shape=(shp, shp),
        grid=(N_LAT_SEQ, DFT_N2 // DFT1_GROUP),
        in_specs=[src, src, _resident((DFT_N1, DFT_N1)), _resident((DFT_N1, DFT_N1)),
                  _resident((DFT_N1, DFT_N2)), _resident((DFT_N1, DFT_N2))],
        out_specs=(dst, dst),
        compiler_params=_params(("arbitrary", "arbitrary")),
        name="dft_stage1",
    )(y1r, y2r, tb["c256"], tb["s256"], tb["tw_cos"], tb["tw_sin"])


def _cmul_const(z, wr, wi):
    re, im = z
    tol = 1e-12
    if abs(wi) < tol:
        return (re, im) if wr > 0 else (-re, -im)
    if abs(wr) < tol:
        return (-im, re) if wi > 0 else (im, -re)
    return (re * wr - im * wi, re * wi + im * wr)


def _fft(zs):
    n = len(zs)
    if n == 1:
        return zs
    even, odd = _fft(zs[0::2]), _fft(zs[1::2])
    out = [None] * n
    for k in range(n // 2):
        ang = 2.0 * math.pi * k / n
        t = _cmul_const(odd[k], math.cos(ang), -math.sin(ang))
        out[k] = (even[k][0] + t[0], even[k][1] + t[1])
        out[k + n // 2] = (even[k][0] - t[0], even[k][1] - t[1])
    return out


def _mixout_lat_kernel(br_ref, bi_ref, a_ref, x_ref, mod_ref, wout_ref, o_ref):
    spec = _fft([(br_ref[n2].astype(F32), bi_ref[n2].astype(F32)) for n2 in range(DFT_N2)])
    f = jnp.concatenate([re for re, _ in spec], axis=0) * (1.0 / math.sqrt(LAT_LEN * GW))
    rows = DFT_N2 * DFT_ROWS
    mix = (jnp.dot(a_ref[...].reshape(rows, SGU_W), wout_ref[:SGU_W, :], preferred_element_type=F32)
           + jnp.dot(f.astype(BF16), wout_ref[SGU_W:, :], preferred_element_type=F32))
    out = x_ref[...].reshape(rows, D) + mod_ref[2:3, :] * mix
    o_ref[...] = out.reshape(DFT_N2, DFT_ROWS, D)


def _mixout_lat(br, bi, a_lat, x, mod, w_out):
    blk = lambda w, off=0: pl.BlockSpec((None, DFT_N2, DFT_ROWS, w), lambda b, r: (b + off, 0, r, 0))
    return pl.pallas_call(
        _mixout_lat_kernel,
        out_shape=jax.ShapeDtypeStruct((N_LAT_SEQ, DFT_N2, DFT_N1, D), F32),
        grid=(N_LAT_SEQ, DFT_N1 // DFT_ROWS),
        in_specs=[blk(FNET_W), blk(FNET_W), blk(SGU_W, NP // LAT_LEN), blk(D),
                  pl.BlockSpec((None, None, 6, D), lambda b, r: (0, b, 0, 0)),
                  _resident((D, D))],
        out_specs=blk(D),
        compiler_params=_params(("arbitrary", "arbitrary")),
        name="mixout_lat",
    )(br, bi, a_lat, x, mod, w_out)


def _ffn_kernel(xc_ref, xl_ref, mod_ref, g_ref, w1_ref, w3_ref, w2_ref, o_ref):
    x = _pick_stream(TM, xc_ref, xl_ref)
    h = _modulate(x, g_ref[...], mod_ref[4:5, :], mod_ref[3:4, :]).astype(BF16)
    acc = None
    for c in range(D_FF // FFN_CHUNK):
        lo, hi = c * FFN_CHUNK, (c + 1) * FFN_CHUNK
        a = jnp.dot(h, w1_ref[:, lo:hi], preferred_element_type=F32)
        b = jnp.dot(h, w3_ref[:, lo:hi], preferred_element_type=F32)
        d = jnp.dot((_silu(a) * b).astype(BF16), w2_ref[lo:hi, :], preferred_element_type=F32)
        acc = d if acc is None else acc + d
    o_ref[...] = x + mod_ref[5:6, :] * acc


def _ffn(x_ctx, x_lat, mod, g, w1, w3, w2):
    ctx, lat = _two_stream_specs(TM, D)
    return pl.pallas_call(
        _ffn_kernel,
        out_shape=jax.ShapeDtypeStruct((T, D), F32),
        grid=(T // TM,),
        in_specs=[
            ctx, lat,
            _mod_spec(0, TM),
            _resident((1, D)),
            _resident((D, D_FF)),
            _resident((D, D_FF)),
            _resident((D_FF, D)),
        ],
        out_specs=pl.BlockSpec((TM, D), lambda i: (i, 0)),
        compiler_params=_params(("arbitrary",)),
        name="ffn",
    )(x_ctx, x_lat, mod, g, w1, w3, w2)


def _qkv_ctx_kernel(x_ref, mod_ref, g_ref, w_ref, q_ref, k_ref, v_ref, kf_ref, vf_ref):
    h = _modulate(x_ref[...], g_ref[...], mod_ref[1:2, :], mod_ref[0:1, :]).astype(BF16)
    qkv = jnp.dot(h, w_ref[...], preferred_element_type=F32)
    k = qkv[:, D:2 * D]
    v = qkv[:, 2 * D:]
    q_ref[...] = (qkv[:, :D] * Q_SCALE).astype(BF16)
    k_ref[...] = k.astype(BF16)
    v_ref[...] = v.astype(BF16)
    kf_ref[...] = k
    vf_ref[...] = v


def _qkv_lat_kernel(x_ref, mod_ref, g_ref, w_ref, cos_ref, sin_ref, q_ref, k_ref, v_ref):
    h = _modulate(x_ref[...], g_ref[...], mod_ref[1:2, :], mod_ref[0:1, :]).astype(BF16)
    qkv = jnp.dot(h, w_ref[...], preferred_element_type=F32)
    cos = cos_ref[...]
    sin = sin_ref[...]
    first = (lax.broadcasted_iota(I32, (TM, LANES), 1) & 31) < 16

    def rotate(xh):
        partner = jnp.where(first, pltpu.roll(xh, LANES - 16, 1), pltpu.roll(xh, 16, 1))
        return xh * cos + partner * sin

    for hh in range(N_HEADS):
        sl = slice(hh * LANES, (hh + 1) * LANES)
        q_ref[:, sl] = (rotate(qkv[:, sl]) * Q_SCALE).astype(BF16)
        k_ref[:, sl] = rotate(qkv[:, D + hh * LANES:D + (hh + 1) * LANES]).astype(BF16)
    v_ref[...] = qkv[:, 2 * D:].astype(BF16)


def _qkv(x, mod, g, w_qkv, tb):
    tok = pl.BlockSpec((TM, D), lambda i: (i, 0))
    bshape = jax.ShapeDtypeStruct((NP, D), BF16)
    lshape = jax.ShapeDtypeStruct((NS, D), BF16)
    fshape = jax.ShapeDtypeStruct((NP, D), F32)
    qc, kc, vc, kf, vf = pl.pallas_call(
        _qkv_ctx_kernel,
        out_shape=(bshape, bshape, bshape, fshape, fshape),
        grid=(NP // TM,),
        in_specs=[tok, _mod_spec(1, TM), _resident((1, D)), _resident((D, 3 * D))],
        out_specs=(tok, tok, tok, tok, tok),
        compiler_params=_params(("arbitrary",)),
        name="qkv_ctx",
    )(x, mod, g, w_qkv)
    off = NP // TM
    ltok = pl.BlockSpec((TM, D), lambda i: (i + off, 0))
    rope = pl.BlockSpec((TM, LANES), lambda i: (i % (LAT_LEN // TM), 0))
    ql, kl, vl = pl.pallas_call(
        _qkv_lat_kernel,
        out_shape=(lshape, lshape, lshape),
        grid=(NS // TM,),
        in_specs=[ltok, _mod_spec(1, TM, off), _resident((1, D)), _resident((D, 3 * D)), rope, rope],
        out_specs=(tok, tok, tok),
        compiler_params=_params(("arbitrary",)),
        name="qkv_lat",
    )(x, mod, g, w_qkv, tb["rope_cos"], tb["rope_sin"])
    return (qc, kc, vc), (ql, kl, vl), kf, vf


def _lam(lp_ref):
    lp = lp_ref[...]
    a = jnp.sum(lp[0:1, :] * lp[1:2, :], axis=-1, keepdims=True)
    b = jnp.sum(lp[2:3, :] * lp[3:4, :], axis=-1, keepdims=True)
    return jnp.exp(a) - jnp.exp(b) + LAM0


def _head_norm(o, sg):
    ms = jnp.mean(o * o, axis=-1, keepdims=True)
    return o * lax.rsqrt(ms + EPS) * (sg * (1.0 - LAM0))


_NT = (((1,), (1,)), ((), ()))


def _attn_ctx_kernel(q_ref, k_ref, v_ref, lp_ref, sg_ref, o_ref):
    lam = _lam(lp_ref)
    map0 = lax.broadcasted_iota(I32, (CTX_LEN, LANES), 1) < HEAD_DIM
    init = _softmax_init(CTX_LEN)
    for hh in range(N_HEADS):
        sl = slice(hh * LANES, (hh + 1) * LANES)
        q = q_ref[:, sl]
        zero = jnp.zeros_like(q)
        outs = []
        for qc in (jnp.where(map0, q, zero), jnp.where(map0, zero, q)):
            _, l, acc = _softmax_step(qc, k_ref[:, sl], v_ref[:, sl], init)
            outs.append(acc / jnp.sum(l, axis=-1, keepdims=True))
        o_ref[:, sl] = _head_norm(outs[0] - lam * outs[1], sg_ref[...]).astype(BF16)


def _attn_ctx(q, k, v, lam_p, subln_g):
    seq = pl.BlockSpec((CTX_LEN, D), lambda i: (i, 0))
    return pl.pallas_call(
        _attn_ctx_kernel,
        out_shape=jax.ShapeDtypeStruct((NP, D), BF16),
        grid=(N_CTX_SEQ,),
        in_specs=[seq, seq, seq, _resident((4, HEAD_DIM)), _resident((1, V_DIM))],
        out_specs=seq,
        compiler_params=_params(("arbitrary",)),
        name="attn_ctx",
    )(q, k, v, lam_p, subln_g)


def _softmax_init(rows):
    return (jnp.full((rows, LANES), -jnp.inf, F32), jnp.zeros((rows, LANES), F32), jnp.zeros((rows, LANES), F32))


def _softmax_step(qc, kc, vc, state):
    m, l, acc = state
    s = lax.dot_general(qc, kc, _NT, preferred_element_type=F32)
    blocks = [s[:, j * LANES:(j + 1) * LANES] for j in range(s.shape[1] // LANES)]
    bm = functools.reduce(jnp.maximum, blocks)
    m_new = jnp.maximum(m, jnp.max(bm, axis=-1, keepdims=True))
    alpha = jnp.exp2(m - m_new)
    ps = [jnp.exp2(b - m_new) for b in blocks]
    l_new = alpha * l + functools.reduce(jnp.add, ps)
    p = jnp.concatenate(ps, axis=1).astype(BF16)
    acc_new = alpha * acc + jnp.dot(p, vc, preferred_element_type=F32)
    return m_new, l_new, acc_new


def _attn_lat_kernel(q_ref, k_ref, v_ref, ck_ref, cv_ref, lp_ref, sg_ref, o_ref):
    q = q_ref[...]
    zero = jnp.zeros_like(q)
    map0 = lax.broadcasted_iota(I32, (TQ, LANES), 1) < HEAD_DIM
    qs = (jnp.where(map0, q, zero), jnp.where(map0, zero, q))
    init = _softmax_init(TQ)
    states = [init, init]
    head_rows = pl.ds(pl.program_id(1), CTX_LEN, stride=N_HEADS)
    chunks = [(ck_ref[head_rows, :].astype(BF16), cv_ref[head_rows, :].astype(BF16))]
    chunks += [(k_ref[j * TK:(j + 1) * TK, :], v_ref[j * TK:(j + 1) * TK, :]) for j in range(LAT_LEN // TK)]
    for kc, vc in chunks:
        states = [_softmax_step(qs[c], kc, vc, states[c]) for c in range(2)]
    outs = [acc / jnp.sum(l, axis=-1, keepdims=True) for _, l, acc in states]
    o = outs[0] - _lam(lp_ref) * outs[1]
    o_ref[...] = _head_norm(o, sg_ref[...]).astype(BF16)


def _attn_lat(q, k, v, cache_k, cache_v, lam_p, subln_g):
    nq = LAT_LEN // TQ
    qspec = pl.BlockSpec((TQ, LANES), lambda b, h, i: (b * nq + i, h))
    kspec = pl.BlockSpec((LAT_LEN, LANES), lambda b, h, i: (b, h))
    cspec = pl.BlockSpec((None, CTX_LEN * N_HEADS, LANES), lambda b, h, i: (b, 0, 0))
    return pl.pallas_call(
        _attn_lat_kernel,
        out_shape=jax.ShapeDtypeStruct((NS, D), BF16),
        grid=(N_LAT_SEQ, N_HEADS, nq),
        in_specs=[qspec, kspec, kspec, cspec, cspec, _resident((4, HEAD_DIM)), _resident((1, V_DIM))],
        out_specs=qspec,
        compiler_params=_params(("arbitrary", "arbitrary", "arbitrary")),
        name="attn_lat",
    )(q, k, v, cache_k, cache_v, lam_p, subln_g)


def _oproj_route_kernel(oc_ref, ol_ref, x_ref, mod_ref, wo_ref, g_ref, wr_ref, tri_ref, below_ref, x3_ref, h_ref,
                        route_ref, cnt_ref):
    is_ctx = pl.program_id(0) < NP // (ROUTE_SUB * TM)
    for sub in range(ROUTE_SUB):
        rows = slice(sub * TM, (sub + 1) * TM)
        o = jnp.where(is_ctx, oc_ref[rows, :], ol_ref[rows, :])
        _route_tile(o, x_ref[rows, :], mod_ref, wo_ref, g_ref, wr_ref, tri_ref, below_ref,
                    x3_ref.at[rows, :], h_ref.at[rows, :], route_ref.at[rows, :], cnt_ref.at[sub])


def _route_tile(o, x, mod_ref, wo_ref, g_ref, wr_ref, tri_ref, below_ref, x3_ref, h_ref, route_ref, cnt_ref):
    x3 = x + mod_ref[2:3, :] * jnp.dot(o, wo_ref[...], preferred_element_type=F32)
    x3_ref[...] = x3
    hb = _modulate(x3, g_ref[...], mod_ref[4:5, :], mod_ref[3:4, :]).astype(BF16)
    h_ref[...] = hb

    logits = jnp.dot(hb, wr_ref[...], preferred_element_type=F32)
    lane = lax.broadcasted_iota(I32, (TM, LANES), 1).astype(F32)
    neg = jnp.float32(-jnp.inf)
    lg = jnp.where(lane < N_EXPERTS, logits, neg)
    v1 = jnp.max(lg, axis=-1, keepdims=True)
    i1 = jnp.min(jnp.where(lg == v1, lane, float(LANES)), axis=-1, keepdims=True)
    lg2 = jnp.where(lane == i1, neg, lg)
    v2 = jnp.max(lg2, axis=-1, keepdims=True)
    i2 = jnp.min(jnp.where(lg2 == v2, lane, float(LANES)), axis=-1, keepdims=True)
    e = jnp.exp(v2 - v1)
    g1 = 1.0 / (1.0 + e)
    g2 = e / (1.0 + e)
    sel1 = lane == i1
    sel2 = lane == i2
    onehot = jnp.where(sel1 | sel2, 1.0, 0.0)
    rank = jnp.dot(tri_ref[...], onehot.astype(BF16), preferred_element_type=F32)
    cnt8 = jnp.floor((jnp.sum(onehot, axis=0, keepdims=True) + (SEG_ALIGN - 1)) * (1.0 / SEG_ALIGN))
    base8 = jnp.dot(jnp.broadcast_to(cnt8, (8, LANES)).astype(BF16), below_ref[...], preferred_element_type=F32)
    slot = base8[0:1, :] * float(SEG_ALIGN) + rank
    pos1 = jnp.sum(jnp.where(sel1, slot, 0.0), axis=-1, keepdims=True)
    pos2 = jnp.sum(jnp.where(sel2, slot, 0.0), axis=-1, keepdims=True)
    cnt_ref[...] = cnt8
    route = jnp.where(lane == 0, i1, 0.0)
    route = jnp.where(lane == 1, i2, route)
    route = jnp.where(lane == 2, g1, route)
    route = jnp.where(lane == 3, g2, route)
    route = jnp.where(lane == 4, pos1, route)
    route = jnp.where(lane == 5, pos2, route)
    route_ref[...] = route


def _oproj_route(o_ctx, o_lat, x, mod, w_o, g, w_r, tb):
    rows = ROUTE_SUB * TM
    tok = pl.BlockSpec((rows, D), lambda i: (i, 0))
    ctx, lat = _two_stream_specs(rows, D)
    return pl.pallas_call(
        _oproj_route_kernel,
        out_shape=(jax.ShapeDtypeStruct((T, D), F32), jax.ShapeDtypeStruct((T, D), BF16),
                   jax.ShapeDtypeStruct((T, LANES), F32), jax.ShapeDtypeStruct((T // TM, 1, LANES), F32)),
        grid=(T // rows,),
        in_specs=[ctx, lat, tok, _mod_spec(1, rows), _resident((D, D)), _resident((1, D)), _resident((D, LANES)),
                  _resident((TM, TM)), _resident((LANES, LANES))],
        out_specs=(tok, tok, pl.BlockSpec((rows, LANES), lambda i: (i, 0)),
                   pl.BlockSpec((ROUTE_SUB, 1, LANES), lambda i: (i, 0, 0))),
        compiler_params=_params(("arbitrary",)),
        name="oproj_route",
    )(o_ctx, o_lat, x, mod, w_o, g, w_r, tb["tri"], tb["below"])


def _bit_copies(n, src0, dst0, bits, make_copy, op):
    for bit in bits:
        done = n & (-2 * bit)

        @pl.when((n & bit) != 0)
        def _(done=done, bit=bit):
            src = pl.multiple_of((src0 + done) * SEG_ALIGN, SEG_ALIGN)
            dst = pl.multiple_of((dst0 + done) * SEG_ALIGN, SEG_ALIGN)
            op(make_copy(src, dst, bit * SEG_ALIGN))


def _segment_copies(tab, make_copy, op):
    for e in range(N_EXPERTS):
        _bit_copies(tab(e), tab(N_EXPERTS + e), tab(2 * N_EXPERTS + e), SEG_BITS, make_copy, op)


def _start(copy):
    copy.start()


def _wait(copy):
    copy.wait()


def _dispatch_kernel(tab_ref, fill_ref, h_ref, route_ref, xs_ref, sorted_s, zero_s, sem, fill_sem):
    i = pl.program_id(0)
    slot = i % 2
    slots = route_ref[...].T
    row = lax.broadcasted_iota(I32, (L_TILE, TM), 0).astype(F32)
    perm = jnp.where((row == slots[4:5, :]) | (row == slots[5:6, :]), 1.0, 0.0).astype(BF16)
    sorted_s[slot] = jnp.dot(perm, h_ref[...], preferred_element_type=F32)

    def copies(tile, s, op):
        def seg(src, dst, rows):
            return pltpu.make_async_copy(sorted_s.at[s, pl.ds(src, rows), :], xs_ref.at[pl.ds(dst, rows), :], sem.at[s])

        _segment_copies(lambda j: tab_ref[tile, j], seg, op)

    copies(i, slot, _start)

    @pl.when(i > 0)
    def _():
        copies(i - 1, 1 - slot, _wait)

    @pl.when(i == pl.num_programs(0) - 1)
    def _():
        copies(i, slot, _wait)

    @pl.when(i == 0)
    def _():
        zero_s[...] = jnp.zeros(zero_s.shape, F32)

        def fill(src, dst, rows):
            del src
            return pltpu.make_async_copy(zero_s.at[pl.ds(0, rows), :], xs_ref.at[pl.ds(dst, rows), :], fill_sem)

        def fills(op):
            for e in range(N_EXPERTS):
                _bit_copies(fill_ref[0, N_EXPERTS + e], 0, fill_ref[0, e], SEG_BITS[1:], fill, op)
            for k in range(MAX_UNUSED_TILES):
                @pl.when(k < fill_ref[0, 2 * N_EXPERTS + 1])
                def _(k=k):
                    dst = pl.multiple_of(fill_ref[0, 2 * N_EXPERTS] * SEG_ALIGN + k * T_MOE, SEG_ALIGN)
                    op(fill(0, dst, T_MOE))

        fills(_start)
        fills(_wait)


def _dispatch(tab, fill, h, route):
    smem = pl.BlockSpec(memory_space=pltpu.SMEM)
    return pl.pallas_call(
        _dispatch_kernel,
        out_shape=jax.ShapeDtypeStruct((N_ROWS, D), F32),
        grid=(T // TM,),
        in_specs=[smem, smem, pl.BlockSpec((TM, D), lambda i: (i, 0)), pl.BlockSpec((TM, LANES), lambda i: (i, 0))],
        out_specs=pl.BlockSpec(memory_space=pl.ANY),
        scratch_shapes=[pltpu.VMEM((2, L_TILE, D), F32), pltpu.VMEM((T_MOE, D), F32),
                        pltpu.SemaphoreType.DMA((2,)), pltpu.SemaphoreType.DMA(())],
        compiler_params=_params(("arbitrary",)),
        name="moe_dispatch",
    )(tab, fill, h, route)


def _moe_kernel(te_ref, tv_ref, xs_ref, w1_ref, w3_ref, w2_ref, y_ref, xb_s, acc_s):
    del te_ref
    i = pl.program_id(0)
    f = pl.program_id(1)
    valid = tv_ref[i] == 1

    def chunk(x):
        a = jnp.dot(x, w1_ref[...], preferred_element_type=F32)
        b = jnp.dot(x, w3_ref[...], preferred_element_type=F32)
        return jnp.dot((_silu(a) * b).astype(BF16), w2_ref[...], preferred_element_type=F32)

    @pl.when(valid & (f == 0))
    def _():
        x = xs_ref[...].astype(BF16)
        xb_s[...] = x
        acc_s[...] = chunk(x)

    @pl.when(valid & (f == N_F_MOE - 1))
    def _():
        y_ref[...] = acc_s[...] + chunk(xb_s[...])

    @pl.when(jnp.logical_not(valid) & (f == N_F_MOE - 1))
    def _():
        y_ref[...] = jnp.zeros(y_ref.shape, F32)


def _moe(tile_expert, tile_valid, xs, w1, w3, w2):
    def fidx(i, f, te, tv):
        return jnp.where(tv[i] == 1, f, N_F_MOE - 1)

    grid_spec = pltpu.PrefetchScalarGridSpec(
        num_scalar_prefetch=2,
        grid=(N_MOE_TILES, N_F_MOE),
        in_specs=[
            pl.BlockSpec((T_MOE, D), lambda i, f, te, tv: (i, 0)),
            pl.BlockSpec((None, D, F_MOE), lambda i, f, te, tv: (te[i], 0, fidx(i, f, te, tv))),
            pl.BlockSpec((None, D, F_MOE), lambda i, f, te, tv: (te[i], 0, fidx(i, f, te, tv))),
            pl.BlockSpec((None, F_MOE, D), lambda i, f, te, tv: (te[i], fidx(i, f, te, tv), 0)),
        ],
        out_specs=pl.BlockSpec((T_MOE, D), lambda i, f, te, tv: (i, 0)),
        scratch_shapes=[pltpu.VMEM((T_MOE, D), BF16), pltpu.VMEM((T_MOE, D), F32)],
    )
    return pl.pallas_call(
        _moe_kernel,
        out_shape=jax.ShapeDtypeStruct((N_ROWS, D), F32),
        grid_spec=grid_spec,
        compiler_params=_params(("arbitrary", "arbitrary")),
        name="moe_experts",
    )(tile_expert, tile_valid, xs, w1, w3, w2)


def _combine_kernel(tab_ref, y_ref, x_ref, route_ref, mod_ref, fg_ref, o_ref, buf, sem, *, first_tile):
    i = pl.program_id(0)
    slot = i % 2

    def fetch(tile, s, op):
        def seg(src, dst, rows):
            return pltpu.make_async_copy(y_ref.at[pl.ds(dst, rows), :], buf.at[s, pl.ds(src, rows), :], sem.at[s])

        _segment_copies(lambda j: tab_ref[tile, j], seg, op)

    @pl.when(i == 0)
    def _():
        buf[...] = jnp.zeros(buf.shape, F32)
        fetch(first_tile, 0, _start)

    @pl.when(i + 1 < pl.num_programs(0))
    def _():
        fetch(first_tile + i + 1, 1 - slot, _start)

    fetch(first_tile + i, slot, _wait)
    yb = buf[slot].astype(BF16)
    route = route_ref[...]
    col = lax.broadcasted_iota(I32, (TM, L_TILE), 1).astype(F32)
    picks = [jnp.dot(jnp.where(col == route[:, c:c + 1], 1.0, 0.0).astype(BF16), yb, preferred_element_type=F32)
             for c in (4, 5)]
    moe = route[:, 2:3] * picks[0] + route[:, 3:4] * picks[1]
    x = x_ref[...] + mod_ref[5:6, :] * moe
    ms = jnp.mean(x * x, axis=-1, keepdims=True)
    o_ref[...] = x * lax.rsqrt(ms + EPS) * fg_ref[...]


def _combine(tab, y, x, route, mod, final_g, n_tok, off_tiles):
    tok = lambda w: pl.BlockSpec((TM, w), lambda i: (i + off_tiles, 0))
    return pl.pallas_call(
        functools.partial(_combine_kernel, first_tile=off_tiles),
        out_shape=jax.ShapeDtypeStruct((n_tok, D), F32),
        grid=(n_tok // TM,),
        in_specs=[
            pl.BlockSpec(memory_space=pltpu.SMEM),
            pl.BlockSpec(memory_space=pl.ANY),
            tok(D), tok(LANES), _mod_spec(1, TM, off_tiles), _resident((1, D)),
        ],
        out_specs=pl.BlockSpec((TM, D), lambda i: (i, 0)),
        scratch_shapes=[pltpu.VMEM((2, L_TILE, D), F32), pltpu.SemaphoreType.DMA((2,))],
        compiler_params=_params(("arbitrary",)),
        name="moe_combine",
    )(tab, y, x, route, mod, final_g)


def kernel(x_prompt, x_sample, cache_k, cache_v, c, c_ctx, ada_w, ada_b, norm1_g, norm2_g, final_g, mix_w_in, sgu_g, sgu_w, sgu_b, mix_w_out, ffn_w1, ffn_w3, ffn_w2, attn_w_qkv, lam_q1, lam_k1, lam_q2, lam_k2, subln_g, attn_w_o, router_w, moe_w1, moe_w3, moe_w2):
    tb = _device_tables()
    xc = x_prompt.reshape(NP, D)
    xl = x_sample.reshape(NS, D)
    cvec = jnp.concatenate([c, c_ctx[None, :], jnp.zeros((3, D), F32)], axis=0)
    mod = _adaln(cvec, ada_w, ada_b)

    bias = jnp.repeat(sgu_b[0].T, GW, axis=1)
    a, y, y1r, y2r = _inproj(xc, xl, mod, norm1_g[0][None, :], mix_w_in[0].astype(BF16), sgu_g[0].reshape(1, SGU_W),
                             sgu_w[0].astype(BF16), bias, tb)
    w_out = mix_w_out[0].astype(BF16)
    x1c = _mixout_ctx(y, a, xc, mod, w_out, tb)
    br, bi = _dft1(y1r.reshape(N_LAT_SEQ, DFT_N1, DFT_N2 * FNET_W), y2r.reshape(N_LAT_SEQ, DFT_N1, DFT_N2 * FNET_W), tb)
    x1l = _mixout_lat(br, bi, a.reshape(T // LAT_LEN, DFT_N2, DFT_N1, SGU_W),
                      xl.reshape(N_LAT_SEQ, DFT_N2, DFT_N1, D), mod, w_out).reshape(NS, D)
    x2 = _ffn(x1c, x1l, mod, norm2_g[0][None, :], ffn_w1[0].astype(BF16), ffn_w3[0].astype(BF16), ffn_w2[0].astype(BF16))

    qkv_c, qkv_l, kf, vf = _qkv(x2, mod, norm1_g[1][None, :], attn_w_qkv[0].astype(BF16), tb)
    lam_p = jnp.stack([lam_q1[0], lam_k1[0], lam_q2[0], lam_k2[0]], axis=0)
    sg = subln_g[0][None, :]
    oc = _attn_ctx(*qkv_c, lam_p, sg)
    cache_rows = (N_LAT_SEQ, CTX_LEN * N_HEADS, LANES)
    ol = _attn_lat(*qkv_l, cache_k.reshape(cache_rows), cache_v.reshape(cache_rows), lam_p, sg)

    w_r = jnp.pad(router_w[0], ((0, 0), (0, LANES - N_EXPERTS))).astype(BF16)
    x3, h2, route, cnt = _oproj_route(oc, ol, x2, mod, attn_w_o[0].astype(BF16), norm2_g[1][None, :], w_r, tb)
    unit_per_tile = T_MOE // SEG_ALIGN
    seg_len = cnt[:, 0, :N_EXPERTS].astype(I32)
    seg_local = jnp.cumsum(seg_len, axis=1) - seg_len
    group_len = jnp.sum(seg_len, axis=0)
    group_tiles = (group_len + unit_per_tile - 1) // unit_per_tile
    tile_ends = jnp.cumsum(group_tiles)
    group_start = (tile_ends - group_tiles) * unit_per_tile
    seg_global = group_start[None, :] + jnp.cumsum(seg_len, axis=0) - seg_len
    tab = jnp.concatenate([seg_len, seg_local, seg_global], axis=1)
    tab = jnp.pad(tab, ((0, 0), (0, LANES - 3 * N_EXPERTS)))
    fill = jnp.concatenate([group_start + group_len, group_tiles * unit_per_tile - group_len,
                            tile_ends[-1:] * unit_per_tile, N_MOE_TILES - tile_ends[-1:]])
    fill = jnp.pad(fill, (0, LANES - 2 * N_EXPERTS - 2)).reshape(1, LANES)
    tile_idx = jnp.arange(N_MOE_TILES, dtype=I32)
    tile_valid = (tile_idx < tile_ends[-1]).astype(I32)
    tile_expert = jnp.sum((jnp.minimum(tile_idx, tile_ends[-1] - 1)[:, None] >= tile_ends[None, :]).astype(I32), axis=-1)

    xs = _dispatch(tab, fill, h2, route)
    ys = _moe(tile_expert, tile_valid, xs, moe_w1[0].astype(BF16), moe_w3[0].astype(BF16), moe_w2[0].astype(BF16))
    fg = final_g[None, :]
    y_prompt = _combine(tab, ys, x3, route, mod, fg, NP, 0)
    y_sample = _combine(tab, ys, x3, route, mod, fg, NS, NP // TM)

    return (y_prompt.reshape(N_CTX_SEQ, CTX_LEN, D), y_sample.reshape(N_LAT_SEQ, LAT_LEN, D),
            kf.reshape(N_CTX_SEQ, 1, CTX_LEN, N_HEADS, 2 * HEAD_DIM), vf.reshape(N_CTX_SEQ, 1, CTX_LEN, N_HEADS, V_DIM))
```

```python
import functools
import math

import numpy as np
import jax
import jax.numpy as jnp
from jax import lax
from jax.experimental import pallas as pl
from jax.experimental.pallas import tpu as pltpu

F32 = jnp.float32
BF16 = jnp.bfloat16
I32 = jnp.int32

D = 1024
N_CTX_SEQ = 32
CTX_LEN = 256
N_LAT_SEQ = 4
LAT_LEN = 4096
NP = N_CTX_SEQ * CTX_LEN
NS = N_LAT_SEQ * LAT_LEN
T = NP + NS
GRID_W = 64
CHUNK = 128
SGU_W = 512
FNET_W = 512
GW = 128
N_HEADS = 8
HEAD_DIM = 64
V_DIM = 128
ROPE_THETA = 10000.0
D_FF = 2816
N_EXPERTS = 8
D_FF_EXPERT = 3584
EPS = 1e-6
LAM0 = 0.8 - 0.6 * math.exp(-0.3 * 1)
CTX_ROW = 4

LANES = 128
TM = 512
TQ = 1024
TK = 2048
Q_SCALE = HEAD_DIM ** -0.5 * math.log2(math.e)
T_MOE = 512
F_MOE = 1792
N_F_MOE = D_FF_EXPERT // F_MOE
assert N_F_MOE == 2
CAST_SPLIT = 4
CAST_COLS = D_FF_EXPERT // CAST_SPLIT
CAST_BLOCKS = N_EXPERTS * CAST_SPLIT
SEG_ALIGN = 8
SEG_BITS = (64, 32, 16, 8, 4, 2, 1)
L_TILE = 1152
N_ROWS = -(-(2 * T + (T // TM) * N_EXPERTS * (SEG_ALIGN - 1) + N_EXPERTS * (T_MOE - SEG_ALIGN)) // T_MOE) * T_MOE
N_MOE_TILES = N_ROWS // T_MOE
MAX_UNUSED_TILES = N_MOE_TILES - 2 * T // T_MOE
FFN_CHUNK = 2816
DFT_N1 = 256
DFT_N2 = 16
DFT_ROWS = 32
DFT1_GROUP = 4
CTX_PER_STEP = 2
ROUTE_SUB = 2
VMEM_LIMIT = 56 * 2 ** 20


def _params(sem, vmem=VMEM_LIMIT):
    return pltpu.CompilerParams(dimension_semantics=sem, vmem_limit_bytes=vmem)


def _resident(shape):
    nd = len(shape)
    return pl.BlockSpec(shape, lambda *_: (0,) * nd, pipeline_mode=pl.Buffered(1))


def _mod_row(i, tm):
    npt = NP // tm
    return jnp.where(i < npt, CTX_ROW, (i - npt) // (LAT_LEN // tm))


def _mod_spec(layer, tm, offset_tiles=0):
    return pl.BlockSpec((None, None, 6, D), lambda i, *_: (layer, _mod_row(i + offset_tiles, tm), 0, 0))


def _two_stream_specs(tm, width):
    npt = NP // tm
    ctx = pl.BlockSpec((tm, width), lambda i: (jnp.minimum(i, npt - 1), 0))
    lat = pl.BlockSpec((tm, width), lambda i: (jnp.maximum(i - npt, 0), 0))
    return ctx, lat


def _pick_stream(tm, ctx_ref, lat_ref):
    return jnp.where(pl.program_id(0) < NP // tm, ctx_ref[...], lat_ref[...])


def _modulate(x, g, scale, shift):
    ms = jnp.mean(x * x, axis=-1, keepdims=True)
    return x * lax.rsqrt(ms + EPS) * (g * (1.0 + scale)) + shift


def _silu(a):
    return a * jax.nn.sigmoid(a)


def _gelu_tanh(x):
    return 0.5 * x * (1.0 + jnp.tanh(0.7978845608028654 * (x + 0.044715 * (x * x * x))))


@functools.lru_cache(maxsize=None)
def _tables():
    def cs(n):
        k = np.arange(n, dtype=np.int64)
        ang = 2.0 * np.pi * ((k[:, None] * k[None, :]) % n) / n
        return np.cos(ang), np.sin(ang)

    c128, s128 = cs(GW)
    dft_ch = np.concatenate([c128, s128], axis=1)
    c256, s256 = cs(CTX_LEN)
    dft_ctx = np.concatenate([c256, -s256], axis=1)
    k1 = np.arange(DFT_N1, dtype=np.int64)[:, None]
    n2 = np.arange(DFT_N2, dtype=np.int64)[None, :]
    tw = 2.0 * np.pi * ((k1 * n2) % LAT_LEN) / LAT_LEN
    half = HEAD_DIM // 2
    inv_freq = ROPE_THETA ** (-np.arange(0, half, 2, dtype=np.float64) / half)
    t = np.arange(LAT_LEN)
    ang_r = (t // GRID_W).astype(np.float64)[:, None] * inv_freq[None, :]
    ang_c = (t % GRID_W).astype(np.float64)[:, None] * inv_freq[None, :]

    def blk(ang):
        c = np.cos(ang)
        s = np.sin(ang)
        return np.concatenate([c, c], axis=1), np.concatenate([-s, s], axis=1)

    cr, sr = blk(ang_r)
    cc, sc = blk(ang_c)
    cos64 = np.concatenate([cr, cc], axis=1)
    sin64 = np.concatenate([sr, sc], axis=1)
    rope_cos = np.concatenate([cos64, cos64], axis=1).astype(np.float32)
    rope_sin = np.concatenate([sin64, sin64], axis=1).astype(np.float32)
    tri = np.tril(np.ones((TM, TM), np.float32), k=-1)
    below = np.triu(np.ones((LANES, LANES), np.float32), k=1)
    f32 = lambda a: np.asarray(a, np.float32)
    return dict(dft_ch=f32(dft_ch), dft_ctx=f32(dft_ctx), c256=f32(c256), s256=f32(s256),
                tw_cos=f32(np.cos(tw)), tw_sin=f32(np.sin(tw)),
                rope_cos=rope_cos, rope_sin=rope_sin, tri=tri, below=below)


def _device_tables():
    tb = {k: jnp.asarray(v) for k, v in _tables().items()}
    for k in ("dft_ch", "dft_ctx", "c256", "s256", "tri", "below"):
        tb[k] = tb[k].astype(BF16)
    return tb


def _adaln_kernel(c_ref, w_ref, b_ref, o_ref):
    s = _silu(c_ref[...]).astype(BF16)
    o_ref[...] = jnp.dot(s, w_ref[...].astype(BF16), preferred_element_type=F32) + b_ref[...]


def _adaln(cvec, ada_w, ada_b):
    tn = 1536
    out = pl.pallas_call(
        _adaln_kernel,
        out_shape=jax.ShapeDtypeStruct((2, 8, 6 * D), F32),
        grid=(2, 6 * D // tn),
        in_specs=[
            pl.BlockSpec((8, D), lambda l, j: (0, 0)),
            pl.BlockSpec((None, D, tn), lambda l, j: (l, 0, j)),
            pl.BlockSpec((None, 1, tn), lambda l, j: (l, 0, j)),
        ],
        out_specs=pl.BlockSpec((None, 8, tn), lambda l, j: (l, 0, j)),
        compiler_params=_params(("arbitrary", "arbitrary")),
        name="adaln",
    )(cvec, ada_w, ada_b.reshape(2, 1, 6 * D))
    return out.reshape(2, 8, 6, D)


def _inproj_kernel(xc_ref, xl_ref, mod_ref, g_ref, win_ref, sgug_ref, sguw_ref, sgub_ref, dft_ref, a_ref, y_ref,
                   y1r_ref, y2r_ref, slab_s):
    is_ctx = pl.program_id(0) < NP // TM
    x = _pick_stream(TM, xc_ref, xl_ref)
    h = _modulate(x, g_ref[...], mod_ref[1:2, :], mod_ref[0:1, :]).astype(BF16)
    p = jnp.dot(h, win_ref[...], preferred_element_type=F32)
    act = _gelu_tanh(p[:, :2 * SGU_W])
    spectra = []
    for g in range(4):
        lo, hi = g * GW, (g + 1) * GW
        u = act[:, lo:hi]
        v = act[:, SGU_W + lo:SGU_W + hi]
        ms = jnp.mean(v * v, axis=-1, keepdims=True)
        vn = (v * lax.rsqrt(ms + EPS) * sgug_ref[:, lo:hi]).astype(BF16)
        w = sguw_ref[g]
        for c in range(TM // CHUNK):
            r0, r1 = c * CHUNK, (c + 1) * CHUNK
            mix = jnp.dot(w, vn[r0:r1, :], preferred_element_type=F32) + sgub_ref[:, lo:hi]
            a_ref[r0:r1, lo:hi] = (u[r0:r1, :] * mix).astype(BF16)
        fg = p[:, 2 * SGU_W + lo:2 * SGU_W + hi].astype(BF16)
        spectra.append(jnp.dot(fg, dft_ref[...], preferred_element_type=F32))

    @pl.when(is_ctx)
    def _():
        for g, yy in enumerate(spectra):
            y_ref[:, g * GW:(g + 1) * GW] = yy[:, :GW].astype(BF16)
            y_ref[:, FNET_W + g * GW:FNET_W + (g + 1) * GW] = yy[:, GW:].astype(BF16)

    @pl.when(jnp.logical_not(is_ctx))
    def _():
        for g, yy in enumerate(spectra):
            for half, dst in ((0, y1r_ref), (1, y2r_ref)):
                slab = slab_s.at[2 * g + half]
                slab[...] = yy[:, half * GW:(half + 1) * GW]
                for n2 in range(DFT_N2):
                    cols = slice(n2 * FNET_W + g * GW, n2 * FNET_W + (g + 1) * GW)
                    dst[:, cols] = slab[pl.ds(n2, TM // DFT_N2, stride=DFT_N2), :].astype(BF16)


def _inproj(x_ctx, x_lat, mod, g, w_in, sgu_g, sgu_w, sgu_b, tb):
    ctx, lat = _two_stream_specs(TM, D)
    npt = NP // TM
    rows_r = TM // DFT_N2
    yr_shape = jax.ShapeDtypeStruct((NS // DFT_N2, DFT_N2 * FNET_W), BF16)
    yr_spec = pl.BlockSpec((rows_r, DFT_N2 * FNET_W), lambda i: (jnp.maximum(i - npt, 0), 0))
    return pl.pallas_call(
        _inproj_kernel,
        out_shape=(jax.ShapeDtypeStruct((T, SGU_W), BF16), jax.ShapeDtypeStruct((NP, 2 * FNET_W), BF16),
                   yr_shape, yr_shape),
        grid=(T // TM,),
        in_specs=[
            ctx, lat,
            _mod_spec(0, TM),
            _resident((1, D)),
            _resident((D, 3 * SGU_W)),
            _resident((1, SGU_W)),
            _resident((4, CHUNK, CHUNK)),
            _resident((CHUNK, SGU_W)),
            _resident((GW, 2 * GW)),
        ],
        out_specs=(pl.BlockSpec((TM, SGU_W), lambda i: (i, 0)),
                   pl.BlockSpec((TM, 2 * FNET_W), lambda i: (jnp.minimum(i, npt - 1), 0)), yr_spec, yr_spec),
        scratch_shapes=[pltpu.VMEM((8, TM, LANES), F32)],
        compiler_params=_params(("arbitrary",)),
        name="inproj_mix",
    )(x_ctx, x_lat, mod, g, w_in, sgu_g, sgu_w, sgu_b, tb["dft_ch"])


def _mix_out(f, a_ref, x_ref, mod_ref, wout_ref, o_ref):
    mix = (jnp.dot(a_ref[...], wout_ref[:SGU_W, :], preferred_element_type=F32)
           + jnp.dot(f.astype(BF16), wout_ref[SGU_W:, :], preferred_element_type=F32))
    o_ref[...] = x_ref[...] + mod_ref[2:3, :] * mix


def _mixout_ctx_kernel(y_ref, a_ref, x_ref, mod_ref, dft_ref, wout_ref, o_ref):
    fs = []
    for s in range(CTX_PER_STEP):
        rows = slice(s * CTX_LEN, (s + 1) * CTX_LEN)
        fs.append(jnp.dot(dft_ref[:, :CTX_LEN], y_ref[rows, :FNET_W], preferred_element_type=F32)
                  + jnp.dot(dft_ref[:, CTX_LEN:], y_ref[rows, FNET_W:], preferred_element_type=F32))
    f = jnp.concatenate(fs, axis=0) * (1.0 / math.sqrt(CTX_LEN * GW))
    _mix_out(f, a_ref, x_ref, mod_ref, wout_ref, o_ref)


def _mixout_ctx(y, a, x, mod, w_out, tb):
    rows = CTX_PER_STEP * CTX_LEN
    tok = lambda w: pl.BlockSpec((rows, w), lambda i: (i, 0))
    return pl.pallas_call(
        _mixout_ctx_kernel,
        out_shape=jax.ShapeDtypeStruct((NP, D), F32),
        grid=(N_CTX_SEQ // CTX_PER_STEP,),
        in_specs=[tok(2 * FNET_W), tok(SGU_W), tok(D),
                  pl.BlockSpec((None, None, 6, D), lambda i: (0, CTX_ROW, 0, 0)),
                  _resident((CTX_LEN, 2 * CTX_LEN)), _resident((D, D))],
        out_specs=tok(D),
        compiler_params=_params(("arbitrary",)),
        name="mixout_ctx",
    )(y, a, x, mod, tb["dft_ctx"], w_out)


def _dft1_kernel(y1_ref, y2_ref, c_ref, s_ref, twc_ref, tws_ref, br_ref, bi_ref):
    y1 = y1_ref[...]
    y2 = y2_ref[...]
    c = c_ref[...]
    s = s_ref[...]
    ar = jnp.dot(c, y1, preferred_element_type=F32) - jnp.dot(s, y2, preferred_element_type=F32)
    ai = -(jnp.dot(c, y2, preferred_element_type=F32) + jnp.dot(s, y1, preferred_element_type=F32))
    lane = lax.broadcasted_iota(I32, (DFT_N1, DFT_N2), 1)
    for r in range(DFT1_GROUP):
        n2 = pl.program_id(1) * DFT1_GROUP + r
        cols = slice(r * FNET_W, (r + 1) * FNET_W)
        tc = jnp.sum(jnp.where(lane == n2, twc_ref[...], 0.0), axis=-1, keepdims=True)
        ts = jnp.sum(jnp.where(lane == n2, tws_ref[...], 0.0), axis=-1, keepdims=True)
        br_ref[r] = (tc * ar[:, cols] + ts * ai[:, cols]).astype(BF16)
        bi_ref[r] = (tc * ai[:, cols] - ts * ar[:, cols]).astype(BF16)


def _dft1(y1r, y2r, tb):
    shp = jax.ShapeDtypeStruct((N_LAT_SEQ, DFT_N2, DFT_N1, FNET_W), BF16)
    src = pl.BlockSpec((None, DFT_N1, DFT1_GROUP * FNET_W), lambda b, n: (b, 0, n))
    dst = pl.BlockSpec((None, DFT1_GROUP, DFT_N1, FNET_W), lambda b, n: (b, n, 0, 0))
    return pl.pallas_call(
        _dft1_kernel,
        out_shape=(shp, shp),
        grid=(N_LAT_SEQ, DFT_N2 // DFT1_GROUP),
        in_specs=[src, src, _resident((DFT_N1, DFT_N1)), _resident((DFT_N1, DFT_N1)),
                  _resident((DFT_N1, DFT_N2)), _resident((DFT_N1, DFT_N2))],
        out_specs=(dst, dst),
        compiler_params=_params(("arbitrary", "arbitrary")),
        name="dft_stage1",
    )(y1r, y2r, tb["c256"], tb["s256"], tb["tw_cos"], tb["tw_sin"])


def _cmul_const(z, wr, wi):
    re, im = z
    tol = 1e-12
    if abs(wi) < tol:
        return (re, im) if wr > 0 else (-re, -im)
    if abs(wr) < tol:
        return (-im, re) if wi > 0 else (im, -re)
    return (re * wr - im * wi, re * wi + im * wr)


def _fft(zs):
    n = len(zs)
    if n == 1:
        return zs
    even, odd = _fft(zs[0::2]), _fft(zs[1::2])
    out = [None] * n
    for k in range(n // 2):
        ang = 2.0 * math.pi * k / n
        t = _cmul_const(odd[k], math.cos(ang), -math.sin(ang))
        out[k] = (even[k][0] + t[0], even[k][1] + t[1])
        out[k + n // 2] = (even[k][0] - t[0], even[k][1] - t[1])
    return out


def _mixout_lat_kernel(br_ref, bi_ref, a_ref, x_ref, mod_ref, wout_ref, o_ref):
    spec = _fft([(br_ref[n2].astype(F32), bi_ref[n2].astype(F32)) for n2 in range(DFT_N2)])
    f = jnp.concatenate([re for re, _ in spec], axis=0) * (1.0 / math.sqrt(LAT_LEN * GW))
    rows = DFT_N2 * DFT_ROWS
    mix = (jnp.dot(a_ref[...].reshape(rows, SGU_W), wout_ref[:SGU_W, :], preferred_element_type=F32)
           + jnp.dot(f.astype(BF16), wout_ref[SGU_W:, :], preferred_element_type=F32))
    out = x_ref[...].reshape(rows, D) + mod_ref[2:3, :] * mix
    o_ref[...] = out.reshape(DFT_N2, DFT_ROWS, D)


def _mixout_lat(br, bi, a_lat, x, mod, w_out):
    blk = lambda w, off=0: pl.BlockSpec((None, DFT_N2, DFT_ROWS, w), lambda b, r: (b + off, 0, r, 0))
    return pl.pallas_call(
        _mixout_lat_kernel,
        out_shape=jax.ShapeDtypeStruct((N_LAT_SEQ, DFT_N2, DFT_N1, D), F32),
        grid=(N_LAT_SEQ, DFT_N1 // DFT_ROWS),
        in_specs=[blk(FNET_W), blk(FNET_W), blk(SGU_W, NP // LAT_LEN), blk(D),
                  pl.BlockSpec((None, None, 6, D), lambda b, r: (0, b, 0, 0)),
                  _resident((D, D))],
        out_specs=blk(D),
        compiler_params=_params(("arbitrary", "arbitrary")),
        name="mixout_lat",
    )(br, bi, a_lat, x, mod, w_out)


def _ffn_kernel(xc_ref, xl_ref, mod_ref, g_ref, w1_ref, w3_ref, w2_ref, o_ref):
    x = _pick_stream(TM, xc_ref, xl_ref)
    h = _modulate(x, g_ref[...], mod_ref[4:5, :], mod_ref[3:4, :]).astype(BF16)
    acc = None
    for c in range(D_FF // FFN_CHUNK):
        lo, hi = c * FFN_CHUNK, (c + 1) * FFN_CHUNK
        a = jnp.dot(h, w1_ref[:, lo:hi], preferred_element_type=F32)
        b = jnp.dot(h, w3_ref[:, lo:hi], preferred_element_type=F32)
        d = jnp.dot((_silu(a) * b).astype(BF16), w2_ref[lo:hi, :], preferred_element_type=F32)
        acc = d if acc is None else acc + d
    o_ref[...] = x + mod_ref[5:6, :] * acc


def _ffn(x_ctx, x_lat, mod, g, w1, w3, w2):
    ctx, lat = _two_stream_specs(TM, D)
    return pl.pallas_call(
        _ffn_kernel,
        out_shape=jax.ShapeDtypeStruct((T, D), F32),
        grid=(T // TM,),
        in_specs=[
            ctx, lat,
            _mod_spec(0, TM),
            _resident((1, D)),
            _resident((D, D_FF)),
            _resident((D, D_FF)),
            _resident((D_FF, D)),
        ],
        out_specs=pl.BlockSpec((TM, D), lambda i: (i, 0)),
        compiler_params=_params(("arbitrary",)),
        name="ffn",
    )(x_ctx, x_lat, mod, g, w1, w3, w2)


def _qkv_ctx_kernel(x_ref, mod_ref, g_ref, w_ref, q_ref, k_ref, v_ref, kf_ref, vf_ref):
    h = _modulate(x_ref[...], g_ref[...], mod_ref[1:2, :], mod_ref[0:1, :]).astype(BF16)
    qkv = jnp.dot(h, w_ref[...], preferred_element_type=F32)
    k = qkv[:, D:2 * D]
    v = qkv[:, 2 * D:]
    q_ref[...] = (qkv[:, :D] * Q_SCALE).astype(BF16)
    k_ref[...] = k.astype(BF16)
    v_ref[...] = v.astype(BF16)
    kf_ref[...] = k
    vf_ref[...] = v


def _qkv_lat_kernel(x_ref, mod_ref, g_ref, w_ref, cos_ref, sin_ref, q_ref, k_ref, v_ref):
    h = _modulate(x_ref[...], g_ref[...], mod_ref[1:2, :], mod_ref[0:1, :]).astype(BF16)
    qkv = jnp.dot(h, w_ref[...], preferred_element_type=F32)
    cos = cos_ref[...]
    sin = sin_ref[...]
    first = (lax.broadcasted_iota(I32, (TM, LANES), 1) & 31) < 16

    def rotate(xh):
        partner = jnp.where(first, pltpu.roll(xh, LANES - 16, 1), pltpu.roll(xh, 16, 1))
        return xh * cos + partner * sin

    for hh in range(N_HEADS):
        sl = slice(hh * LANES, (hh + 1) * LANES)
        q_ref[:, sl] = (rotate(qkv[:, sl]) * Q_SCALE).astype(BF16)
        k_ref[:, sl] = rotate(qkv[:, D + hh * LANES:D + (hh + 1) * LANES]).astype(BF16)
    v_ref[...] = qkv[:, 2 * D:].astype(BF16)


def _qkv(x, mod, g, w_qkv, tb):
    tok = pl.BlockSpec((TM, D), lambda i: (i, 0))
    bshape = jax.ShapeDtypeStruct((NP, D), BF16)
    lshape = jax.ShapeDtypeStruct((NS, D), BF16)
    fshape = jax.ShapeDtypeStruct((NP, D), F32)
    qc, kc, vc, kf, vf = pl.pallas_call(
        _qkv_ctx_kernel,
        out_shape=(bshape, bshape, bshape, fshape, fshape),
        grid=(NP // TM,),
        in_specs=[tok, _mod_spec(1, TM), _resident((1, D)), _resident((D, 3 * D))],
        out_specs=(tok, tok, tok, tok, tok),
        compiler_params=_params(("arbitrary",)),
        name="qkv_ctx",
    )(x, mod, g, w_qkv)
    off = NP // TM
    ltok = pl.BlockSpec((TM, D), lambda i: (i + off, 0))
    rope = pl.BlockSpec((TM, LANES), lambda i: (i % (LAT_LEN // TM), 0))
    ql, kl, vl = pl.pallas_call(
        _qkv_lat_kernel,
        out_shape=(lshape, lshape, lshape),
        grid=(NS // TM,),
        in_specs=[ltok, _mod_spec(1, TM, off), _resident((1, D)), _resident((D, 3 * D)), rope, rope],
        out_specs=(tok, tok, tok),
        compiler_params=_params(("arbitrary",)),
        name="qkv_lat",
    )(x, mod, g, w_qkv, tb["rope_cos"], tb["rope_sin"])
    return (qc, kc, vc), (ql, kl, vl), kf, vf


def _lam(lp_ref):
    lp = lp_ref[...]
    a = jnp.sum(lp[0:1, :] * lp[1:2, :], axis=-1, keepdims=True)
    b = jnp.sum(lp[2:3, :] * lp[3:4, :], axis=-1, keepdims=True)
    return jnp.exp(a) - jnp.exp(b) + LAM0


def _head_norm(o, sg):
    ms = jnp.mean(o * o, axis=-1, keepdims=True)
    return o * lax.rsqrt(ms + EPS) * (sg * (1.0 - LAM0))


_NT = (((1,), (1,)), ((), ()))


def _attn_ctx_kernel(q_ref, k_ref, v_ref, lp_ref, sg_ref, o_ref):
    lam = _lam(lp_ref)
    map0 = lax.broadcasted_iota(I32, (CTX_LEN, LANES), 1) < HEAD_DIM
    init = _softmax_init(CTX_LEN)
    for hh in range(N_HEADS):
        sl = slice(hh * LANES, (hh + 1) * LANES)
        q = q_ref[:, sl]
        zero = jnp.zeros_like(q)
        outs = []
        for qc in (jnp.where(map0, q, zero), jnp.where(map0, zero, q)):
            _, l, acc = _softmax_step(qc, k_ref[:, sl], v_ref[:, sl], init)
            outs.append(acc / jnp.sum(l, axis=-1, keepdims=True))
        o_ref[:, sl] = _head_norm(outs[0] - lam * outs[1], sg_ref[...]).astype(BF16)


def _attn_ctx(q, k, v, lam_p, subln_g):
    seq = pl.BlockSpec((CTX_LEN, D), lambda i: (i, 0))
    return pl.pallas_call(
        _attn_ctx_kernel,
        out_shape=jax.ShapeDtypeStruct((NP, D), BF16),
        grid=(N_CTX_SEQ,),
        in_specs=[seq, seq, seq, _resident((4, HEAD_DIM)), _resident((1, V_DIM))],
        out_specs=seq,
        compiler_params=_params(("arbitrary",)),
        name="attn_ctx",
    )(q, k, v, lam_p, subln_g)


def _softmax_init(rows):
    return (jnp.full((rows, LANES), -jnp.inf, F32), jnp.zeros((rows, LANES), F32), jnp.zeros((rows, LANES), F32))


def _softmax_step(qc, kc, vc, state):
    m, l, acc = state
    s = lax.dot_general(qc, kc, _NT, preferred_element_type=F32)
    blocks = [s[:, j * LANES:(j + 1) * LANES] for j in range(s.shape[1] // LANES)]
    bm = functools.reduce(jnp.maximum, blocks)
    m_new = jnp.maximum(m, jnp.max(bm, axis=-1, keepdims=True))
    alpha = jnp.exp2(m - m_new)
    ps = [jnp.exp2(b - m_new) for b in blocks]
    l_new = alpha * l + functools.reduce(jnp.add, ps)
    p = jnp.concatenate(ps, axis=1).astype(BF16)
    acc_new = alpha * acc + jnp.dot(p, vc, preferred_element_type=F32)
    return m_new, l_new, acc_new


def _attn_lat_kernel(q_ref, k_ref, v_ref, ck_ref, cv_ref, lp_ref, sg_ref, w1_ref, w3_ref, w2_ref,
                     o_ref, w1b_ref, w3b_ref, w2b_ref):
    step = (pl.program_id(0) * N_HEADS + pl.program_id(1)) * (LAT_LEN // TQ) + pl.program_id(2)
    for phase, (src, dst) in enumerate(((w1_ref, w1b_ref), (w3_ref, w3b_ref), (w2_ref, w2b_ref))):
        @pl.when((step >= phase * CAST_BLOCKS) & (step < (phase + 1) * CAST_BLOCKS))
        def _(src=src, dst=dst):
            dst[...] = src[...].astype(BF16)

    q = q_ref[...]
    zero = jnp.zeros_like(q)
    map0 = lax.broadcasted_iota(I32, (TQ, LANES), 1) < HEAD_DIM
    qs = (jnp.where(map0, q, zero), jnp.where(map0, zero, q))
    init = _softmax_init(TQ)
    states = [init, init]
    head_rows = pl.ds(pl.program_id(1), CTX_LEN, stride=N_HEADS)
    chunks = [(ck_ref[head_rows, :].astype(BF16), cv_ref[head_rows, :].astype(BF16))]
    chunks += [(k_ref[j * TK:(j + 1) * TK, :], v_ref[j * TK:(j + 1) * TK, :]) for j in range(LAT_LEN // TK)]
    for kc, vc in chunks:
        states = [_softmax_step(qs[c], kc, vc, states[c]) for c in range(2)]
    outs = [acc / jnp.sum(l, axis=-1, keepdims=True) for _, l, acc in states]
    o = outs[0] - _lam(lp_ref) * outs[1]
    o_ref[...] = _head_norm(o, sg_ref[...]).astype(BF16)


def _attn_lat(q, k, v, cache_k, cache_v, lam_p, subln_g, moe_w1, moe_w3, moe_w2):
    nq = LAT_LEN // TQ
    assert N_LAT_SEQ * N_HEADS * nq >= 3 * CAST_BLOCKS
    qspec = pl.BlockSpec((TQ, LANES), lambda b, h, i: (b * nq + i, h))
    kspec = pl.BlockSpec((LAT_LEN, LANES), lambda b, h, i: (b, h))
    cspec = pl.BlockSpec((None, CTX_LEN * N_HEADS, LANES), lambda b, h, i: (b, 0, 0))

    def cast_block(phase, b, h, i):
        j = jnp.clip((b * N_HEADS + h) * nq + i - phase * CAST_BLOCKS, 0, CAST_BLOCKS - 1)
        return j // CAST_SPLIT, j % CAST_SPLIT

    def up_spec(phase):
        return pl.BlockSpec((None, D, CAST_COLS), lambda b, h, i: (cast_block(phase, b, h, i)[0], 0,
                                                                    cast_block(phase, b, h, i)[1]))

    down_spec = pl.BlockSpec((None, CAST_COLS, D), lambda b, h, i: (*cast_block(2, b, h, i), 0))
    up_shape = jax.ShapeDtypeStruct((N_EXPERTS, D, D_FF_EXPERT), BF16)
    return pl.pallas_call(
        _attn_lat_kernel,
        out_shape=(jax.ShapeDtypeStruct((NS, D), BF16), up_shape, up_shape,
                   jax.ShapeDtypeStruct((N_EXPERTS, D_FF_EXPERT, D), BF16)),
        grid=(N_LAT_SEQ, N_HEADS, nq),
        in_specs=[qspec, kspec, kspec, cspec, cspec, _resident((4, HEAD_DIM)), _resident((1, V_DIM)),
                  up_spec(0), up_spec(1), down_spec],
        out_specs=(qspec, up_spec(0), up_spec(1), down_spec),
        compiler_params=_params(("arbitrary", "arbitrary", "arbitrary")),
        name="attn_lat",
    )(q, k, v, cache_k, cache_v, lam_p, subln_g, moe_w1, moe_w3, moe_w2)


def _oproj_route_kernel(oc_ref, ol_ref, x_ref, mod_ref, wo_ref, g_ref, wr_ref, tri_ref, below_ref, x3_ref, h_ref,
                        route_ref, cnt_ref):
    is_ctx = pl.program_id(0) < NP // (ROUTE_SUB * TM)
    for sub in range(ROUTE_SUB):
        rows = slice(sub * TM, (sub + 1) * TM)
        o = jnp.where(is_ctx, oc_ref[rows, :], ol_ref[rows, :])
        _route_tile(o, x_ref[rows, :], mod_ref, wo_ref, g_ref, wr_ref, tri_ref, below_ref,
                    x3_ref.at[rows, :], h_ref.at[rows, :], route_ref.at[rows, :], cnt_ref.at[sub])


def _route_tile(o, x, mod_ref, wo_ref, g_ref, wr_ref, tri_ref, below_ref, x3_ref, h_ref, route_ref, cnt_ref):
    x3 = x + mod_ref[2:3, :] * jnp.dot(o, wo_ref[...], preferred_element_type=F32)
    x3_ref[...] = x3
    hb = _modulate(x3, g_ref[...], mod_ref[4:5, :], mod_ref[3:4, :]).astype(BF16)
    h_ref[...] = hb

    logits = jnp.dot(hb, wr_ref[...], preferred_element_type=F32)
    lane = lax.broadcasted_iota(I32, (TM, LANES), 1).astype(F32)
    neg = jnp.float32(-jnp.inf)
    lg = jnp.where(lane < N_EXPERTS, logits, neg)
    v1 = jnp.max(lg, axis=-1, keepdims=True)
    i1 = jnp.min(jnp.where(lg == v1, lane, float(LANES)), axis=-1, keepdims=True)
    lg2 = jnp.where(lane == i1, neg, lg)
    v2 = jnp.max(lg2, axis=-1, keepdims=True)
    i2 = jnp.min(jnp.where(lg2 == v2, lane, float(LANES)), axis=-1, keepdims=True)
    e = jnp.exp(v2 - v1)
    g1 = 1.0 / (1.0 + e)
    g2 = e / (1.0 + e)
    sel1 = lane == i1
    sel2 = lane == i2
    onehot = jnp.where(sel1 | sel2, 1.0, 0.0)
    rank = jnp.dot(tri_ref[...], onehot.astype(BF16), preferred_element_type=F32)
    cnt8 = jnp.floor((jnp.sum(onehot, axis=0, keepdims=True) + (SEG_ALIGN - 1)) * (1.0 / SEG_ALIGN))
    base8 = jnp.dot(jnp.broadcast_to(cnt8, (8, LANES)).astype(BF16), below_ref[...], preferred_element_type=F32)
    slot = base8[0:1, :] * float(SEG_ALIGN) + rank
    pos1 = jnp.sum(jnp.where(sel1, slot, 0.0), axis=-1, keepdims=True)
    pos2 = jnp.sum(jnp.where(sel2, slot, 0.0), axis=-1, keepdims=True)
    cnt_ref[...] = cnt8
    route = jnp.where(lane == 0, i1, 0.0)
    route = jnp.where(lane == 1, i2, route)
    route = jnp.where(lane == 2, g1, route)
    route = jnp.where(lane == 3, g2, route)
    route = jnp.where(lane == 4, pos1, route)
    route = jnp.where(lane == 5, pos2, route)
    route_ref[...] = route


def _oproj_route(o_ctx, o_lat, x, mod, w_o, g, w_r, tb):
    rows = ROUTE_SUB * TM
    tok = pl.BlockSpec((rows, D), lambda i: (i, 0))
    ctx, lat = _two_stream_specs(rows, D)
    return pl.pallas_call(
        _oproj_route_kernel,
        out_shape=(jax.ShapeDtypeStruct((T, D), F32), jax.ShapeDtypeStruct((T, D), BF16),
                   jax.ShapeDtypeStruct((T, LANES), F32), jax.ShapeDtypeStruct((T // TM, 1, LANES), F32)),
        grid=(T // rows,),
        in_specs=[ctx, lat, tok, _mod_spec(1, rows), _resident((D, D)), _resident((1, D)), _resident((D, LANES)),
                  _resident((TM, TM)), _resident((LANES, LANES))],
        out_specs=(tok, tok, pl.BlockSpec((rows, LANES), lambda i: (i, 0)),
                   pl.BlockSpec((ROUTE_SUB, 1, LANES), lambda i: (i, 0, 0))),
        compiler_params=_params(("arbitrary",)),
        name="oproj_route",
    )(o_ctx, o_lat, x, mod, w_o, g, w_r, tb["tri"], tb["below"])


def _bit_copies(n, src0, dst0, bits, make_copy, op):
    for bit in bits:
        done = n & (-2 * bit)

        @pl.when((n & bit) != 0)
        def _(done=done, bit=bit):
            src = pl.multiple_of((src0 + done) * SEG_ALIGN, SEG_ALIGN)
            dst = pl.multiple_of((dst0 + done) * SEG_ALIGN, SEG_ALIGN)
            op(make_copy(src, dst, bit * SEG_ALIGN))


def _segment_copies(tab, make_copy, op):
    for e in range(N_EXPERTS):
        _bit_copies(tab(e), tab(N_EXPERTS + e), tab(2 * N_EXPERTS + e), SEG_BITS, make_copy, op)


def _start(copy):
    copy.start()


def _wait(copy):
    copy.wait()


def _dispatch_kernel(tab_ref, fill_ref, h_ref, route_ref, xs_ref, sorted_s, zero_s, sem, fill_sem):
    i = pl.program_id(0)
    slot = i % 2
    slots = route_ref[...].T
    row = lax.broadcasted_iota(I32, (L_TILE, TM), 0).astype(F32)
    perm = jnp.where((row == slots[4:5, :]) | (row == slots[5:6, :]), 1.0, 0.0).astype(BF16)
    sorted_s[slot] = jnp.dot(perm, h_ref[...], preferred_element_type=F32)

    def copies(tile, s, op):
        def seg(src, dst, rows):
            return pltpu.make_async_copy(sorted_s.at[s, pl.ds(src, rows), :], xs_ref.at[pl.ds(dst, rows), :], sem.at[s])

        _segment_copies(lambda j: tab_ref[tile, j], seg, op)

    copies(i, slot, _start)

    @pl.when(i > 0)
    def _():
        copies(i - 1, 1 - slot, _wait)

    @pl.when(i == pl.num_programs(0) - 1)
    def _():
        copies(i, slot, _wait)

    @pl.when(i == 0)
    def _():
        zero_s[...] = jnp.zeros(zero_s.shape, F32)

        def fill(src, dst, rows):
            del src
            return pltpu.make_async_copy(zero_s.at[pl.ds(0, rows), :], xs_ref.at[pl.ds(dst, rows), :], fill_sem)

        def fills(op):
            for e in range(N_EXPERTS):
                _bit_copies(fill_ref[0, N_EXPERTS + e], 0, fill_ref[0, e], SEG_BITS[1:], fill, op)
            for k in range(MAX_UNUSED_TILES):
                @pl.when(k < fill_ref[0, 2 * N_EXPERTS + 1])
                def _(k=k):
                    dst = pl.multiple_of(fill_ref[0, 2 * N_EXPERTS] * SEG_ALIGN + k * T_MOE, SEG_ALIGN)
                    op(fill(0, dst, T_MOE))

        fills(_start)
        fills(_wait)


def _dispatch(tab, fill, h, route):
    smem = pl.BlockSpec(memory_space=pltpu.SMEM)
    return pl.pallas_call(
        _dispatch_kernel,
        out_shape=jax.ShapeDtypeStruct((N_ROWS, D), F32),
        grid=(T // TM,),
        in_specs=[smem, smem, pl.BlockSpec((TM, D), lambda i: (i, 0)), pl.BlockSpec((TM, LANES), lambda i: (i, 0))],
        out_specs=pl.BlockSpec(memory_space=pl.ANY),
        scratch_shapes=[pltpu.VMEM((2, L_TILE, D), F32), pltpu.VMEM((T_MOE, D), F32),
                        pltpu.SemaphoreType.DMA((2,)), pltpu.SemaphoreType.DMA(())],
        compiler_params=_params(("arbitrary",)),
        name="moe_dispatch",
    )(tab, fill, h, route)


def _moe_kernel(te_ref, tv_ref, xs_ref, w1_ref, w3_ref, w2_ref, y_ref, xb_s, acc_s):
    del te_ref
    i = pl.program_id(0)
    f = pl.program_id(1)
    valid = tv_ref[i] == 1

    def chunk(x):
        a = jnp.dot(x, w1_ref[...], preferred_element_type=F32)
        b = jnp.dot(x, w3_ref[...], preferred_element_type=F32)
        return jnp.dot((_silu(a) * b).astype(BF16), w2_ref[...], preferred_element_type=F32)

    @pl.when(valid & (f == 0))
    def _():
        x = xs_ref[...].astype(BF16)
        xb_s[...] = x
        acc_s[...] = chunk(x)

    @pl.when(valid & (f == N_F_MOE - 1))
    def _():
        y_ref[...] = acc_s[...] + chunk(xb_s[...])

    @pl.when(jnp.logical_not(valid) & (f == N_F_MOE - 1))
    def _():
        y_ref[...] = jnp.zeros(y_ref.shape, F32)


def _moe(tile_expert, tile_valid, xs, w1, w3, w2):
    def fidx(i, f, te, tv):
        return jnp.where(tv[i] == 1, f, N_F_MOE - 1)

    grid_spec = pltpu.PrefetchScalarGridSpec(
        num_scalar_prefetch=2,
        grid=(N_MOE_TILES, N_F_MOE),
        in_specs=[
            pl.BlockSpec((T_MOE, D), lambda i, f, te, tv: (i, 0)),
            pl.BlockSpec((None, D, F_MOE), lambda i, f, te, tv: (te[i], 0, fidx(i, f, te, tv))),
            pl.BlockSpec((None, D, F_MOE), lambda i, f, te, tv: (te[i], 0, fidx(i, f, te, tv))),
            pl.BlockSpec((None, F_MOE, D), lambda i, f, te, tv: (te[i], fidx(i, f, te, tv), 0)),
        ],
        out_specs=pl.BlockSpec((T_MOE, D), lambda i, f, te, tv: (i, 0)),
        scratch_shapes=[pltpu.VMEM((T_MOE, D), BF16), pltpu.VMEM((T_MOE, D), F32)],
    )
    return pl.pallas_call(
        _moe_kernel,
        out_shape=jax.ShapeDtypeStruct((N_ROWS, D), F32),
        grid_spec=grid_spec,
        compiler_params=_params(("arbitrary", "arbitrary")),
        name="moe_experts",
    )(tile_expert, tile_valid, xs, w1, w3, w2)


def _combine_kernel(tab_ref, y_ref, x_ref, route_ref, mod_ref, fg_ref, o_ref, buf, sem, *, first_tile):
    i = pl.program_id(0)
    slot = i % 2

    def fetch(tile, s, op):
        def seg(src, dst, rows):
            return pltpu.make_async_copy(y_ref.at[pl.ds(dst, rows), :], buf.at[s, pl.ds(src, rows), :], sem.at[s])

        _segment_copies(lambda j: tab_ref[tile, j], seg, op)

    @pl.when(i == 0)
    def _():
        buf[...] = jnp.zeros(buf.shape, F32)
        fetch(first_tile, 0, _start)

    @pl.when(i + 1 < pl.num_programs(0))
    def _():
        fetch(first_tile + i + 1, 1 - slot, _start)

    fetch(first_tile + i, slot, _wait)
    yb = buf[slot].astype(BF16)
    route = route_ref[...]
    col = lax.broadcasted_iota(I32, (TM, L_TILE), 1).astype(F32)
    picks = [jnp.dot(jnp.where(col == route[:, c:c + 1], 1.0, 0.0).astype(BF16), yb, preferred_element_type=F32)
             for c in (4, 5)]
    moe = route[:, 2:3] * picks[0] + route[:, 3:4] * picks[1]
    x = x_ref[...] + mod_ref[5:6, :] * moe
    ms = jnp.mean(x * x, axis=-1, keepdims=True)
    o_ref[...] = x * lax.rsqrt(ms + EPS) * fg_ref[...]


def _combine(tab, y, x, route, mod, final_g, n_tok, off_tiles):
    tok = lambda w: pl.BlockSpec((TM, w), lambda i: (i + off_tiles, 0))
    return pl.pallas_call(
        functools.partial(_combine_kernel, first_tile=off_tiles),
        out_shape=jax.ShapeDtypeStruct((n_tok, D), F32),
        grid=(n_tok // TM,),
        in_specs=[
            pl.BlockSpec(memory_space=pltpu.SMEM),
            pl.BlockSpec(memory_space=pl.ANY),
            tok(D), tok(LANES), _mod_spec(1, TM, off_tiles), _resident((1, D)),
        ],
        out_specs=pl.BlockSpec((TM, D), lambda i: (i, 0)),
        scratch_shapes=[pltpu.VMEM((2, L_TILE, D), F32), pltpu.SemaphoreType.DMA((2,))],
        compiler_params=_params(("arbitrary",)),
        name="moe_combine",
    )(tab, y, x, route, mod, final_g)


def kernel(x_prompt, x_sample, cache_k, cache_v, c, c_ctx, ada_w, ada_b, norm1_g, norm2_g, final_g, mix_w_in, sgu_g, sgu_w, sgu_b, mix_w_out, ffn_w1, ffn_w3, ffn_w2, attn_w_qkv, lam_q1, lam_k1, lam_q2, lam_k2, subln_g, attn_w_o, router_w, moe_w1, moe_w3, moe_w2):
    tb = _device_tables()
    xc = x_prompt.reshape(NP, D)
    xl = x_sample.reshape(NS, D)
    cvec = jnp.concatenate([c, c_ctx[None, :], jnp.zeros((3, D), F32)], axis=0)
    mod = _adaln(cvec, ada_w, ada_b)

    bias = jnp.repeat(sgu_b[0].T, GW, axis=1)
    a, y, y1r, y2r = _inproj(xc, xl, mod, norm1_g[0][None, :], mix_w_in[0].astype(BF16), sgu_g[0].reshape(1, SGU_W),
                             sgu_w[0].astype(BF16), bias, tb)
    w_out = mix_w_out[0].astype(BF16)
    x1c = _mixout_ctx(y, a, xc, mod, w_out, tb)
    br, bi = _dft1(y1r.reshape(N_LAT_SEQ, DFT_N1, DFT_N2 * FNET_W), y2r.reshape(N_LAT_SEQ, DFT_N1, DFT_N2 * FNET_W), tb)
    x1l = _mixout_lat(br, bi, a.reshape(T // LAT_LEN, DFT_N2, DFT_N1, SGU_W),
                      xl.reshape(N_LAT_SEQ, DFT_N2, DFT_N1, D), mod, w_out).reshape(NS, D)
    x2 = _ffn(x1c, x1l, mod, norm2_g[0][None, :], ffn_w1[0].astype(BF16), ffn_w3[0].astype(BF16), ffn_w2[0].astype(BF16))

    qkv_c, qkv_l, kf, vf = _qkv(x2, mod, norm1_g[1][None, :], attn_w_qkv[0].astype(BF16), tb)
    lam_p = jnp.stack([lam_q1[0], lam_k1[0], lam_q2[0], lam_k2[0]], axis=0)
    sg = subln_g[0][None, :]
    oc = _attn_ctx(*qkv_c, lam_p, sg)
    cache_rows = (N_LAT_SEQ, CTX_LEN * N_HEADS, LANES)
    ol, w1b, w3b, w2b = _attn_lat(*qkv_l, cache_k.reshape(cache_rows), cache_v.reshape(cache_rows), lam_p, sg,
                                  moe_w1[0], moe_w3[0], moe_w2[0])

    w_r = jnp.pad(router_w[0], ((0, 0), (0, LANES - N_EXPERTS))).astype(BF16)
    x3, h2, route, cnt = _oproj_route(oc, ol, x2, mod, attn_w_o[0].astype(BF16), norm2_g[1][None, :], w_r, tb)
    unit_per_tile = T_MOE // SEG_ALIGN
    seg_len = cnt[:, 0, :N_EXPERTS].astype(I32)
    seg_local = jnp.cumsum(seg_len, axis=1) - seg_len
    group_len = jnp.sum(seg_len, axis=0)
    group_tiles = (group_len + unit_per_tile - 1) // unit_per_tile
    tile_ends = jnp.cumsum(group_tiles)
    group_start = (tile_ends - group_tiles) * unit_per_tile
    seg_global = group_start[None, :] + jnp.cumsum(seg_len, axis=0) - seg_len
    tab = jnp.concatenate([seg_len, seg_local, seg_global], axis=1)
    tab = jnp.pad(tab, ((0, 0), (0, LANES - 3 * N_EXPERTS)))
    fill = jnp.concatenate([group_start + group_len, group_tiles * unit_per_tile - group_len,
                            tile_ends[-1:] * unit_per_tile, N_MOE_TILES - tile_ends[-1:]])
    fill = jnp.pad(fill, (0, LANES - 2 * N_EXPERTS - 2)).reshape(1, LANES)
    tile_idx = jnp.arange(N_MOE_TILES, dtype=I32)
    tile_valid = (tile_idx < tile_ends[-1]).astype(I32)
    tile_expert = jnp.sum((jnp.minimum(tile_idx, tile_ends[-1] - 1)[:, None] >= tile_ends[None, :]).astype(I32), axis=-1)

    xs = _dispatch(tab, fill, h2, route)
    ys = _moe(tile_expert, tile_valid, xs, w1b, w3b, w2b)
    fg = final_g[None, :]
    y_prompt = _combine(tab, ys, x3, route, mod, fg, NP, 0)
    y_sample = _combine(tab, ys, x3, route, mod, fg, NS, NP // TM)

    return (y_prompt.reshape(N_CTX_SEQ, CTX_LEN, D), y_sample.reshape(N_LAT_SEQ, LAT_LEN, D),
            kf.reshape(N_CTX_SEQ, 1, CTX_LEN, N_HEADS, 2 * HEAD_DIM), vf.reshape(N_CTX_SEQ, 1, CTX_LEN, N_HEADS, V_DIM))
```

```python
import functools
import math

import numpy as np
import jax
import jax.numpy as jnp
from jax import lax
from jax.experimental import pallas as pl
from jax.experimental.pallas import tpu as pltpu

F32 = jnp.float32
BF16 = jnp.bfloat16
I32 = jnp.int32

D = 1024
N_CTX_SEQ = 32
CTX_LEN = 256
N_LAT_SEQ = 4
LAT_LEN = 4096
NP = N_CTX_SEQ * CTX_LEN
NS = N_LAT_SEQ * LAT_LEN
T = NP + NS
GRID_W = 64
CHUNK = 128
SGU_W = 512
FNET_W = 512
GW = 128
N_HEADS = 8
HEAD_DIM = 64
V_DIM = 128
ROPE_THETA = 10000.0
D_FF = 2816
N_EXPERTS = 8
D_FF_EXPERT = 3584
EPS = 1e-6
LAM0 = 0.8 - 0.6 * math.exp(-0.3 * 1)
CTX_ROW = 4

LANES = 128
TM = 512
TQ = 1024
TK = 2048
Q_SCALE = HEAD_DIM ** -0.5 * math.log2(math.e)
T_MOE = 512
F_MOE = 1792
N_F_MOE = D_FF_EXPERT // F_MOE
assert N_F_MOE == 2
CAST_SPLIT = 4
CAST_COLS = D_FF_EXPERT // CAST_SPLIT
CAST_BLOCKS = N_EXPERTS * CAST_SPLIT
FFN_CAST_BLOCKS = 16
SEG_ALIGN = 8
SEG_BITS = (64, 32, 16, 8, 4, 2, 1)
L_TILE = 1152
N_ROWS = -(-(2 * T + (T // TM) * N_EXPERTS * (SEG_ALIGN - 1) + N_EXPERTS * (T_MOE - SEG_ALIGN)) // T_MOE) * T_MOE
N_MOE_TILES = N_ROWS // T_MOE
MAX_UNUSED_TILES = N_MOE_TILES - 2 * T // T_MOE
FFN_CHUNK = 2816
DFT_N1 = 256
DFT_N2 = 16
DFT_ROWS = 32
DFT1_GROUP = 4
CTX_PER_STEP = 2
ROUTE_SUB = 2
VMEM_LIMIT = 56 * 2 ** 20


def _params(sem, vmem=VMEM_LIMIT):
    return pltpu.CompilerParams(dimension_semantics=sem, vmem_limit_bytes=vmem)


def _resident(shape):
    nd = len(shape)
    return pl.BlockSpec(shape, lambda *_: (0,) * nd, pipeline_mode=pl.Buffered(1))


def _mod_row(i, tm):
    npt = NP // tm
    return jnp.where(i < npt, CTX_ROW, (i - npt) // (LAT_LEN // tm))


def _mod_spec(layer, tm, offset_tiles=0):
    return pl.BlockSpec((None, None, 6, D), lambda i, *_: (layer, _mod_row(i + offset_tiles, tm), 0, 0))


def _two_stream_specs(tm, width):
    npt = NP // tm
    ctx = pl.BlockSpec((tm, width), lambda i: (jnp.minimum(i, npt - 1), 0))
    lat = pl.BlockSpec((tm, width), lambda i: (jnp.maximum(i - npt, 0), 0))
    return ctx, lat


def _pick_stream(tm, ctx_ref, lat_ref):
    return jnp.where(pl.program_id(0) < NP // tm, ctx_ref[...], lat_ref[...])


def _modulate(x, g, scale, shift):
    ms = jnp.mean(x * x, axis=-1, keepdims=True)
    return x * lax.rsqrt(ms + EPS) * (g * (1.0 + scale)) + shift


def _silu(a):
    return a * jax.nn.sigmoid(a)


def _gelu_tanh(x):
    return 0.5 * x * (1.0 + jnp.tanh(0.7978845608028654 * (x + 0.044715 * (x * x * x))))


@functools.lru_cache(maxsize=None)
def _tables():
    def cs(n):
        k = np.arange(n, dtype=np.int64)
        ang = 2.0 * np.pi * ((k[:, None] * k[None, :]) % n) / n
        return np.cos(ang), np.sin(ang)

    c128, s128 = cs(GW)
    dft_ch = np.concatenate([c128, s128], axis=1)
    c256, s256 = cs(CTX_LEN)
    dft_ctx = np.concatenate([c256, -s256], axis=1)
    k1 = np.arange(DFT_N1, dtype=np.int64)[:, None]
    n2 = np.arange(DFT_N2, dtype=np.int64)[None, :]
    tw = 2.0 * np.pi * ((k1 * n2) % LAT_LEN) / LAT_LEN
    half = HEAD_DIM // 2
    inv_freq = ROPE_THETA ** (-np.arange(0, half, 2, dtype=np.float64) / half)
    t = np.arange(LAT_LEN)
    ang_r = (t // GRID_W).astype(np.float64)[:, None] * inv_freq[None, :]
    ang_c = (t % GRID_W).astype(np.float64)[:, None] * inv_freq[None, :]

    def blk(ang):
        c = np.cos(ang)
        s = np.sin(ang)
        return np.concatenate([c, c], axis=1), np.concatenate([-s, s], axis=1)

    cr, sr = blk(ang_r)
    cc, sc = blk(ang_c)
    cos64 = np.concatenate([cr, cc], axis=1)
    sin64 = np.concatenate([sr, sc], axis=1)
    rope_cos = np.concatenate([cos64, cos64], axis=1).astype(np.float32)
    rope_sin = np.concatenate([sin64, sin64], axis=1).astype(np.float32)
    tri = np.tril(np.ones((TM, TM), np.float32), k=-1)
    below = np.triu(np.ones((LANES, LANES), np.float32), k=1)
    f32 = lambda a: np.asarray(a, np.float32)
    return dict(dft_ch=f32(dft_ch), dft_ctx=f32(dft_ctx), c256=f32(c256), s256=f32(s256),
                tw_cos=f32(np.cos(tw)), tw_sin=f32(np.sin(tw)),
                rope_cos=rope_cos, rope_sin=rope_sin, tri=tri, below=below)


def _device_tables():
    tb = {k: jnp.asarray(v) for k, v in _tables().items()}
    for k in ("dft_ch", "dft_ctx", "c256", "s256", "tri", "below"):
        tb[k] = tb[k].astype(BF16)
    return tb


def _adaln_kernel(c_ref, w_ref, b_ref, o_ref):
    s = _silu(c_ref[...]).astype(BF16)
    o_ref[...] = jnp.dot(s, w_ref[...].astype(BF16), preferred_element_type=F32) + b_ref[...]


def _adaln(cvec, ada_w, ada_b):
    tn = 1536
    out = pl.pallas_call(
        _adaln_kernel,
        out_shape=jax.ShapeDtypeStruct((2, 8, 6 * D), F32),
        grid=(2, 6 * D // tn),
        in_specs=[
            pl.BlockSpec((8, D), lambda l, j: (0, 0)),
            pl.BlockSpec((None, D, tn), lambda l, j: (l, 0, j)),
            pl.BlockSpec((None, 1, tn), lambda l, j: (l, 0, j)),
        ],
        out_specs=pl.BlockSpec((None, 8, tn), lambda l, j: (l, 0, j)),
        compiler_params=_params(("arbitrary", "arbitrary")),
        name="adaln",
    )(cvec, ada_w, ada_b.reshape(2, 1, 6 * D))
    return out.reshape(2, 8, 6, D)


def _side_cast(step, n_blocks, pairs):
    for phase, (src, dst) in enumerate(pairs):
        @pl.when((step >= phase * n_blocks) & (step < (phase + 1) * n_blocks))
        def _(src=src, dst=dst):
            dst[...] = src[...].astype(BF16)


def _row_block_spec(shape, n_blocks, phase):
    return pl.BlockSpec((shape[0] // n_blocks, shape[1]),
                        lambda i: (jnp.clip(i - phase * n_blocks, 0, n_blocks - 1), 0))


def _inproj_kernel(xc_ref, xl_ref, mod_ref, g_ref, win_ref, sgug_ref, sguw_ref, sgub_ref, dft_ref,
                   w1_ref, w3_ref, w2_ref, a_ref, y_ref, y1r_ref, y2r_ref, w1b_ref, w3b_ref, w2b_ref, slab_s):
    _side_cast(pl.program_id(0), FFN_CAST_BLOCKS, ((w1_ref, w1b_ref), (w3_ref, w3b_ref), (w2_ref, w2b_ref)))
    is_ctx = pl.program_id(0) < NP // TM
    x = _pick_stream(TM, xc_ref, xl_ref)
    h = _modulate(x, g_ref[...], mod_ref[1:2, :], mod_ref[0:1, :]).astype(BF16)
    p = jnp.dot(h, win_ref[...], preferred_element_type=F32)
    act = _gelu_tanh(p[:, :2 * SGU_W])
    spectra = []
    for g in range(4):
        lo, hi = g * GW, (g + 1) * GW
        u = act[:, lo:hi]
        v = act[:, SGU_W + lo:SGU_W + hi]
        ms = jnp.mean(v * v, axis=-1, keepdims=True)
        vn = (v * lax.rsqrt(ms + EPS) * sgug_ref[:, lo:hi]).astype(BF16)
        w = sguw_ref[g]
        for c in range(TM // CHUNK):
            r0, r1 = c * CHUNK, (c + 1) * CHUNK
            mix = jnp.dot(w, vn[r0:r1, :], preferred_element_type=F32) + sgub_ref[:, lo:hi]
            a_ref[r0:r1, lo:hi] = (u[r0:r1, :] * mix).astype(BF16)
        fg = p[:, 2 * SGU_W + lo:2 * SGU_W + hi].astype(BF16)
        spectra.append(jnp.dot(fg, dft_ref[...], preferred_element_type=F32))

    @pl.when(is_ctx)
    def _():
        for g, yy in enumerate(spectra):
            y_ref[:, g * GW:(g + 1) * GW] = yy[:, :GW].astype(BF16)
            y_ref[:, FNET_W + g * GW:FNET_W + (g + 1) * GW] = yy[:, GW:].astype(BF16)

    @pl.when(jnp.logical_not(is_ctx))
    def _():
        for g, yy in enumerate(spectra):
            for half, dst in ((0, y1r_ref), (1, y2r_ref)):
                slab = slab_s.at[2 * g + half]
                slab[...] = yy[:, half * GW:(half + 1) * GW]
                for n2 in range(DFT_N2):
                    cols = slice(n2 * FNET_W + g * GW, n2 * FNET_W + (g + 1) * GW)
                    dst[:, cols] = slab[pl.ds(n2, TM // DFT_N2, stride=DFT_N2), :].astype(BF16)


def _inproj(x_ctx, x_lat, mod, g, w_in, sgu_g, sgu_w, sgu_b, tb, ffn_w1, ffn_w3, ffn_w2):
    assert T // TM >= 3 * FFN_CAST_BLOCKS
    ctx, lat = _two_stream_specs(TM, D)
    npt = NP // TM
    rows_r = TM // DFT_N2
    yr_shape = jax.ShapeDtypeStruct((NS // DFT_N2, DFT_N2 * FNET_W), BF16)
    yr_spec = pl.BlockSpec((rows_r, DFT_N2 * FNET_W), lambda i: (jnp.maximum(i - npt, 0), 0))
    cast_specs = [_row_block_spec(w.shape, FFN_CAST_BLOCKS, k) for k, w in enumerate((ffn_w1, ffn_w3, ffn_w2))]
    return pl.pallas_call(
        _inproj_kernel,
        out_shape=(jax.ShapeDtypeStruct((T, SGU_W), BF16), jax.ShapeDtypeStruct((NP, 2 * FNET_W), BF16),
                   yr_shape, yr_shape) + tuple(jax.ShapeDtypeStruct(w.shape, BF16) for w in (ffn_w1, ffn_w3, ffn_w2)),
        grid=(T // TM,),
        in_specs=[
            ctx, lat,
            _mod_spec(0, TM),
            _resident((1, D)),
            _resident((D, 3 * SGU_W)),
            _resident((1, SGU_W)),
            _resident((4, CHUNK, CHUNK)),
            _resident((CHUNK, SGU_W)),
            _resident((GW, 2 * GW)),
        ] + cast_specs,
        out_specs=(pl.BlockSpec((TM, SGU_W), lambda i: (i, 0)),
                   pl.BlockSpec((TM, 2 * FNET_W), lambda i: (jnp.minimum(i, npt - 1), 0)), yr_spec, yr_spec,
                   *cast_specs),
        scratch_shapes=[pltpu.VMEM((8, TM, LANES), F32)],
        compiler_params=_params(("arbitrary",)),
        name="inproj_mix",
    )(x_ctx, x_lat, mod, g, w_in, sgu_g, sgu_w, sgu_b, tb["dft_ch"], ffn_w1, ffn_w3, ffn_w2)


def _mix_out(f, a_ref, x_ref, mod_ref, wout_ref, o_ref):
    mix = (jnp.dot(a_ref[...], wout_ref[:SGU_W, :], preferred_element_type=F32)
           + jnp.dot(f.astype(BF16), wout_ref[SGU_W:, :], preferred_element_type=F32))
    o_ref[...] = x_ref[...] + mod_ref[2:3, :] * mix


def _mixout_ctx_kernel(y_ref, a_ref, x_ref, mod_ref, dft_ref, wout_ref, wqkv_ref, o_ref, wqkvb_ref):
    _side_cast(pl.program_id(0), N_CTX_SEQ // CTX_PER_STEP, ((wqkv_ref, wqkvb_ref),))
    fs = []
    for s in range(CTX_PER_STEP):
        rows = slice(s * CTX_LEN, (s + 1) * CTX_LEN)
        fs.append(jnp.dot(dft_ref[:, :CTX_LEN], y_ref[rows, :FNET_W], preferred_element_type=F32)
                  + jnp.dot(dft_ref[:, CTX_LEN:], y_ref[rows, FNET_W:], preferred_element_type=F32))
    f = jnp.concatenate(fs, axis=0) * (1.0 / math.sqrt(CTX_LEN * GW))
    _mix_out(f, a_ref, x_ref, mod_ref, wout_ref, o_ref)


def _mixout_ctx(y, a, x, mod, w_out, tb, w_qkv):
    rows = CTX_PER_STEP * CTX_LEN
    steps = N_CTX_SEQ // CTX_PER_STEP
    tok = lambda w: pl.BlockSpec((rows, w), lambda i: (i, 0))
    cast_spec = _row_block_spec(w_qkv.shape, steps, 0)
    return pl.pallas_call(
        _mixout_ctx_kernel,
        out_shape=(jax.ShapeDtypeStruct((NP, D), F32), jax.ShapeDtypeStruct(w_qkv.shape, BF16)),
        grid=(steps,),
        in_specs=[tok(2 * FNET_W), tok(SGU_W), tok(D),
                  pl.BlockSpec((None, None, 6, D), lambda i: (0, CTX_ROW, 0, 0)),
                  _resident((CTX_LEN, 2 * CTX_LEN)), _resident((D, D)), cast_spec],
        out_specs=(tok(D), cast_spec),
        compiler_params=_params(("arbitrary",)),
        name="mixout_ctx",
    )(y, a, x, mod, tb["dft_ctx"], w_out, w_qkv)


def _dft1_kernel(y1_ref, y2_ref, c_ref, s_ref, twc_ref, tws_ref, br_ref, bi_ref):
    y1 = y1_ref[...]
    y2 = y2_ref[...]
    c = c_ref[...]
    s = s_ref[...]
    ar = jnp.dot(c, y1, preferred_element_type=F32) - jnp.dot(s, y2, preferred_element_type=F32)
    ai = -(jnp.dot(c, y2, preferred_element_type=F32) + jnp.dot(s, y1, preferred_element_type=F32))
    lane = lax.broadcasted_iota(I32, (DFT_N1, DFT_N2), 1)
    for r in range(DFT1_GROUP):
        n2 = pl.program_id(1) * DFT1_GROUP + r
        cols = slice(r * FNET_W, (r + 1) * FNET_W)
        tc = jnp.sum(jnp.where(lane == n2, twc_ref[...], 0.0), axis=-1, keepdims=True)
        ts = jnp.sum(jnp.where(lane == n2, tws_ref[...], 0.0), axis=-1, keepdims=True)
        br_ref[r] = (tc * ar[:, cols] + ts * ai[:, cols]).astype(BF16)
        bi_ref[r] = (tc * ai[:, cols] - ts * ar[:, cols]).astype(BF16)


def _dft1(y1r, y2r, tb):
    shp = jax.ShapeDtypeStruct((N_LAT_SEQ, DFT_N2, DFT_N1, FNET_W), BF16)
    src = pl.BlockSpec((None, DFT_N1, DFT1_GROUP * FNET_W), lambda b, n: (b, 0, n))
    dst = pl.BlockSpec((None, DFT1_GROUP, DFT_N1, FNET_W), lambda b, n: (b, n, 0, 0))
    return pl.pallas_call(
        _dft1_kernel,
        out_shape=(shp, shp),
        grid=(N_LAT_SEQ, DFT_N2 // DFT1_GROUP),
        in_specs=[src, src, _resident((DFT_N1, DFT_N1)), _resident((DFT_N1, DFT_N1)),
                  _resident((DFT_N1, DFT_N2)), _resident((DFT_N1, DFT_N2))],
        out_specs=(dst, dst),
        compiler_params=_params(("arbitrary", "arbitrary")),
        name="dft_stage1",
    )(y1r, y2r, tb["c256"], tb["s256"], tb["tw_cos"], tb["tw_sin"])


def _cmul_const(z, wr, wi):
    re, im = z
    tol = 1e-12
    if abs(wi) < tol:
        return (re, im) if wr > 0 else (-re, -im)
    if abs(wr) < tol:
        return (-im, re) if wi > 0 else (im, -re)
    return (re * wr - im * wi, re * wi + im * wr)


def _fft(zs):
    n = len(zs)
    if n == 1:
        return zs
    even, odd = _fft(zs[0::2]), _fft(zs[1::2])
    out = [None] * n
    for k in range(n // 2):
        ang = 2.0 * math.pi * k / n
        t = _cmul_const(odd[k], math.cos(ang), -math.sin(ang))
        out[k] = (even[k][0] + t[0], even[k][1] + t[1])
        out[k + n // 2] = (even[k][0] - t[0], even[k][1] - t[1])
    return out


def _mixout_lat_kernel(br_ref, bi_ref, a_ref, x_ref, mod_ref, wout_ref, o_ref):
    spec = _fft([(br_ref[n2].astype(F32), bi_ref[n2].astype(F32)) for n2 in range(DFT_N2)])
    f = jnp.concatenate([re for re, _ in spec], axis=0) * (1.0 / math.sqrt(LAT_LEN * GW))
    rows = DFT_N2 * DFT_ROWS
    mix = (jnp.dot(a_ref[...].reshape(rows, SGU_W), wout_ref[:SGU_W, :], preferred_element_type=F32)
           + jnp.dot(f.astype(BF16), wout_ref[SGU_W:, :], preferred_element_type=F32))
    out = x_ref[...].reshape(rows, D) + mod_ref[2:3, :] * mix
    o_ref[...] = out.reshape(DFT_N2, DFT_ROWS, D)


def _mixout_lat(br, bi, a_lat, x, mod, w_out):
    blk = lambda w, off=0: pl.BlockSpec((None, DFT_N2, DFT_ROWS, w), lambda b, r: (b + off, 0, r, 0))
    return pl.pallas_call(
        _mixout_lat_kernel,
        out_shape=jax.ShapeDtypeStruct((N_LAT_SEQ, DFT_N2, DFT_N1, D), F32),
        grid=(N_LAT_SEQ, DFT_N1 // DFT_ROWS),
        in_specs=[blk(FNET_W), blk(FNET_W), blk(SGU_W, NP // LAT_LEN), blk(D),
                  pl.BlockSpec((None, None, 6, D), lambda b, r: (0, b, 0, 0)),
                  _resident((D, D))],
        out_specs=blk(D),
        compiler_params=_params(("arbitrary", "arbitrary")),
        name="mixout_lat",
    )(br, bi, a_lat, x, mod, w_out)


def _ffn_kernel(xc_ref, xl_ref, mod_ref, g_ref, w1_ref, w3_ref, w2_ref, o_ref):
    x = _pick_stream(TM, xc_ref, xl_ref)
    h = _modulate(x, g_ref[...], mod_ref[4:5, :], mod_ref[3:4, :]).astype(BF16)
    acc = None
    for c in range(D_FF // FFN_CHUNK):
        lo, hi = c * FFN_CHUNK, (c + 1) * FFN_CHUNK
        a = jnp.dot(h, w1_ref[:, lo:hi], preferred_element_type=F32)
        b = jnp.dot(h, w3_ref[:, lo:hi], preferred_element_type=F32)
        d = jnp.dot((_silu(a) * b).astype(BF16), w2_ref[lo:hi, :], preferred_element_type=F32)
        acc = d if acc is None else acc + d
    o_ref[...] = x + mod_ref[5:6, :] * acc


def _ffn(x_ctx, x_lat, mod, g, w1, w3, w2):
    ctx, lat = _two_stream_specs(TM, D)
    return pl.pallas_call(
        _ffn_kernel,
        out_shape=jax.ShapeDtypeStruct((T, D), F32),
        grid=(T // TM,),
        in_specs=[
            ctx, lat,
            _mod_spec(0, TM),
            _resident((1, D)),
            _resident((D, D_FF)),
            _resident((D, D_FF)),
            _resident((D_FF, D)),
        ],
        out_specs=pl.BlockSpec((TM, D), lambda i: (i, 0)),
        compiler_params=_params(("arbitrary",)),
        name="ffn",
    )(x_ctx, x_lat, mod, g, w1, w3, w2)


def _qkv_ctx_kernel(x_ref, mod_ref, g_ref, w_ref, q_ref, k_ref, v_ref, kf_ref, vf_ref):
    h = _modulate(x_ref[...], g_ref[...], mod_ref[1:2, :], mod_ref[0:1, :]).astype(BF16)
    qkv = jnp.dot(h, w_ref[...], preferred_element_type=F32)
    k = qkv[:, D:2 * D]
    v = qkv[:, 2 * D:]
    q_ref[...] = (qkv[:, :D] * Q_SCALE).astype(BF16)
    k_ref[...] = k.astype(BF16)
    v_ref[...] = v.astype(BF16)
    kf_ref[...] = k
    vf_ref[...] = v


def _qkv_lat_kernel(x_ref, mod_ref, g_ref, w_ref, cos_ref, sin_ref, q_ref, k_ref, v_ref):
    h = _modulate(x_ref[...], g_ref[...], mod_ref[1:2, :], mod_ref[0:1, :]).astype(BF16)
    qkv = jnp.dot(h, w_ref[...], preferred_element_type=F32)
    cos = cos_ref[...]
    sin = sin_ref[...]
    first = (lax.broadcasted_iota(I32, (TM, LANES), 1) & 31) < 16

    def rotate(xh):
        partner = jnp.where(first, pltpu.roll(xh, LANES - 16, 1), pltpu.roll(xh, 16, 1))
        return xh * cos + partner * sin

    for hh in range(N_HEADS):
        sl = slice(hh * LANES, (hh + 1) * LANES)
        q_ref[:, sl] = (rotate(qkv[:, sl]) * Q_SCALE).astype(BF16)
        k_ref[:, sl] = rotate(qkv[:, D + hh * LANES:D + (hh + 1) * LANES]).astype(BF16)
    v_ref[...] = qkv[:, 2 * D:].astype(BF16)


def _qkv(x, mod, g, w_qkv, tb):
    tok = pl.BlockSpec((TM, D), lambda i: (i, 0))
    bshape = jax.ShapeDtypeStruct((NP, D), BF16)
    lshape = jax.ShapeDtypeStruct((NS, D), BF16)
    fshape = jax.ShapeDtypeStruct((NP, D), F32)
    qc, kc, vc, kf, vf = pl.pallas_call(
        _qkv_ctx_kernel,
        out_shape=(bshape, bshape, bshape, fshape, fshape),
        grid=(NP // TM,),
        in_specs=[tok, _mod_spec(1, TM), _resident((1, D)), _resident((D, 3 * D))],
        out_specs=(tok, tok, tok, tok, tok),
        compiler_params=_params(("arbitrary",)),
        name="qkv_ctx",
    )(x, mod, g, w_qkv)
    off = NP // TM
    ltok = pl.BlockSpec((TM, D), lambda i: (i + off, 0))
    rope = pl.BlockSpec((TM, LANES), lambda i: (i % (LAT_LEN // TM), 0))
    ql, kl, vl = pl.pallas_call(
        _qkv_lat_kernel,
        out_shape=(lshape, lshape, lshape),
        grid=(NS // TM,),
        in_specs=[ltok, _mod_spec(1, TM, off), _resident((1, D)), _resident((D, 3 * D)), rope, rope],
        out_specs=(tok, tok, tok),
        compiler_params=_params(("arbitrary",)),
        name="qkv_lat",
    )(x, mod, g, w_qkv, tb["rope_cos"], tb["rope_sin"])
    return (qc, kc, vc), (ql, kl, vl), kf, vf


def _lam(lp_ref):
    lp = lp_ref[...]
    a = jnp.sum(lp[0:1, :] * lp[1:2, :], axis=-1, keepdims=True)
    b = jnp.sum(lp[2:3, :] * lp[3:4, :], axis=-1, keepdims=True)
    return jnp.exp(a) - jnp.exp(b) + LAM0


def _head_norm(o, sg):
    ms = jnp.mean(o * o, axis=-1, keepdims=True)
    return o * lax.rsqrt(ms + EPS) * (sg * (1.0 - LAM0))


_NT = (((1,), (1,)), ((), ()))


def _attn_ctx_kernel(q_ref, k_ref, v_ref, lp_ref, sg_ref, o_ref):
    lam = _lam(lp_ref)
    map0 = lax.broadcasted_iota(I32, (CTX_LEN, LANES), 1) < HEAD_DIM
    init = _softmax_init(CTX_LEN)
    for hh in range(N_HEADS):
        sl = slice(hh * LANES, (hh + 1) * LANES)
        q = q_ref[:, sl]
        zero = jnp.zeros_like(q)
        outs = []
        for qc in (jnp.where(map0, q, zero), jnp.where(map0, zero, q)):
            _, l, acc = _softmax_step(qc, k_ref[:, sl], v_ref[:, sl], init)
            outs.append(acc / jnp.sum(l, axis=-1, keepdims=True))
        o_ref[:, sl] = _head_norm(outs[0] - lam * outs[1], sg_ref[...]).astype(BF16)


def _attn_ctx(q, k, v, lam_p, subln_g):
    seq = pl.BlockSpec((CTX_LEN, D), lambda i: (i, 0))
    return pl.pallas_call(
        _attn_ctx_kernel,
        out_shape=jax.ShapeDtypeStruct((NP, D), BF16),
        grid=(N_CTX_SEQ,),
        in_specs=[seq, seq, seq, _resident((4, HEAD_DIM)), _resident((1, V_DIM))],
        out_specs=seq,
        compiler_params=_params(("arbitrary",)),
        name="attn_ctx",
    )(q, k, v, lam_p, subln_g)


def _softmax_init(rows):
    return (jnp.full((rows, LANES), -jnp.inf, F32), jnp.zeros((rows, LANES), F32), jnp.zeros((rows, LANES), F32))


def _softmax_step(qc, kc, vc, state):
    m, l, acc = state
    s = lax.dot_general(qc, kc, _NT, preferred_element_type=F32)
    blocks = [s[:, j * LANES:(j + 1) * LANES] for j in range(s.shape[1] // LANES)]
    bm = functools.reduce(jnp.maximum, blocks)
    m_new = jnp.maximum(m, jnp.max(bm, axis=-1, keepdims=True))
    alpha = jnp.exp2(m - m_new)
    ps = [jnp.exp2(b - m_new) for b in blocks]
    l_new = alpha * l + functools.reduce(jnp.add, ps)
    p = jnp.concatenate(ps, axis=1).astype(BF16)
    acc_new = alpha * acc + jnp.dot(p, vc, preferred_element_type=F32)
    return m_new, l_new, acc_new


def _attn_lat_kernel(q_ref, k_ref, v_ref, ck_ref, cv_ref, lp_ref, sg_ref, w1_ref, w3_ref, w2_ref,
                     o_ref, w1b_ref, w3b_ref, w2b_ref):
    step = (pl.program_id(0) * N_HEADS + pl.program_id(1)) * (LAT_LEN // TQ) + pl.program_id(2)
    _side_cast(step, CAST_BLOCKS, ((w1_ref, w1b_ref), (w3_ref, w3b_ref), (w2_ref, w2b_ref)))

    q = q_ref[...]
    zero = jnp.zeros_like(q)
    map0 = lax.broadcasted_iota(I32, (TQ, LANES), 1) < HEAD_DIM
    qs = (jnp.where(map0, q, zero), jnp.where(map0, zero, q))
    init = _softmax_init(TQ)
    states = [init, init]
    head_rows = pl.ds(pl.program_id(1), CTX_LEN, stride=N_HEADS)
    chunks = [(ck_ref[head_rows, :].astype(BF16), cv_ref[head_rows, :].astype(BF16))]
    chunks += [(k_ref[j * TK:(j + 1) * TK, :], v_ref[j * TK:(j + 1) * TK, :]) for j in range(LAT_LEN // TK)]
    for kc, vc in chunks:
        states = [_softmax_step(qs[c], kc, vc, states[c]) for c in range(2)]
    outs = [acc / jnp.sum(l, axis=-1, keepdims=True) for _, l, acc in states]
    o = outs[0] - _lam(lp_ref) * outs[1]
    o_ref[...] = _head_norm(o, sg_ref[...]).astype(BF16)


def _attn_lat(q, k, v, cache_k, cache_v, lam_p, subln_g, moe_w1, moe_w3, moe_w2):
    nq = LAT_LEN // TQ
    assert N_LAT_SEQ * N_HEADS * nq >= 3 * CAST_BLOCKS
    qspec = pl.BlockSpec((TQ, LANES), lambda b, h, i: (b * nq + i, h))
    kspec = pl.BlockSpec((LAT_LEN, LANES), lambda b, h, i: (b, h))
    cspec = pl.BlockSpec((None, CTX_LEN * N_HEADS, LANES), lambda b, h, i: (b, 0, 0))

    def cast_block(phase, b, h, i):
        j = jnp.clip((b * N_HEADS + h) * nq + i - phase * CAST_BLOCKS, 0, CAST_BLOCKS - 1)
        return j // CAST_SPLIT, j % CAST_SPLIT

    def up_spec(phase):
        return pl.BlockSpec((None, D, CAST_COLS), lambda b, h, i: (cast_block(phase, b, h, i)[0], 0,
                                                                    cast_block(phase, b, h, i)[1]))

    down_spec = pl.BlockSpec((None, CAST_COLS, D), lambda b, h, i: (*cast_block(2, b, h, i), 0))
    up_shape = jax.ShapeDtypeStruct((N_EXPERTS, D, D_FF_EXPERT), BF16)
    return pl.pallas_call(
        _attn_lat_kernel,
        out_shape=(jax.ShapeDtypeStruct((NS, D), BF16), up_shape, up_shape,
                   jax.ShapeDtypeStruct((N_EXPERTS, D_FF_EXPERT, D), BF16)),
        grid=(N_LAT_SEQ, N_HEADS, nq),
        in_specs=[qspec, kspec, kspec, cspec, cspec, _resident((4, HEAD_DIM)), _resident((1, V_DIM)),
                  up_spec(0), up_spec(1), down_spec],
        out_specs=(qspec, up_spec(0), up_spec(1), down_spec),
        compiler_params=_params(("arbitrary", "arbitrary", "arbitrary")),
        name="attn_lat",
    )(q, k, v, cache_k, cache_v, lam_p, subln_g, moe_w1, moe_w3, moe_w2)


def _oproj_route_kernel(oc_ref, ol_ref, x_ref, mod_ref, wo_ref, g_ref, wr_ref, tri_ref, below_ref, x3_ref, h_ref,
                        route_ref, cnt_ref):
    is_ctx = pl.program_id(0) < NP // (ROUTE_SUB * TM)
    for sub in range(ROUTE_SUB):
        rows = slice(sub * TM, (sub + 1) * TM)
        o = jnp.where(is_ctx, oc_ref[rows, :], ol_ref[rows, :])
        _route_tile(o, x_ref[rows, :], mod_ref, wo_ref, g_ref, wr_ref, tri_ref, below_ref,
                    x3_ref.at[rows, :], h_ref.at[rows, :], route_ref.at[rows, :], cnt_ref.at[sub])


def _route_tile(o, x, mod_ref, wo_ref, g_ref, wr_ref, tri_ref, below_ref, x3_ref, h_ref, route_ref, cnt_ref):
    x3 = x + mod_ref[2:3, :] * jnp.dot(o, wo_ref[...], preferred_element_type=F32)
    x3_ref[...] = x3
    hb = _modulate(x3, g_ref[...], mod_ref[4:5, :], mod_ref[3:4, :]).astype(BF16)
    h_ref[...] = hb

    logits = jnp.dot(hb, wr_ref[...], preferred_element_type=F32)
    lane = lax.broadcasted_iota(I32, (TM, LANES), 1).astype(F32)
    neg = jnp.float32(-jnp.inf)
    lg = jnp.where(lane < N_EXPERTS, logits, neg)
    v1 = jnp.max(lg, axis=-1, keepdims=True)
    i1 = jnp.min(jnp.where(lg == v1, lane, float(LANES)), axis=-1, keepdims=True)
    lg2 = jnp.where(lane == i1, neg, lg)
    v2 = jnp.max(lg2, axis=-1, keepdims=True)
    i2 = jnp.min(jnp.where(lg2 == v2, lane, float(LANES)), axis=-1, keepdims=True)
    e = jnp.exp(v2 - v1)
    g1 = 1.0 / (1.0 + e)
    g2 = e / (1.0 + e)
    sel1 = lane == i1
    sel2 = lane == i2
    onehot = jnp.where(sel1 | sel2, 1.0, 0.0)
    rank = jnp.dot(tri_ref[...], onehot.astype(BF16), preferred_element_type=F32)
    cnt8 = jnp.floor((jnp.sum(onehot, axis=0, keepdims=True) + (SEG_ALIGN - 1)) * (1.0 / SEG_ALIGN))
    base8 = jnp.dot(jnp.broadcast_to(cnt8, (8, LANES)).astype(BF16), below_ref[...], preferred_element_type=F32)
    slot = base8[0:1, :] * float(SEG_ALIGN) + rank
    pos1 = jnp.sum(jnp.where(sel1, slot, 0.0), axis=-1, keepdims=True)
    pos2 = jnp.sum(jnp.where(sel2, slot, 0.0), axis=-1, keepdims=True)
    cnt_ref[...] = cnt8
    route = jnp.where(lane == 0, i1, 0.0)
    route = jnp.where(lane == 1, i2, route)
    route = jnp.where(lane == 2, g1, route)
    route = jnp.where(lane == 3, g2, route)
    route = jnp.where(lane == 4, pos1, route)
    route = jnp.where(lane == 5, pos2, route)
    route_ref[...] = route


def _oproj_route(o_ctx, o_lat, x, mod, w_o, g, w_r, tb):
    rows = ROUTE_SUB * TM
    tok = pl.BlockSpec((rows, D), lambda i: (i, 0))
    ctx, lat = _two_stream_specs(rows, D)
    return pl.pallas_call(
        _oproj_route_kernel,
        out_shape=(jax.ShapeDtypeStruct((T, D), F32), jax.ShapeDtypeStruct((T, D), BF16),
                   jax.ShapeDtypeStruct((T, LANES), F32), jax.ShapeDtypeStruct((T // TM, 1, LANES), F32)),
        grid=(T // rows,),
        in_specs=[ctx, lat, tok, _mod_spec(1, rows), _resident((D, D)), _resident((1, D)), _resident((D, LANES)),
                  _resident((TM, TM)), _resident((LANES, LANES))],
        out_specs=(tok, tok, pl.BlockSpec((rows, LANES), lambda i: (i, 0)),
                   pl.BlockSpec((ROUTE_SUB, 1, LANES), lambda i: (i, 0, 0))),
        compiler_params=_params(("arbitrary",)),
        name="oproj_route",
    )(o_ctx, o_lat, x, mod, w_o, g, w_r, tb["tri"], tb["below"])


def _bit_copies(n, src0, dst0, bits, make_copy, op):
    for bit in bits:
        done = n & (-2 * bit)

        @pl.when((n & bit) != 0)
        def _(done=done, bit=bit):
            src = pl.multiple_of((src0 + done) * SEG_ALIGN, SEG_ALIGN)
            dst = pl.multiple_of((dst0 + done) * SEG_ALIGN, SEG_ALIGN)
            op(make_copy(src, dst, bit * SEG_ALIGN))


def _segment_copies(tab, make_copy, op):
    for e in range(N_EXPERTS):
        _bit_copies(tab(e), tab(N_EXPERTS + e), tab(2 * N_EXPERTS + e), SEG_BITS, make_copy, op)


def _start(copy):
    copy.start()


def _wait(copy):
    copy.wait()


def _dispatch_kernel(tab_ref, fill_ref, h_ref, route_ref, xs_ref, sorted_s, zero_s, sem, fill_sem):
    i = pl.program_id(0)
    slot = i % 2
    slots = route_ref[...].T
    row = lax.broadcasted_iota(I32, (L_TILE, TM), 0).astype(F32)
    perm = jnp.where((row == slots[4:5, :]) | (row == slots[5:6, :]), 1.0, 0.0).astype(BF16)
    sorted_s[slot] = jnp.dot(perm, h_ref[...], preferred_element_type=F32)

    def copies(tile, s, op):
        def seg(src, dst, rows):
            return pltpu.make_async_copy(sorted_s.at[s, pl.ds(src, rows), :], xs_ref.at[pl.ds(dst, rows), :], sem.at[s])

        _segment_copies(lambda j: tab_ref[tile, j], seg, op)

    copies(i, slot, _start)

    @pl.when(i > 0)
    def _():
        copies(i - 1, 1 - slot, _wait)

    @pl.when(i == pl.num_programs(0) - 1)
    def _():
        copies(i, slot, _wait)

    @pl.when(i == 0)
    def _():
        zero_s[...] = jnp.zeros(zero_s.shape, F32)

        def fill(src, dst, rows):
            del src
            return pltpu.make_async_copy(zero_s.at[pl.ds(0, rows), :], xs_ref.at[pl.ds(dst, rows), :], fill_sem)

        def fills(op):
            for e in range(N_EXPERTS):
                _bit_copies(fill_ref[0, N_EXPERTS + e], 0, fill_ref[0, e], SEG_BITS[1:], fill, op)
            for k in range(MAX_UNUSED_TILES):
                @pl.when(k < fill_ref[0, 2 * N_EXPERTS + 1])
                def _(k=k):
                    dst = pl.multiple_of(fill_ref[0, 2 * N_EXPERTS] * SEG_ALIGN + k * T_MOE, SEG_ALIGN)
                    op(fill(0, dst, T_MOE))

        fills(_start)
        fills(_wait)


def _dispatch(tab, fill, h, route):
    smem = pl.BlockSpec(memory_space=pltpu.SMEM)
    return pl.pallas_call(
        _dispatch_kernel,
        out_shape=jax.ShapeDtypeStruct((N_ROWS, D), F32),
        grid=(T // TM,),
        in_specs=[smem, smem, pl.BlockSpec((TM, D), lambda i: (i, 0)), pl.BlockSpec((TM, LANES), lambda i: (i, 0))],
        out_specs=pl.BlockSpec(memory_space=pl.ANY),
        scratch_shapes=[pltpu.VMEM((2, L_TILE, D), F32), pltpu.VMEM((T_MOE, D), F32),
                        pltpu.SemaphoreType.DMA((2,)), pltpu.SemaphoreType.DMA(())],
        compiler_params=_params(("arbitrary",)),
        name="moe_dispatch",
    )(tab, fill, h, route)


def _moe_kernel(te_ref, tv_ref, xs_ref, w1_ref, w3_ref, w2_ref, y_ref, xb_s, acc_s):
    del te_ref
    i = pl.program_id(0)
    f = pl.program_id(1)
    valid = tv_ref[i] == 1

    def chunk(x):
        a = jnp.dot(x, w1_ref[...], preferred_element_type=F32)
        b = jnp.dot(x, w3_ref[...], preferred_element_type=F32)
        return jnp.dot((_silu(a) * b).astype(BF16), w2_ref[...], preferred_element_type=F32)

    @pl.when(valid & (f == 0))
    def _():
        x = xs_ref[...].astype(BF16)
        xb_s[...] = x
        acc_s[...] = chunk(x)

    @pl.when(valid & (f == N_F_MOE - 1))
    def _():
        y_ref[...] = acc_s[...] + chunk(xb_s[...])

    @pl.when(jnp.logical_not(valid) & (f == N_F_MOE - 1))
    def _():
        y_ref[...] = jnp.zeros(y_ref.shape, F32)


def _moe(tile_expert, tile_valid, xs, w1, w3, w2):
    def fidx(i, f, te, tv):
        return jnp.where(tv[i] == 1, f, N_F_MOE - 1)

    grid_spec = pltpu.PrefetchScalarGridSpec(
        num_scalar_prefetch=2,
        grid=(N_MOE_TILES, N_F_MOE),
        in_specs=[
            pl.BlockSpec((T_MOE, D), lambda i, f, te, tv: (i, 0)),
            pl.BlockSpec((None, D, F_MOE), lambda i, f, te, tv: (te[i], 0, fidx(i, f, te, tv))),
            pl.BlockSpec((None, D, F_MOE), lambda i, f, te, tv: (te[i], 0, fidx(i, f, te, tv))),
            pl.BlockSpec((None, F_MOE, D), lambda i, f, te, tv: (te[i], fidx(i, f, te, tv), 0)),
        ],
        out_specs=pl.BlockSpec((T_MOE, D), lambda i, f, te, tv: (i, 0)),
        scratch_shapes=[pltpu.VMEM((T_MOE, D), BF16), pltpu.VMEM((T_MOE, D), F32)],
    )
    return pl.pallas_call(
        _moe_kernel,
        out_shape=jax.ShapeDtypeStruct((N_ROWS, D), F32),
        grid_spec=grid_spec,
        compiler_params=_params(("arbitrary", "arbitrary")),
        name="moe_experts",
    )(tile_expert, tile_valid, xs, w1, w3, w2)


def _combine_kernel(tab_ref, y_ref, x_ref, route_ref, mod_ref, fg_ref, o_ref, buf, sem, *, first_tile):
    i = pl.program_id(0)
    slot = i % 2

    def fetch(tile, s, op):
        def seg(src, dst, rows):
            return pltpu.make_async_copy(y_ref.at[pl.ds(dst, rows), :], buf.at[s, pl.ds(src, rows), :], sem.at[s])

        _segment_copies(lambda j: tab_ref[tile, j], seg, op)

    @pl.when(i == 0)
    def _():
        buf[...] = jnp.zeros(buf.shape, F32)
        fetch(first_tile, 0, _start)

    @pl.when(i + 1 < pl.num_programs(0))
    def _():
        fetch(first_tile + i + 1, 1 - slot, _start)

    fetch(first_tile + i, slot, _wait)
    yb = buf[slot].astype(BF16)
    route = route_ref[...]
    col = lax.broadcasted_iota(I32, (TM, L_TILE), 1).astype(F32)
    picks = [jnp.dot(jnp.where(col == route[:, c:c + 1], 1.0, 0.0).astype(BF16), yb, preferred_element_type=F32)
             for c in (4, 5)]
    moe = route[:, 2:3] * picks[0] + route[:, 3:4] * picks[1]
    x = x_ref[...] + mod_ref[5:6, :] * moe
    ms = jnp.mean(x * x, axis=-1, keepdims=True)
    o_ref[...] = x * lax.rsqrt(ms + EPS) * fg_ref[...]


def _combine(tab, y, x, route, mod, final_g, n_tok, off_tiles):
    tok = lambda w: pl.BlockSpec((TM, w), lambda i: (i + off_tiles, 0))
    return pl.pallas_call(
        functools.partial(_combine_kernel, first_tile=off_tiles),
        out_shape=jax.ShapeDtypeStruct((n_tok, D), F32),
        grid=(n_tok // TM,),
        in_specs=[
            pl.BlockSpec(memory_space=pltpu.SMEM),
            pl.BlockSpec(memory_space=pl.ANY),
            tok(D), tok(LANES), _mod_spec(1, TM, off_tiles), _resident((1, D)),
        ],
        out_specs=pl.BlockSpec((TM, D), lambda i: (i, 0)),
        scratch_shapes=[pltpu.VMEM((2, L_TILE, D), F32), pltpu.SemaphoreType.DMA((2,))],
        compiler_params=_params(("arbitrary",)),
        name="moe_combine",
    )(tab, y, x, route, mod, final_g)


def kernel(x_prompt, x_sample, cache_k, cache_v, c, c_ctx, ada_w, ada_b, norm1_g, norm2_g, final_g, mix_w_in, sgu_g, sgu_w, sgu_b, mix_w_out, ffn_w1, ffn_w3, ffn_w2, attn_w_qkv, lam_q1, lam_k1, lam_q2, lam_k2, subln_g, attn_w_o, router_w, moe_w1, moe_w3, moe_w2):
    tb = _device_tables()
    xc = x_prompt.reshape(NP, D)
    xl = x_sample.reshape(NS, D)
    cvec = jnp.concatenate([c, c_ctx[None, :], jnp.zeros((3, D), F32)], axis=0)
    mod = _adaln(cvec, ada_w, ada_b)

    bias = jnp.repeat(sgu_b[0].T, GW, axis=1)
    a, y, y1r, y2r, ffn_w1b, ffn_w3b, ffn_w2b = _inproj(
        xc, xl, mod, norm1_g[0][None, :], mix_w_in[0].astype(BF16), sgu_g[0].reshape(1, SGU_W), sgu_w[0].astype(BF16),
        bias, tb, ffn_w1[0], ffn_w3[0], ffn_w2[0])
    w_out = mix_w_out[0].astype(BF16)
    x1c, w_qkvb = _mixout_ctx(y, a, xc, mod, w_out, tb, attn_w_qkv[0])
    br, bi = _dft1(y1r.reshape(N_LAT_SEQ, DFT_N1, DFT_N2 * FNET_W), y2r.reshape(N_LAT_SEQ, DFT_N1, DFT_N2 * FNET_W), tb)
    x1l = _mixout_lat(br, bi, a.reshape(T // LAT_LEN, DFT_N2, DFT_N1, SGU_W),
                      xl.reshape(N_LAT_SEQ, DFT_N2, DFT_N1, D), mod, w_out).reshape(NS, D)
    x2 = _ffn(x1c, x1l, mod, norm2_g[0][None, :], ffn_w1b, ffn_w3b, ffn_w2b)

    qkv_c, qkv_l, kf, vf = _qkv(x2, mod, norm1_g[1][None, :], w_qkvb, tb)
    lam_p = jnp.stack([lam_q1[0], lam_k1[0], lam_q2[0], lam_k2[0]], axis=0)
    sg = subln_g[0][None, :]
    oc = _attn_ctx(*qkv_c, lam_p, sg)
    cache_rows = (N_LAT_SEQ, CTX_LEN * N_HEADS, LANES)
    ol, w1b, w3b, w2b = _attn_lat(*qkv_l, cache_k.reshape(cache_rows), cache_v.reshape(cache_rows), lam_p, sg,
                                  moe_w1[0], moe_w3[0], moe_w2[0])

    w_r = jnp.pad(router_w[0], ((0, 0), (0, LANES - N_EXPERTS))).astype(BF16)
    x3, h2, route, cnt = _oproj_route(oc, ol, x2, mod, attn_w_o[0].astype(BF16), norm2_g[1][None, :], w_r, tb)
    unit_per_tile = T_MOE // SEG_ALIGN
    seg_len = cnt[:, 0, :N_EXPERTS].astype(I32)
    seg_local = jnp.cumsum(seg_len, axis=1) - seg_len
    group_len = jnp.sum(seg_len, axis=0)
    group_tiles = (group_len + unit_per_tile - 1) // unit_per_tile
    tile_ends = jnp.cumsum(group_tiles)
    group_start = (tile_ends - group_tiles) * unit_per_tile
    seg_global = group_start[None, :] + jnp.cumsum(seg_len, axis=0) - seg_len
    tab = jnp.concatenate([seg_len, seg_local, seg_global], axis=1)
    tab = jnp.pad(tab, ((0, 0), (0, LANES - 3 * N_EXPERTS)))
    fill = jnp.concatenate([group_start + group_len, group_tiles * unit_per_tile - group_len,
                            tile_ends[-1:] * unit_per_tile, N_MOE_TILES - tile_ends[-1:]])
    fill = jnp.pad(fill, (0, LANES - 2 * N_EXPERTS - 2)).reshape(1, LANES)
    tile_idx = jnp.arange(N_MOE_TILES, dtype=I32)
    tile_valid = (tile_idx < tile_ends[-1]).astype(I32)
    tile_expert = jnp.sum((jnp.minimum(tile_idx, tile_ends[-1] - 1)[:, None] >= tile_ends[None, :]).astype(I32), axis=-1)

    xs = _dispatch(tab, fill, h2, route)
    ys = _moe(tile_expert, tile_valid, xs, w1b, w3b, w2b)
    fg = final_g[None, :]
    y_prompt = _combine(tab, ys, x3, route, mod, fg, NP, 0)
    y_sample = _combine(tab, ys, x3, route, mod, fg, NS, NP // TM)

    return (y_prompt.reshape(N_CTX_SEQ, CTX_LEN, D), y_sample.reshape(N_LAT_SEQ, LAT_LEN, D),
            kf.reshape(N_CTX_SEQ, 1, CTX_LEN, N_HEADS, 2 * HEAD_DIM), vf.reshape(N_CTX_SEQ, 1, CTX_LEN, N_HEADS, V_DIM))
```

```python
import functools
import math

import numpy as np
import jax
import jax.numpy as jnp
from jax import lax
from jax.experimental import pallas as pl
from jax.experimental.pallas import tpu as pltpu

F32 = jnp.float32
BF16 = jnp.bfloat16
I32 = jnp.int32

D = 1024
N_CTX_SEQ = 32
CTX_LEN = 256
N_LAT_SEQ = 4
LAT_LEN = 4096
NP = N_CTX_SEQ * CTX_LEN
NS = N_LAT_SEQ * LAT_LEN
T = NP + NS
GRID_W = 64
CHUNK = 128
SGU_W = 512
FNET_W = 512
GW = 128
N_HEADS = 8
HEAD_DIM = 64
V_DIM = 128
ROPE_THETA = 10000.0
D_FF = 2816
N_EXPERTS = 8
D_FF_EXPERT = 3584
EPS = 1e-6
LAM0 = 0.8 - 0.6 * math.exp(-0.3 * 1)
CTX_ROW = 4

LANES = 128
TM = 512
TQ = 1024
TK = 2048
Q_SCALE = HEAD_DIM ** -0.5 * math.log2(math.e)
T_MOE = 512
F_MOE = 1792
N_F_MOE = D_FF_EXPERT // F_MOE
assert N_F_MOE == 2
CAST_SPLIT = 4
CAST_COLS = D_FF_EXPERT // CAST_SPLIT
CAST_BLOCKS = N_EXPERTS * CAST_SPLIT
SEG_ALIGN = 8
SEG_BITS = (64, 32, 16, 8, 4, 2, 1)
L_TILE = 1152
N_ROWS = -(-(2 * T + (T // TM) * N_EXPERTS * (SEG_ALIGN - 1) + N_EXPERTS * (T_MOE - SEG_ALIGN)) // T_MOE) * T_MOE
N_MOE_TILES = N_ROWS // T_MOE
MAX_UNUSED_TILES = N_MOE_TILES - 2 * T // T_MOE
FFN_CHUNK = 2816
DFT_N1 = 256
DFT_N2 = 16
DFT_ROWS = 32
DFT1_GROUP = 4
CTX_PER_STEP = 2
ROUTE_SUB = 2
VMEM_LIMIT = 56 * 2 ** 20


def _params(sem, vmem=VMEM_LIMIT):
    return pltpu.CompilerParams(dimension_semantics=sem, vmem_limit_bytes=vmem)


def _resident(shape):
    nd = len(shape)
    return pl.BlockSpec(shape, lambda *_: (0,) * nd, pipeline_mode=pl.Buffered(1))


def _mod_row(i, tm):
    npt = NP // tm
    return jnp.where(i < npt, CTX_ROW, (i - npt) // (LAT_LEN // tm))


def _mod_spec(layer, tm, offset_tiles=0):
    return pl.BlockSpec((None, None, 6, D), lambda i, *_: (layer, _mod_row(i + offset_tiles, tm), 0, 0))


def _two_stream_specs(tm, width):
    npt = NP // tm
    ctx = pl.BlockSpec((tm, width), lambda i: (jnp.minimum(i, npt - 1), 0))
    lat = pl.BlockSpec((tm, width), lambda i: (jnp.maximum(i - npt, 0), 0))
    return ctx, lat


def _pick_stream(tm, ctx_ref, lat_ref):
    return jnp.where(pl.program_id(0) < NP // tm, ctx_ref[...], lat_ref[...])


def _modulate(x, g, scale, shift):
    ms = jnp.mean(x * x, axis=-1, keepdims=True)
    return x * lax.rsqrt(ms + EPS) * (g * (1.0 + scale)) + shift


def _silu(a):
    return a * jax.nn.sigmoid(a)


def _gelu_tanh(x):
    return 0.5 * x * (1.0 + jnp.tanh(0.7978845608028654 * (x + 0.044715 * (x * x * x))))


@functools.lru_cache(maxsize=None)
def _tables():
    def cs(n):
        k = np.arange(n, dtype=np.int64)
        ang = 2.0 * np.pi * ((k[:, None] * k[None, :]) % n) / n
        return np.cos(ang), np.sin(ang)

    c128, s128 = cs(GW)
    dft_ch = np.concatenate([c128, s128], axis=1)
    c256, s256 = cs(CTX_LEN)
    dft_ctx = np.concatenate([c256, -s256], axis=1)
    k1 = np.arange(DFT_N1, dtype=np.int64)[:, None]
    n2 = np.arange(DFT_N2, dtype=np.int64)[None, :]
    tw = 2.0 * np.pi * ((k1 * n2) % LAT_LEN) / LAT_LEN
    half = HEAD_DIM // 2
    inv_freq = ROPE_THETA ** (-np.arange(0, half, 2, dtype=np.float64) / half)
    t = np.arange(LAT_LEN)
    ang_r = (t // GRID_W).astype(np.float64)[:, None] * inv_freq[None, :]
    ang_c = (t % GRID_W).astype(np.float64)[:, None] * inv_freq[None, :]

    def blk(ang):
        c = np.cos(ang)
        s = np.sin(ang)
        return np.concatenate([c, c], axis=1), np.concatenate([-s, s], axis=1)

    cr, sr = blk(ang_r)
    cc, sc = blk(ang_c)
    cos64 = np.concatenate([cr, cc], axis=1)
    sin64 = np.concatenate([sr, sc], axis=1)
    rope_cos = np.concatenate([cos64, cos64], axis=1).astype(np.float32)
    rope_sin = np.concatenate([sin64, sin64], axis=1).astype(np.float32)
    tri = np.tril(np.ones((TM, TM), np.float32), k=-1)
    below = np.triu(np.ones((LANES, LANES), np.float32), k=1)
    f32 = lambda a: np.asarray(a, np.float32)
    return dict(dft_ch=f32(dft_ch), dft_ctx=f32(dft_ctx), c256=f32(c256), s256=f32(s256),
                tw_cos=f32(np.cos(tw)), tw_sin=f32(np.sin(tw)),
                rope_cos=rope_cos, rope_sin=rope_sin, tri=tri, below=below)


def _device_tables():
    tb = {k: jnp.asarray(v) for k, v in _tables().items()}
    for k in ("dft_ch", "dft_ctx", "c256", "s256", "tri", "below"):
        tb[k] = tb[k].astype(BF16)
    return tb


def _adaln_kernel(c_ref, w_ref, b_ref, o_ref):
    s = _silu(c_ref[...]).astype(BF16)
    o_ref[...] = jnp.dot(s, w_ref[...].astype(BF16), preferred_element_type=F32) + b_ref[...]


def _adaln(cvec, ada_w, ada_b):
    tn = 1536
    out = pl.pallas_call(
        _adaln_kernel,
        out_shape=jax.ShapeDtypeStruct((2, 8, 6 * D), F32),
        grid=(2, 6 * D // tn),
        in_specs=[
            pl.BlockSpec((8, D), lambda l, j: (0, 0)),
            pl.BlockSpec((None, D, tn), lambda l, j: (l, 0, j)),
            pl.BlockSpec((None, 1, tn), lambda l, j: (l, 0, j)),
        ],
        out_specs=pl.BlockSpec((None, 8, tn), lambda l, j: (l, 0, j)),
        compiler_params=_params(("arbitrary", "arbitrary")),
        name="adaln",
    )(cvec, ada_w, ada_b.reshape(2, 1, 6 * D))
    return out.reshape(2, 8, 6, D)


def _inproj_kernel(xc_ref, xl_ref, mod_ref, g_ref, win_ref, sgug_ref, sguw_ref, sgub_ref, dft_ref, a_ref, y_ref,
                   y1r_ref, y2r_ref, slab_s):
    is_ctx = pl.program_id(0) < NP // TM
    x = _pick_stream(TM, xc_ref, xl_ref)
    h = _modulate(x, g_ref[...], mod_ref[1:2, :], mod_ref[0:1, :]).astype(BF16)
    p = jnp.dot(h, win_ref[...], preferred_element_type=F32)
    act = _gelu_tanh(p[:, :2 * SGU_W])
    spectra = []
    for g in range(4):
        lo, hi = g * GW, (g + 1) * GW
        u = act[:, lo:hi]
        v = act[:, SGU_W + lo:SGU_W + hi]
        ms = jnp.mean(v * v, axis=-1, keepdims=True)
        vn = (v * lax.rsqrt(ms + EPS) * sgug_ref[:, lo:hi]).astype(BF16)
        chunks = jnp.concatenate([vn[c * CHUNK:(c + 1) * CHUNK, :] for c in range(TM // CHUNK)], axis=1)
        mixed = jnp.dot(sguw_ref[g], chunks, preferred_element_type=F32)
        for c in range(TM // CHUNK):
            r0, r1 = c * CHUNK, (c + 1) * CHUNK
            mix = mixed[:, c * GW:(c + 1) * GW] + sgub_ref[:, lo:hi]
            a_ref[r0:r1, lo:hi] = (u[r0:r1, :] * mix).astype(BF16)
        fg = p[:, 2 * SGU_W + lo:2 * SGU_W + hi].astype(BF16)
        spectra.append(jnp.dot(fg, dft_ref[...], preferred_element_type=F32))

    @pl.when(is_ctx)
    def _():
        for g, yy in enumerate(spectra):
            y_ref[:, g * GW:(g + 1) * GW] = yy[:, :GW].astype(BF16)
            y_ref[:, FNET_W + g * GW:FNET_W + (g + 1) * GW] = yy[:, GW:].astype(BF16)

    @pl.when(jnp.logical_not(is_ctx))
    def _():
        for g, yy in enumerate(spectra):
            for half, dst in ((0, y1r_ref), (1, y2r_ref)):
                slab = slab_s.at[2 * g + half]
                slab[...] = yy[:, half * GW:(half + 1) * GW]
                for n2 in range(DFT_N2):
                    cols = slice(n2 * FNET_W + g * GW, n2 * FNET_W + (g + 1) * GW)
                    dst[:, cols] = slab[pl.ds(n2, TM // DFT_N2, stride=DFT_N2), :].astype(BF16)


def _inproj(x_ctx, x_lat, mod, g, w_in, sgu_g, sgu_w, sgu_b, tb):
    ctx, lat = _two_stream_specs(TM, D)
    npt = NP // TM
    rows_r = TM // DFT_N2
    yr_shape = jax.ShapeDtypeStruct((NS // DFT_N2, DFT_N2 * FNET_W), BF16)
    yr_spec = pl.BlockSpec((rows_r, DFT_N2 * FNET_W), lambda i: (jnp.maximum(i - npt, 0), 0))
    return pl.pallas_call(
        _inproj_kernel,
        out_shape=(jax.ShapeDtypeStruct((T, SGU_W), BF16), jax.ShapeDtypeStruct((NP, 2 * FNET_W), BF16),
                   yr_shape, yr_shape),
        grid=(T // TM,),
        in_specs=[
            ctx, lat,
            _mod_spec(0, TM),
            _resident((1, D)),
            _resident((D, 3 * SGU_W)),
            _resident((1, SGU_W)),
            _resident((4, CHUNK, CHUNK)),
            _resident((CHUNK, SGU_W)),
            _resident((GW, 2 * GW)),
        ],
        out_specs=(pl.BlockSpec((TM, SGU_W), lambda i: (i, 0)),
                   pl.BlockSpec((TM, 2 * FNET_W), lambda i: (jnp.minimum(i, npt - 1), 0)), yr_spec, yr_spec),
        scratch_shapes=[pltpu.VMEM((8, TM, LANES), F32)],
        compiler_params=_params(("arbitrary",)),
        name="inproj_mix",
    )(x_ctx, x_lat, mod, g, w_in, sgu_g, sgu_w, sgu_b, tb["dft_ch"])


def _mix_out(f, a_ref, x_ref, mod_ref, wout_ref, o_ref):
    mix = (jnp.dot(a_ref[...], wout_ref[:SGU_W, :], preferred_element_type=F32)
           + jnp.dot(f.astype(BF16), wout_ref[SGU_W:, :], preferred_element_type=F32))
    o_ref[...] = x_ref[...] + mod_ref[2:3, :] * mix


def _mixout_ctx_kernel(y_ref, a_ref, x_ref, mod_ref, dft_ref, wout_ref, o_ref):
    fs = []
    for s in range(CTX_PER_STEP):
        rows = slice(s * CTX_LEN, (s + 1) * CTX_LEN)
        fs.append(jnp.dot(dft_ref[:, :CTX_LEN], y_ref[rows, :FNET_W], preferred_element_type=F32)
                  + jnp.dot(dft_ref[:, CTX_LEN:], y_ref[rows, FNET_W:], preferred_element_type=F32))
    f = jnp.concatenate(fs, axis=0) * (1.0 / math.sqrt(CTX_LEN * GW))
    _mix_out(f, a_ref, x_ref, mod_ref, wout_ref, o_ref)


def _mixout_ctx(y, a, x, mod, w_out, tb):
    rows = CTX_PER_STEP * CTX_LEN
    tok = lambda w: pl.BlockSpec((rows, w), lambda i: (i, 0))
    return pl.pallas_call(
        _mixout_ctx_kernel,
        out_shape=jax.ShapeDtypeStruct((NP, D), F32),
        grid=(N_CTX_SEQ // CTX_PER_STEP,),
        in_specs=[tok(2 * FNET_W), tok(SGU_W), tok(D),
                  pl.BlockSpec((None, None, 6, D), lambda i: (0, CTX_ROW, 0, 0)),
                  _resident((CTX_LEN, 2 * CTX_LEN)), _resident((D, D))],
        out_specs=tok(D),
        compiler_params=_params(("arbitrary",)),
        name="mixout_ctx",
    )(y, a, x, mod, tb["dft_ctx"], w_out)


def _dft1_kernel(y1_ref, y2_ref, c_ref, s_ref, twc_ref, tws_ref, br_ref, bi_ref):
    y1 = y1_ref[...]
    y2 = y2_ref[...]
    c = c_ref[...]
    s = s_ref[...]
    ar = jnp.dot(c, y1, preferred_element_type=F32) - jnp.dot(s, y2, preferred_element_type=F32)
    ai = -(jnp.dot(c, y2, preferred_element_type=F32) + jnp.dot(s, y1, preferred_element_type=F32))
    lane = lax.broadcasted_iota(I32, (DFT_N1, DFT_N2), 1)
    for r in range(DFT1_GROUP):
        n2 = pl.program_id(1) * DFT1_GROUP + r
        cols = slice(r * FNET_W, (r + 1) * FNET_W)
        tc = jnp.sum(jnp.where(lane == n2, twc_ref[...], 0.0), axis=-1, keepdims=True)
        ts = jnp.sum(jnp.where(lane == n2, tws_ref[...], 0.0), axis=-1, keepdims=True)
        br_ref[r] = (tc * ar[:, cols] + ts * ai[:, cols]).astype(BF16)
        bi_ref[r] = (tc * ai[:, cols] - ts * ar[:, cols]).astype(BF16)


def _dft1(y1r, y2r, tb):
    shp = jax.ShapeDtypeStruct((N_LAT_SEQ, DFT_N2, DFT_N1, FNET_W), BF16)
    src = pl.BlockSpec((None, DFT_N1, DFT1_GROUP * FNET_W), lambda b, n: (b, 0, n))
    dst = pl.BlockSpec((None, DFT1_GROUP, DFT_N1, FNET_W), lambda b, n: (b, n, 0, 0))
    return pl.pallas_call(
        _dft1_kernel,
        out_shape=(shp, shp),
        grid=(N_LAT_SEQ, DFT_N2 // DFT1_GROUP),
        in_specs=[src, src, _resident((DFT_N1, DFT_N1)), _resident((DFT_N1, DFT_N1)),
                  _resident((DFT_N1, DFT_N2)), _resident((DFT_N1, DFT_N2))],
        out_specs=(dst, dst),
        compiler_params=_params(("arbitrary", "arbitrary")),
        name="dft_stage1",
    )(y1r, y2r, tb["c256"], tb["s256"], tb["tw_cos"], tb["tw_sin"])


def _cmul_const(z, wr, wi):
    re, im = z
    tol = 1e-12
    if abs(wi) < tol:
        return (re, im) if wr > 0 else (-re, -im)
    if abs(wr) < tol:
        return (-im, re) if wi > 0 else (im, -re)
    return (re * wr - im * wi, re * wi + im * wr)


def _fft(zs):
    n = len(zs)
    if n == 1:
        return zs
    even, odd = _fft(zs[0::2]), _fft(zs[1::2])
    out = [None] * n
    for k in range(n // 2):
        ang = 2.0 * math.pi * k / n
        t = _cmul_const(odd[k], math.cos(ang), -math.sin(ang))
        out[k] = (even[k][0] + t[0], even[k][1] + t[1])
        out[k + n // 2] = (even[k][0] - t[0], even[k][1] - t[1])
    return out


def _mixout_lat_kernel(br_ref, bi_ref, a_ref, x_ref, mod_ref, wout_ref, o_ref):
    spec = _fft([(br_ref[n2].astype(F32), bi_ref[n2].astype(F32)) for n2 in range(DFT_N2)])
    f = jnp.concatenate([re for re, _ in spec], axis=0) * (1.0 / math.sqrt(LAT_LEN * GW))
    rows = DFT_N2 * DFT_ROWS
    mix = (jnp.dot(a_ref[...].reshape(rows, SGU_W), wout_ref[:SGU_W, :], preferred_element_type=F32)
           + jnp.dot(f.astype(BF16), wout_ref[SGU_W:, :], preferred_element_type=F32))
    out = x_ref[...].reshape(rows, D) + mod_ref[2:3, :] * mix
    o_ref[...] = out.reshape(DFT_N2, DFT_ROWS, D)


def _mixout_lat(br, bi, a_lat, x, mod, w_out):
    blk = lambda w, off=0: pl.BlockSpec((None, DFT_N2, DFT_ROWS, w), lambda b, r: (b + off, 0, r, 0))
    return pl.pallas_call(
        _mixout_lat_kernel,
        out_shape=jax.ShapeDtypeStruct((N_LAT_SEQ, DFT_N2, DFT_N1, D), F32),
        grid=(N_LAT_SEQ, DFT_N1 // DFT_ROWS),
        in_specs=[blk(FNET_W), blk(FNET_W), blk(SGU_W, NP // LAT_LEN), blk(D),
                  pl.BlockSpec((None, None, 6, D), lambda b, r: (0, b, 0, 0)),
                  _resident((D, D))],
        out_specs=blk(D),
        compiler_params=_params(("arbitrary", "arbitrary")),
        name="mixout_lat",
    )(br, bi, a_lat, x, mod, w_out)


def _ffn_kernel(xc_ref, xl_ref, mod_ref, g_ref, w1_ref, w3_ref, w2_ref, o_ref):
    x = _pick_stream(TM, xc_ref, xl_ref)
    h = _modulate(x, g_ref[...], mod_ref[4:5, :], mod_ref[3:4, :]).astype(BF16)
    acc = None
    for c in range(D_FF // FFN_CHUNK):
        lo, hi = c * FFN_CHUNK, (c + 1) * FFN_CHUNK
        a = jnp.dot(h, w1_ref[:, lo:hi], preferred_element_type=F32)
        b = jnp.dot(h, w3_ref[:, lo:hi], preferred_element_type=F32)
        d = jnp.dot((_silu(a) * b).astype(BF16), w2_ref[lo:hi, :], preferred_element_type=F32)
        acc = d if acc is None else acc + d
    o_ref[...] = x + mod_ref[5:6, :] * acc


def _ffn(x_ctx, x_lat, mod, g, w1, w3, w2):
    ctx, lat = _two_stream_specs(TM, D)
    return pl.pallas_call(
        _ffn_kernel,
        out_shape=jax.ShapeDtypeStruct((T, D), F32),
        grid=(T // TM,),
        in_specs=[
            ctx, lat,
            _mod_spec(0, TM),
            _resident((1, D)),
            _resident((D, D_FF)),
            _resident((D, D_FF)),
            _resident((D_FF, D)),
        ],
        out_specs=pl.BlockSpec((TM, D), lambda i: (i, 0)),
        compiler_params=_params(("arbitrary",)),
        name="ffn",
    )(x_ctx, x_lat, mod, g, w1, w3, w2)


def _qkv_ctx_kernel(x_ref, mod_ref, g_ref, w_ref, q_ref, k_ref, v_ref, kf_ref, vf_ref):
    h = _modulate(x_ref[...], g_ref[...], mod_ref[1:2, :], mod_ref[0:1, :]).astype(BF16)
    qkv = jnp.dot(h, w_ref[...], preferred_element_type=F32)
    k = qkv[:, D:2 * D]
    v = qkv[:, 2 * D:]
    q_ref[...] = (qkv[:, :D] * Q_SCALE).astype(BF16)
    k_ref[...] = k.astype(BF16)
    v_ref[...] = v.astype(BF16)
    kf_ref[...] = k
    vf_ref[...] = v


def _qkv_lat_kernel(x_ref, mod_ref, g_ref, w_ref, cos_ref, sin_ref, q_ref, k_ref, v_ref):
    h = _modulate(x_ref[...], g_ref[...], mod_ref[1:2, :], mod_ref[0:1, :]).astype(BF16)
    qkv = jnp.dot(h, w_ref[...], preferred_element_type=F32)
    cos = cos_ref[...]
    sin = sin_ref[...]
    first = (lax.broadcasted_iota(I32, (TM, LANES), 1) & 31) < 16

    def rotate(xh):
        partner = jnp.where(first, pltpu.roll(xh, LANES - 16, 1), pltpu.roll(xh, 16, 1))
        return xh * cos + partner * sin

    for hh in range(N_HEADS):
        sl = slice(hh * LANES, (hh + 1) * LANES)
        q_ref[:, sl] = (rotate(qkv[:, sl]) * Q_SCALE).astype(BF16)
        k_ref[:, sl] = rotate(qkv[:, D + hh * LANES:D + (hh + 1) * LANES]).astype(BF16)
    v_ref[...] = qkv[:, 2 * D:].astype(BF16)


def _qkv(x, mod, g, w_qkv, tb):
    tok = pl.BlockSpec((TM, D), lambda i: (i, 0))
    bshape = jax.ShapeDtypeStruct((NP, D), BF16)
    lshape = jax.ShapeDtypeStruct((NS, D), BF16)
    fshape = jax.ShapeDtypeStruct((NP, D), F32)
    qc, kc, vc, kf, vf = pl.pallas_call(
        _qkv_ctx_kernel,
        out_shape=(bshape, bshape, bshape, fshape, fshape),
        grid=(NP // TM,),
        in_specs=[tok, _mod_spec(1, TM), _resident((1, D)), _resident((D, 3 * D))],
        out_specs=(tok, tok, tok, tok, tok),
        compiler_params=_params(("arbitrary",)),
        name="qkv_ctx",
    )(x, mod, g, w_qkv)
    off = NP // TM
    ltok = pl.BlockSpec((TM, D), lambda i: (i + off, 0))
    rope = pl.BlockSpec((TM, LANES), lambda i: (i % (LAT_LEN // TM), 0))
    ql, kl, vl = pl.pallas_call(
        _qkv_lat_kernel,
        out_shape=(lshape, lshape, lshape),
        grid=(NS // TM,),
        in_specs=[ltok, _mod_spec(1, TM, off), _resident((1, D)), _resident((D, 3 * D)), rope, rope],
        out_specs=(tok, tok, tok),
        compiler_params=_params(("arbitrary",)),
        name="qkv_lat",
    )(x, mod, g, w_qkv, tb["rope_cos"], tb["rope_sin"])
    return (qc, kc, vc), (ql, kl, vl), kf, vf


def _lam(lp_ref):
    lp = lp_ref[...]
    a = jnp.sum(lp[0:1, :] * lp[1:2, :], axis=-1, keepdims=True)
    b = jnp.sum(lp[2:3, :] * lp[3:4, :], axis=-1, keepdims=True)
    return jnp.exp(a) - jnp.exp(b) + LAM0


def _head_norm(o, sg):
    ms = jnp.mean(o * o, axis=-1, keepdims=True)
    return o * lax.rsqrt(ms + EPS) * (sg * (1.0 - LAM0))


_NT = (((1,), (1,)), ((), ()))


def _attn_ctx_kernel(q_ref, k_ref, v_ref, lp_ref, sg_ref, o_ref):
    lam = _lam(lp_ref)
    map0 = lax.broadcasted_iota(I32, (CTX_LEN, LANES), 1) < HEAD_DIM
    init = _softmax_init(CTX_LEN)
    for hh in range(N_HEADS):
        sl = slice(hh * LANES, (hh + 1) * LANES)
        q = q_ref[:, sl]
        zero = jnp.zeros_like(q)
        outs = []
        for qc in (jnp.where(map0, q, zero), jnp.where(map0, zero, q)):
            _, l, acc = _softmax_step(qc, k_ref[:, sl], v_ref[:, sl], init)
            outs.append(acc / jnp.sum(l, axis=-1, keepdims=True))
        o_ref[:, sl] = _head_norm(outs[0] - lam * outs[1], sg_ref[...]).astype(BF16)


def _attn_ctx(q, k, v, lam_p, subln_g):
    seq = pl.BlockSpec((CTX_LEN, D), lambda i: (i, 0))
    return pl.pallas_call(
        _attn_ctx_kernel,
        out_shape=jax.ShapeDtypeStruct((NP, D), BF16),
        grid=(N_CTX_SEQ,),
        in_specs=[seq, seq, seq, _resident((4, HEAD_DIM)), _resident((1, V_DIM))],
        out_specs=seq,
        compiler_params=_params(("arbitrary",)),
        name="attn_ctx",
    )(q, k, v, lam_p, subln_g)


def _softmax_init(rows):
    return (jnp.full((rows, LANES), -jnp.inf, F32), jnp.zeros((rows, LANES), F32), jnp.zeros((rows, LANES), F32))


def _softmax_step(qc, kc, vc, state):
    m, l, acc = state
    s = lax.dot_general(qc, kc, _NT, preferred_element_type=F32)
    blocks = [s[:, j * LANES:(j + 1) * LANES] for j in range(s.shape[1] // LANES)]
    bm = functools.reduce(jnp.maximum, blocks)
    m_new = jnp.maximum(m, jnp.max(bm, axis=-1, keepdims=True))
    alpha = jnp.exp2(m - m_new)
    ps = [jnp.exp2(b - m_new) for b in blocks]
    l_new = alpha * l + functools.reduce(jnp.add, ps)
    p = jnp.concatenate(ps, axis=1).astype(BF16)
    acc_new = alpha * acc + jnp.dot(p, vc, preferred_element_type=F32)
    return m_new, l_new, acc_new


def _attn_lat_kernel(q_ref, k_ref, v_ref, ck_ref, cv_ref, lp_ref, sg_ref, w1_ref, w3_ref, w2_ref,
                     o_ref, w1b_ref, w3b_ref, w2b_ref):
    step = (pl.program_id(0) * N_HEADS + pl.program_id(1)) * (LAT_LEN // TQ) + pl.program_id(2)
    for phase, (src, dst) in enumerate(((w1_ref, w1b_ref), (w3_ref, w3b_ref), (w2_ref, w2b_ref))):
        @pl.when((step >= phase * CAST_BLOCKS) & (step < (phase + 1) * CAST_BLOCKS))
        def _(src=src, dst=dst):
            dst[...] = src[...].astype(BF16)

    q = q_ref[...]
    zero = jnp.zeros_like(q)
    map0 = lax.broadcasted_iota(I32, (TQ, LANES), 1) < HEAD_DIM
    qs = (jnp.where(map0, q, zero), jnp.where(map0, zero, q))
    init = _softmax_init(TQ)
    states = [init, init]
    head_rows = pl.ds(pl.program_id(1), CTX_LEN, stride=N_HEADS)
    chunks = [(ck_ref[head_rows, :].astype(BF16), cv_ref[head_rows, :].astype(BF16))]
    chunks += [(k_ref[j * TK:(j + 1) * TK, :], v_ref[j * TK:(j + 1) * TK, :]) for j in range(LAT_LEN // TK)]
    for kc, vc in chunks:
        states = [_softmax_step(qs[c], kc, vc, states[c]) for c in range(2)]
    outs = [acc / jnp.sum(l, axis=-1, keepdims=True) for _, l, acc in states]
    o = outs[0] - _lam(lp_ref) * outs[1]
    o_ref[...] = _head_norm(o, sg_ref[...]).astype(BF16)


def _attn_lat(q, k, v, cache_k, cache_v, lam_p, subln_g, moe_w1, moe_w3, moe_w2):
    nq = LAT_LEN // TQ
    assert N_LAT_SEQ * N_HEADS * nq >= 3 * CAST_BLOCKS
    qspec = pl.BlockSpec((TQ, LANES), lambda b, h, i: (b * nq + i, h))
    kspec = pl.BlockSpec((LAT_LEN, LANES), lambda b, h, i: (b, h))
    cspec = pl.BlockSpec((None, CTX_LEN * N_HEADS, LANES), lambda b, h, i: (b, 0, 0))

    def cast_block(phase, b, h, i):
        j = jnp.clip((b * N_HEADS + h) * nq + i - phase * CAST_BLOCKS, 0, CAST_BLOCKS - 1)
        return j // CAST_SPLIT, j % CAST_SPLIT

    def up_spec(phase):
        return pl.BlockSpec((None, D, CAST_COLS), lambda b, h, i: (cast_block(phase, b, h, i)[0], 0,
                                                                    cast_block(phase, b, h, i)[1]))

    down_spec = pl.BlockSpec((None, CAST_COLS, D), lambda b, h, i: (*cast_block(2, b, h, i), 0))
    up_shape = jax.ShapeDtypeStruct((N_EXPERTS, D, D_FF_EXPERT), BF16)
    return pl.pallas_call(
        _attn_lat_kernel,
        out_shape=(jax.ShapeDtypeStruct((NS, D), BF16), up_shape, up_shape,
                   jax.ShapeDtypeStruct((N_EXPERTS, D_FF_EXPERT, D), BF16)),
        grid=(N_LAT_SEQ, N_HEADS, nq),
        in_specs=[qspec, kspec, kspec, cspec, cspec, _resident((4, HEAD_DIM)), _resident((1, V_DIM)),
                  up_spec(0), up_spec(1), down_spec],
        out_specs=(qspec, up_spec(0), up_spec(1), down_spec),
        compiler_params=_params(("arbitrary", "arbitrary", "arbitrary")),
        name="attn_lat",
    )(q, k, v, cache_k, cache_v, lam_p, subln_g, moe_w1, moe_w3, moe_w2)


def _oproj_route_kernel(oc_ref, ol_ref, x_ref, mod_ref, wo_ref, g_ref, wr_ref, tri_ref, below_ref, x3_ref, h_ref,
                        route_ref, cnt_ref):
    is_ctx = pl.program_id(0) < NP // (ROUTE_SUB * TM)
    for sub in range(ROUTE_SUB):
        rows = slice(sub * TM, (sub + 1) * TM)
        o = jnp.where(is_ctx, oc_ref[rows, :], ol_ref[rows, :])
        _route_tile(o, x_ref[rows, :], mod_ref, wo_ref, g_ref, wr_ref, tri_ref, below_ref,
                    x3_ref.at[rows, :], h_ref.at[rows, :], route_ref.at[rows, :], cnt_ref.at[sub])


def _route_tile(o, x, mod_ref, wo_ref, g_ref, wr_ref, tri_ref, below_ref, x3_ref, h_ref, route_ref, cnt_ref):
    x3 = x + mod_ref[2:3, :] * jnp.dot(o, wo_ref[...], preferred_element_type=F32)
    x3_ref[...] = x3
    hb = _modulate(x3, g_ref[...], mod_ref[4:5, :], mod_ref[3:4, :]).astype(BF16)
    h_ref[...] = hb

    logits = jnp.dot(hb, wr_ref[...], preferred_element_type=F32)
    lane = lax.broadcasted_iota(I32, (TM, LANES), 1).astype(F32)
    neg = jnp.float32(-jnp.inf)
    lg = jnp.where(lane < N_EXPERTS, logits, neg)
    v1 = jnp.max(lg, axis=-1, keepdims=True)
    i1 = jnp.min(jnp.where(lg == v1, lane, float(LANES)), axis=-1, keepdims=True)
    lg2 = jnp.where(lane == i1, neg, lg)
    v2 = jnp.max(lg2, axis=-1, keepdims=True)
    i2 = jnp.min(jnp.where(lg2 == v2, lane, float(LANES)), axis=-1, keepdims=True)
    e = jnp.exp(v2 - v1)
    g1 = 1.0 / (1.0 + e)
    g2 = e / (1.0 + e)
    sel1 = lane == i1
    sel2 = lane == i2
    onehot = jnp.where(sel1 | sel2, 1.0, 0.0)
    rank = jnp.dot(tri_ref[...], onehot.astype(BF16), preferred_element_type=F32)
    cnt8 = jnp.floor((jnp.sum(onehot, axis=0, keepdims=True) + (SEG_ALIGN - 1)) * (1.0 / SEG_ALIGN))
    base8 = jnp.dot(jnp.broadcast_to(cnt8, (8, LANES)).astype(BF16), below_ref[...], preferred_element_type=F32)
    slot = base8[0:1, :] * float(SEG_ALIGN) + rank
    pos1 = jnp.sum(jnp.where(sel1, slot, 0.0), axis=-1, keepdims=True)
    pos2 = jnp.sum(jnp.where(sel2, slot, 0.0), axis=-1, keepdims=True)
    cnt_ref[...] = cnt8
    route = jnp.where(lane == 0, i1, 0.0)
    route = jnp.where(lane == 1, i2, route)
    route = jnp.where(lane == 2, g1, route)
    route = jnp.where(lane == 3, g2, route)
    route = jnp.where(lane == 4, pos1, route)
    route = jnp.where(lane == 5, pos2, route)
    route_ref[...] = route


def _oproj_route(o_ctx, o_lat, x, mod, w_o, g, w_r, tb):
    rows = ROUTE_SUB * TM
    tok = pl.BlockSpec((rows, D), lambda i: (i, 0))
    ctx, lat = _two_stream_specs(rows, D)
    return pl.pallas_call(
        _oproj_route_kernel,
        out_shape=(jax.ShapeDtypeStruct((T, D), F32), jax.ShapeDtypeStruct((T, D), BF16),
                   jax.ShapeDtypeStruct((T, LANES), F32), jax.ShapeDtypeStruct((T // TM, 1, LANES), F32)),
        grid=(T // rows,),
        in_specs=[ctx, lat, tok, _mod_spec(1, rows), _resident((D, D)), _resident((1, D)), _resident((D, LANES)),
                  _resident((TM, TM)), _resident((LANES, LANES))],
        out_specs=(tok, tok, pl.BlockSpec((rows, LANES), lambda i: (i, 0)),
                   pl.BlockSpec((ROUTE_SUB, 1, LANES), lambda i: (i, 0, 0))),
        compiler_params=_params(("arbitrary",)),
        name="oproj_route",
    )(o_ctx, o_lat, x, mod, w_o, g, w_r, tb["tri"], tb["below"])


def _bit_copies(n, src0, dst0, bits, make_copy, op):
    for bit in bits:
        done = n & (-2 * bit)

        @pl.when((n & bit) != 0)
        def _(done=done, bit=bit):
            src = pl.multiple_of((src0 + done) * SEG_ALIGN, SEG_ALIGN)
            dst = pl.multiple_of((dst0 + done) * SEG_ALIGN, SEG_ALIGN)
            op(make_copy(src, dst, bit * SEG_ALIGN))


def _segment_copies(tab, make_copy, op):
    for e in range(N_EXPERTS):
        _bit_copies(tab(e), tab(N_EXPERTS + e), tab(2 * N_EXPERTS + e), SEG_BITS, make_copy, op)


def _start(copy):
    copy.start()


def _wait(copy):
    copy.wait()


def _dispatch_kernel(tab_ref, fill_ref, h_ref, route_ref, xs_ref, sorted_s, zero_s, sem, fill_sem):
    i = pl.program_id(0)
    slot = i % 2
    slots = route_ref[...].T
    row = lax.broadcasted_iota(I32, (L_TILE, TM), 0).astype(F32)
    perm = jnp.where((row == slots[4:5, :]) | (row == slots[5:6, :]), 1.0, 0.0).astype(BF16)
    sorted_s[slot] = jnp.dot(perm, h_ref[...], preferred_element_type=F32)

    def copies(tile, s, op):
        def seg(src, dst, rows):
            return pltpu.make_async_copy(sorted_s.at[s, pl.ds(src, rows), :], xs_ref.at[pl.ds(dst, rows), :], sem.at[s])

        _segment_copies(lambda j: tab_ref[tile, j], seg, op)

    copies(i, slot, _start)

    @pl.when(i > 0)
    def _():
        copies(i - 1, 1 - slot, _wait)

    @pl.when(i == pl.num_programs(0) - 1)
    def _():
        copies(i, slot, _wait)

    @pl.when(i == 0)
    def _():
        zero_s[...] = jnp.zeros(zero_s.shape, F32)

        def fill(src, dst, rows):
            del src
            return pltpu.make_async_copy(zero_s.at[pl.ds(0, rows), :], xs_ref.at[pl.ds(dst, rows), :], fill_sem)

        def fills(op):
            for e in range(N_EXPERTS):
                _bit_copies(fill_ref[0, N_EXPERTS + e], 0, fill_ref[0, e], SEG_BITS[1:], fill, op)
            for k in range(MAX_UNUSED_TILES):
                @pl.when(k < fill_ref[0, 2 * N_EXPERTS + 1])
                def _(k=k):
                    dst = pl.multiple_of(fill_ref[0, 2 * N_EXPERTS] * SEG_ALIGN + k * T_MOE, SEG_ALIGN)
                    op(fill(0, dst, T_MOE))

        fills(_start)
        fills(_wait)


def _dispatch(tab, fill, h, route):
    smem = pl.BlockSpec(memory_space=pltpu.SMEM)
    return pl.pallas_call(
        _dispatch_kernel,
        out_shape=jax.ShapeDtypeStruct((N_ROWS, D), F32),
        grid=(T // TM,),
        in_specs=[smem, smem, pl.BlockSpec((TM, D), lambda i: (i, 0)), pl.BlockSpec((TM, LANES), lambda i: (i, 0))],
        out_specs=pl.BlockSpec(memory_space=pl.ANY),
        scratch_shapes=[pltpu.VMEM((2, L_TILE, D), F32), pltpu.VMEM((T_MOE, D), F32),
                        pltpu.SemaphoreType.DMA((2,)), pltpu.SemaphoreType.DMA(())],
        compiler_params=_params(("arbitrary",)),
        name="moe_dispatch",
    )(tab, fill, h, route)


def _moe_kernel(te_ref, tv_ref, xs_ref, w1_ref, w3_ref, w2_ref, y_ref, xb_s, acc_s):
    del te_ref
    i = pl.program_id(0)
    f = pl.program_id(1)
    kind = tv_ref[i]

    def chunk(x):
        a = jnp.dot(x, w1_ref[...], preferred_element_type=F32)
        b = jnp.dot(x, w3_ref[...], preferred_element_type=F32)
        return jnp.dot((_silu(a) * b).astype(BF16), w2_ref[...], preferred_element_type=F32)

    for code, rows in ((1, T_MOE), (2, T_MOE // 2)):
        @pl.when((kind == code) & (f == 0))
        def _(rows=rows):
            x = xs_ref[:rows, :].astype(BF16)
            xb_s[:rows, :] = x
            acc_s[:rows, :] = chunk(x)

        @pl.when((kind == code) & (f == N_F_MOE - 1))
        def _(rows=rows):
            y_ref[:rows, :] = acc_s[:rows, :] + chunk(xb_s[:rows, :])
            if rows < T_MOE:
                y_ref[rows:, :] = jnp.zeros((T_MOE - rows, D), F32)

    @pl.when((kind == 0) & (f == N_F_MOE - 1))
    def _():
        y_ref[...] = jnp.zeros(y_ref.shape, F32)


def _moe(tile_expert, tile_valid, xs, w1, w3, w2):
    def fidx(i, f, te, tv):
        return jnp.where(tv[i] != 0, f, N_F_MOE - 1)

    grid_spec = pltpu.PrefetchScalarGridSpec(
        num_scalar_prefetch=2,
        grid=(N_MOE_TILES, N_F_MOE),
        in_specs=[
            pl.BlockSpec((T_MOE, D), lambda i, f, te, tv: (i, 0)),
            pl.BlockSpec((None, D, F_MOE), lambda i, f, te, tv: (te[i], 0, fidx(i, f, te, tv))),
            pl.BlockSpec((None, D, F_MOE), lambda i, f, te, tv: (te[i], 0, fidx(i, f, te, tv))),
            pl.BlockSpec((None, F_MOE, D), lambda i, f, te, tv: (te[i], fidx(i, f, te, tv), 0)),
        ],
        out_specs=pl.BlockSpec((T_MOE, D), lambda i, f, te, tv: (i, 0)),
        scratch_shapes=[pltpu.VMEM((T_MOE, D), BF16), pltpu.VMEM((T_MOE, D), F32)],
    )
    return pl.pallas_call(
        _moe_kernel,
        out_shape=jax.ShapeDtypeStruct((N_ROWS, D), F32),
        grid_spec=grid_spec,
        compiler_params=_params(("arbitrary", "arbitrary")),
        name="moe_experts",
    )(tile_expert, tile_valid, xs, w1, w3, w2)


def _combine_kernel(tab_ref, y_ref, x_ref, route_ref, mod_ref, fg_ref, o_ref, buf, sem, *, first_tile):
    i = pl.program_id(0)
    slot = i % 2

    def fetch(tile, s, op):
        def seg(src, dst, rows):
            return pltpu.make_async_copy(y_ref.at[pl.ds(dst, rows), :], buf.at[s, pl.ds(src, rows), :], sem.at[s])

        _segment_copies(lambda j: tab_ref[tile, j], seg, op)

    @pl.when(i == 0)
    def _():
        buf[...] = jnp.zeros(buf.shape, F32)
        fetch(first_tile, 0, _start)

    @pl.when(i + 1 < pl.num_programs(0))
    def _():
        fetch(first_tile + i + 1, 1 - slot, _start)

    fetch(first_tile + i, slot, _wait)
    yb = buf[slot].astype(BF16)
    route = route_ref[...]
    col = lax.broadcasted_iota(I32, (TM, L_TILE), 1).astype(F32)
    picks = [jnp.dot(jnp.where(col == route[:, c:c + 1], 1.0, 0.0).astype(BF16), yb, preferred_element_type=F32)
             for c in (4, 5)]
    moe = route[:, 2:3] * picks[0] + route[:, 3:4] * picks[1]
    x = x_ref[...] + mod_ref[5:6, :] * moe
    ms = jnp.mean(x * x, axis=-1, keepdims=True)
    o_ref[...] = x * lax.rsqrt(ms + EPS) * fg_ref[...]


def _combine(tab, y, x, route, mod, final_g, n_tok, off_tiles):
    tok = lambda w: pl.BlockSpec((TM, w), lambda i: (i + off_tiles, 0))
    return pl.pallas_call(
        functools.partial(_combine_kernel, first_tile=off_tiles),
        out_shape=jax.ShapeDtypeStruct((n_tok, D), F32),
        grid=(n_tok // TM,),
        in_specs=[
            pl.BlockSpec(memory_space=pltpu.SMEM),
            pl.BlockSpec(memory_space=pl.ANY),
            tok(D), tok(LANES), _mod_spec(1, TM, off_tiles), _resident((1, D)),
        ],
        out_specs=pl.BlockSpec((TM, D), lambda i: (i, 0)),
        scratch_shapes=[pltpu.VMEM((2, L_TILE, D), F32), pltpu.SemaphoreType.DMA((2,))],
        compiler_params=_params(("arbitrary",)),
        name="moe_combine",
    )(tab, y, x, route, mod, final_g)


def kernel(x_prompt, x_sample, cache_k, cache_v, c, c_ctx, ada_w, ada_b, norm1_g, norm2_g, final_g, mix_w_in, sgu_g, sgu_w, sgu_b, mix_w_out, ffn_w1, ffn_w3, ffn_w2, attn_w_qkv, lam_q1, lam_k1, lam_q2, lam_k2, subln_g, attn_w_o, router_w, moe_w1, moe_w3, moe_w2):
    tb = _device_tables()
    xc = x_prompt.reshape(NP, D)
    xl = x_sample.reshape(NS, D)
    cvec = jnp.concatenate([c, c_ctx[None, :], jnp.zeros((3, D), F32)], axis=0)
    mod = _adaln(cvec, ada_w, ada_b)

    bias = jnp.repeat(sgu_b[0].T, GW, axis=1)
    a, y, y1r, y2r = _inproj(xc, xl, mod, norm1_g[0][None, :], mix_w_in[0].astype(BF16), sgu_g[0].reshape(1, SGU_W),
                             sgu_w[0].astype(BF16), bias, tb)
    w_out = mix_w_out[0].astype(BF16)
    x1c = _mixout_ctx(y, a, xc, mod, w_out, tb)
    br, bi = _dft1(y1r.reshape(N_LAT_SEQ, DFT_N1, DFT_N2 * FNET_W), y2r.reshape(N_LAT_SEQ, DFT_N1, DFT_N2 * FNET_W), tb)
    x1l = _mixout_lat(br, bi, a.reshape(T // LAT_LEN, DFT_N2, DFT_N1, SGU_W),
                      xl.reshape(N_LAT_SEQ, DFT_N2, DFT_N1, D), mod, w_out).reshape(NS, D)
    x2 = _ffn(x1c, x1l, mod, norm2_g[0][None, :], ffn_w1[0].astype(BF16), ffn_w3[0].astype(BF16), ffn_w2[0].astype(BF16))

    qkv_c, qkv_l, kf, vf = _qkv(x2, mod, norm1_g[1][None, :], attn_w_qkv[0].astype(BF16), tb)
    lam_p = jnp.stack([lam_q1[0], lam_k1[0], lam_q2[0], lam_k2[0]], axis=0)
    sg = subln_g[0][None, :]
    oc = _attn_ctx(*qkv_c, lam_p, sg)
    cache_rows = (N_LAT_SEQ, CTX_LEN * N_HEADS, LANES)
    ol, w1b, w3b, w2b = _attn_lat(*qkv_l, cache_k.reshape(cache_rows), cache_v.reshape(cache_rows), lam_p, sg,
                                  moe_w1[0], moe_w3[0], moe_w2[0])

    w_r = jnp.pad(router_w[0], ((0, 0), (0, LANES - N_EXPERTS))).astype(BF16)
    x3, h2, route, cnt = _oproj_route(oc, ol, x2, mod, attn_w_o[0].astype(BF16), norm2_g[1][None, :], w_r, tb)
    unit_per_tile = T_MOE // SEG_ALIGN
    seg_len = cnt[:, 0, :N_EXPERTS].astype(I32)
    seg_local = jnp.cumsum(seg_len, axis=1) - seg_len
    group_len = jnp.sum(seg_len, axis=0)
    group_tiles = (group_len + unit_per_tile - 1) // unit_per_tile
    tile_ends = jnp.cumsum(group_tiles)
    group_start = (tile_ends - group_tiles) * unit_per_tile
    seg_global = group_start[None, :] + jnp.cumsum(seg_len, axis=0) - seg_len
    tab = jnp.concatenate([seg_len, seg_local, seg_global], axis=1)
    tab = jnp.pad(tab, ((0, 0), (0, LANES - 3 * N_EXPERTS)))
    fill = jnp.concatenate([group_start + group_len, group_tiles * unit_per_tile - group_len,
                            tile_ends[-1:] * unit_per_tile, N_MOE_TILES - tile_ends[-1:]])
    fill = jnp.pad(fill, (0, LANES - 2 * N_EXPERTS - 2)).reshape(1, LANES)
    tile_idx = jnp.arange(N_MOE_TILES, dtype=I32)
    tile_expert = jnp.sum((jnp.minimum(tile_idx, tile_ends[-1] - 1)[:, None] >= tile_ends[None, :]).astype(I32), axis=-1)
    of_expert = tile_expert[:, None] == jnp.arange(N_EXPERTS, dtype=I32)[None, :]
    rows_left = jnp.sum(jnp.where(of_expert, (group_len - (tile_idx[:, None] - (tile_ends - group_tiles)[None, :])
                                              * unit_per_tile)[:, :], 0), axis=-1)
    tile_valid = jnp.where(tile_idx >= tile_ends[-1], 0, jnp.where(rows_left <= unit_per_tile // 2, 2, 1)).astype(I32)

    xs = _dispatch(tab, fill, h2, route)
    ys = _moe(tile_expert, tile_valid, xs, w1b, w3b, w2b)
    fg = final_g[None, :]
    y_prompt = _combine(tab, ys, x3, route, mod, fg, NP, 0)
    y_sample = _combine(tab, ys, x3, route, mod, fg, NS, NP // TM)

    return (y_prompt.reshape(N_CTX_SEQ, CTX_LEN, D), y_sample.reshape(N_LAT_SEQ, LAT_LEN, D),
            kf.reshape(N_CTX_SEQ, 1, CTX_LEN, N_HEADS, 2 * HEAD_DIM), vf.reshape(N_CTX_SEQ, 1, CTX_LEN, N_HEADS, V_DIM))
```

```python
import functools
import math

import numpy as np
import jax
import jax.numpy as jnp
from jax import lax
from jax.experimental import pallas as pl
from jax.experimental.pallas import tpu as pltpu

F32 = jnp.float32
BF16 = jnp.bfloat16
I32 = jnp.int32

D = 1024
N_CTX_SEQ = 32
CTX_LEN = 256
N_LAT_SEQ = 4
LAT_LEN = 4096
NP = N_CTX_SEQ * CTX_LEN
NS = N_LAT_SEQ * LAT_LEN
T = NP + NS
GRID_W = 64
CHUNK = 128
SGU_W = 512
FNET_W = 512
GW = 128
N_HEADS = 8
HEAD_DIM = 64
V_DIM = 128
ROPE_THETA = 10000.0
D_FF = 2816
N_EXPERTS = 8
D_FF_EXPERT = 3584
EPS = 1e-6
LAM0 = 0.8 - 0.6 * math.exp(-0.3 * 1)
CTX_ROW = 4

LANES = 128
TM = 512
TQ = 1024
TK = 2048
Q_SCALE = HEAD_DIM ** -0.5 * math.log2(math.e)
T_MOE = 512
F_MOE = 1792
N_F_MOE = D_FF_EXPERT // F_MOE
assert N_F_MOE == 2
CAST_SPLIT = 4
CAST_COLS = D_FF_EXPERT // CAST_SPLIT
CAST_BLOCKS = N_EXPERTS * CAST_SPLIT
SEG_ALIGN = 8
SEG_BITS = (64, 32, 16, 8, 4, 2, 1)
L_TILE = 1152
N_ROWS = -(-(2 * T + (T // TM) * N_EXPERTS * (SEG_ALIGN - 1) + N_EXPERTS * (T_MOE - SEG_ALIGN)) // T_MOE) * T_MOE
N_MOE_TILES = N_ROWS // T_MOE
MAX_UNUSED_TILES = N_MOE_TILES - 2 * T // T_MOE
FFN_CHUNK = 2816
DFT_N1 = 256
DFT_N2 = 16
DFT_ROWS = 32
DFT1_GROUP = 4
CTX_PER_STEP = 2
ROUTE_SUB = 2
CTX_ATTN_ROWS = 1024
VMEM_LIMIT = 56 * 2 ** 20


def _params(sem, vmem=VMEM_LIMIT):
    return pltpu.CompilerParams(dimension_semantics=sem, vmem_limit_bytes=vmem)


def _resident(shape):
    nd = len(shape)
    return pl.BlockSpec(shape, lambda *_: (0,) * nd, pipeline_mode=pl.Buffered(1))


def _mod_row(i, tm):
    npt = NP // tm
    return jnp.where(i < npt, CTX_ROW, (i - npt) // (LAT_LEN // tm))


def _mod_spec(layer, tm, offset_tiles=0):
    return pl.BlockSpec((None, None, 6, D), lambda i, *_: (layer, _mod_row(i + offset_tiles, tm), 0, 0))


def _two_stream_specs(tm, width):
    npt = NP // tm
    ctx = pl.BlockSpec((tm, width), lambda i: (jnp.minimum(i, npt - 1), 0))
    lat = pl.BlockSpec((tm, width), lambda i: (jnp.maximum(i - npt, 0), 0))
    return ctx, lat


def _pick_stream(tm, ctx_ref, lat_ref):
    return jnp.where(pl.program_id(0) < NP // tm, ctx_ref[...], lat_ref[...])


def _modulate(x, g, scale, shift):
    ms = jnp.mean(x * x, axis=-1, keepdims=True)
    return x * lax.rsqrt(ms + EPS) * (g * (1.0 + scale)) + shift


def _silu(a):
    return a * jax.nn.sigmoid(a)


def _gelu_tanh(x):
    return 0.5 * x * (1.0 + jnp.tanh(0.7978845608028654 * (x + 0.044715 * (x * x * x))))


@functools.lru_cache(maxsize=None)
def _tables():
    def cs(n):
        k = np.arange(n, dtype=np.int64)
        ang = 2.0 * np.pi * ((k[:, None] * k[None, :]) % n) / n
        return np.cos(ang), np.sin(ang)

    c128, s128 = cs(GW)
    dft_ch = np.concatenate([c128, s128], axis=1)
    c256, s256 = cs(CTX_LEN)
    dft_ctx = np.concatenate([c256, -s256], axis=1)
    k1 = np.arange(DFT_N1, dtype=np.int64)[:, None]
    n2 = np.arange(DFT_N2, dtype=np.int64)[None, :]
    tw = 2.0 * np.pi * ((k1 * n2) % LAT_LEN) / LAT_LEN
    half = HEAD_DIM // 2
    inv_freq = ROPE_THETA ** (-np.arange(0, half, 2, dtype=np.float64) / half)
    t = np.arange(LAT_LEN)
    ang_r = (t // GRID_W).astype(np.float64)[:, None] * inv_freq[None, :]
    ang_c = (t % GRID_W).astype(np.float64)[:, None] * inv_freq[None, :]

    def blk(ang):
        c = np.cos(ang)
        s = np.sin(ang)
        return np.concatenate([c, c], axis=1), np.concatenate([-s, s], axis=1)

    cr, sr = blk(ang_r)
    cc, sc = blk(ang_c)
    cos64 = np.concatenate([cr, cc], axis=1)
    sin64 = np.concatenate([sr, sc], axis=1)
    rope_cos = np.concatenate([cos64, cos64], axis=1).astype(np.float32)
    rope_sin = np.concatenate([sin64, sin64], axis=1).astype(np.float32)
    tri = np.tril(np.ones((TM, TM), np.float32), k=-1)
    below = np.triu(np.ones((LANES, LANES), np.float32), k=1)
    f32 = lambda a: np.asarray(a, np.float32)
    return dict(dft_ch=f32(dft_ch), dft_ctx=f32(dft_ctx), c256=f32(c256), s256=f32(s256),
                tw_cos=f32(np.cos(tw)), tw_sin=f32(np.sin(tw)),
                rope_cos=rope_cos, rope_sin=rope_sin, tri=tri, below=below)


def _device_tables():
    tb = {k: jnp.asarray(v) for k, v in _tables().items()}
    for k in ("dft_ch", "dft_ctx", "c256", "s256", "tri", "below"):
        tb[k] = tb[k].astype(BF16)
    return tb


def _adaln_kernel(c_ref, w_ref, b_ref, o_ref):
    s = _silu(c_ref[...]).astype(BF16)
    o_ref[...] = jnp.dot(s, w_ref[...].astype(BF16), preferred_element_type=F32) + b_ref[...]


def _adaln(cvec, ada_w, ada_b):
    tn = 1536
    out = pl.pallas_call(
        _adaln_kernel,
        out_shape=jax.ShapeDtypeStruct((2, 8, 6 * D), F32),
        grid=(2, 6 * D // tn),
        in_specs=[
            pl.BlockSpec((8, D), lambda l, j: (0, 0)),
            pl.BlockSpec((None, D, tn), lambda l, j: (l, 0, j)),
            pl.BlockSpec((None, 1, tn), lambda l, j: (l, 0, j)),
        ],
        out_specs=pl.BlockSpec((None, 8, tn), lambda l, j: (l, 0, j)),
        compiler_params=_params(("arbitrary", "arbitrary")),
        name="adaln",
    )(cvec, ada_w, ada_b.reshape(2, 1, 6 * D))
    return out.reshape(2, 8, 6, D)


def _inproj_kernel(xc_ref, xl_ref, mod_ref, g_ref, win_ref, sgug_ref, sguw_ref, sgub_ref, dft_ref, a_ref, y_ref,
                   y1r_ref, y2r_ref, slab_s):
    is_ctx = pl.program_id(0) < NP // TM
    x = _pick_stream(TM, xc_ref, xl_ref)
    h = _modulate(x, g_ref[...], mod_ref[1:2, :], mod_ref[0:1, :]).astype(BF16)
    p = jnp.dot(h, win_ref[...], preferred_element_type=F32)
    act = _gelu_tanh(p[:, :2 * SGU_W])
    spectra = []
    for g in range(4):
        lo, hi = g * GW, (g + 1) * GW
        u = act[:, lo:hi]
        v = act[:, SGU_W + lo:SGU_W + hi]
        ms = jnp.mean(v * v, axis=-1, keepdims=True)
        vn = (v * lax.rsqrt(ms + EPS) * sgug_ref[:, lo:hi]).astype(BF16)
        w = sguw_ref[g]
        for c in range(TM // CHUNK):
            r0, r1 = c * CHUNK, (c + 1) * CHUNK
            mix = jnp.dot(w, vn[r0:r1, :], preferred_element_type=F32) + sgub_ref[:, lo:hi]
            a_ref[r0:r1, lo:hi] = (u[r0:r1, :] * mix).astype(BF16)
        fg = p[:, 2 * SGU_W + lo:2 * SGU_W + hi].astype(BF16)
        spectra.append(jnp.dot(fg, dft_ref[...], preferred_element_type=F32))

    @pl.when(is_ctx)
    def _():
        for g, yy in enumerate(spectra):
            y_ref[:, g * GW:(g + 1) * GW] = yy[:, :GW].astype(BF16)
            y_ref[:, FNET_W + g * GW:FNET_W + (g + 1) * GW] = yy[:, GW:].astype(BF16)

    @pl.when(jnp.logical_not(is_ctx))
    def _():
        for g, yy in enumerate(spectra):
            for half, dst in ((0, y1r_ref), (1, y2r_ref)):
                slab = slab_s.at[2 * g + half]
                slab[...] = yy[:, half * GW:(half + 1) * GW]
                for n2 in range(DFT_N2):
                    cols = slice(n2 * FNET_W + g * GW, n2 * FNET_W + (g + 1) * GW)
                    dst[:, cols] = slab[pl.ds(n2, TM // DFT_N2, stride=DFT_N2), :].astype(BF16)


def _inproj(x_ctx, x_lat, mod, g, w_in, sgu_g, sgu_w, sgu_b, tb):
    ctx, lat = _two_stream_specs(TM, D)
    npt = NP // TM
    rows_r = TM // DFT_N2
    yr_shape = jax.ShapeDtypeStruct((NS // DFT_N2, DFT_N2 * FNET_W), BF16)
    yr_spec = pl.BlockSpec((rows_r, DFT_N2 * FNET_W), lambda i: (jnp.maximum(i - npt, 0), 0))
    return pl.pallas_call(
        _inproj_kernel,
        out_shape=(jax.ShapeDtypeStruct((T, SGU_W), BF16), jax.ShapeDtypeStruct((NP, 2 * FNET_W), BF16),
                   yr_shape, yr_shape),
        grid=(T // TM,),
        in_specs=[
            ctx, lat,
            _mod_spec(0, TM),
            _resident((1, D)),
            _resident((D, 3 * SGU_W)),
            _resident((1, SGU_W)),
            _resident((4, CHUNK, CHUNK)),
            _resident((CHUNK, SGU_W)),
            _resident((GW, 2 * GW)),
        ],
        out_specs=(pl.BlockSpec((TM, SGU_W), lambda i: (i, 0)),
                   pl.BlockSpec((TM, 2 * FNET_W), lambda i: (jnp.minimum(i, npt - 1), 0)), yr_spec, yr_spec),
        scratch_shapes=[pltpu.VMEM((8, TM, LANES), F32)],
        compiler_params=_params(("arbitrary",)),
        name="inproj_mix",
    )(x_ctx, x_lat, mod, g, w_in, sgu_g, sgu_w, sgu_b, tb["dft_ch"])


def _mix_out(f, a_ref, x_ref, mod_ref, wout_ref, o_ref):
    mix = (jnp.dot(a_ref[...], wout_ref[:SGU_W, :], preferred_element_type=F32)
           + jnp.dot(f.astype(BF16), wout_ref[SGU_W:, :], preferred_element_type=F32))
    o_ref[...] = x_ref[...] + mod_ref[2:3, :] * mix


def _mixout_ctx_kernel(y_ref, a_ref, x_ref, mod_ref, dft_ref, wout_ref, o_ref):
    fs = []
    for s in range(CTX_PER_STEP):
        rows = slice(s * CTX_LEN, (s + 1) * CTX_LEN)
        fs.append(jnp.dot(dft_ref[:, :CTX_LEN], y_ref[rows, :FNET_W], preferred_element_type=F32)
                  + jnp.dot(dft_ref[:, CTX_LEN:], y_ref[rows, FNET_W:], preferred_element_type=F32))
    f = jnp.concatenate(fs, axis=0) * (1.0 / math.sqrt(CTX_LEN * GW))
    _mix_out(f, a_ref, x_ref, mod_ref, wout_ref, o_ref)


def _mixout_ctx(y, a, x, mod, w_out, tb):
    rows = CTX_PER_STEP * CTX_LEN
    tok = lambda w: pl.BlockSpec((rows, w), lambda i: (i, 0))
    return pl.pallas_call(
        _mixout_ctx_kernel,
        out_shape=jax.ShapeDtypeStruct((NP, D), F32),
        grid=(N_CTX_SEQ // CTX_PER_STEP,),
        in_specs=[tok(2 * FNET_W), tok(SGU_W), tok(D),
                  pl.BlockSpec((None, None, 6, D), lambda i: (0, CTX_ROW, 0, 0)),
                  _resident((CTX_LEN, 2 * CTX_LEN)), _resident((D, D))],
        out_specs=tok(D),
        compiler_params=_params(("arbitrary",)),
        name="mixout_ctx",
    )(y, a, x, mod, tb["dft_ctx"], w_out)


def _dft1_kernel(y1_ref, y2_ref, c_ref, s_ref, twc_ref, tws_ref, br_ref, bi_ref):
    y1 = y1_ref[...]
    y2 = y2_ref[...]
    c = c_ref[...]
    s = s_ref[...]
    ar = jnp.dot(c, y1, preferred_element_type=F32) - jnp.dot(s, y2, preferred_element_type=F32)
    ai = -(jnp.dot(c, y2, preferred_element_type=F32) + jnp.dot(s, y1, preferred_element_type=F32))
    lane = lax.broadcasted_iota(I32, (DFT_N1, DFT_N2), 1)
    for r in range(DFT1_GROUP):
        n2 = pl.program_id(1) * DFT1_GROUP + r
        cols = slice(r * FNET_W, (r + 1) * FNET_W)
        tc = jnp.sum(jnp.where(lane == n2, twc_ref[...], 0.0), axis=-1, keepdims=True)
        ts = jnp.sum(jnp.where(lane == n2, tws_ref[...], 0.0), axis=-1, keepdims=True)
        br_ref[r] = (tc * ar[:, cols] + ts * ai[:, cols]).astype(BF16)
        bi_ref[r] = (tc * ai[:, cols] - ts * ar[:, cols]).astype(BF16)


def _dft1(y1r, y2r, tb):
    shp = jax.ShapeDtypeStruct((N_LAT_SEQ, DFT_N2, DFT_N1, FNET_W), BF16)
    src = pl.BlockSpec((None, DFT_N1, DFT1_GROUP * FNET_W), lambda b, n: (b, 0, n))
    dst = pl.BlockSpec((None, DFT1_GROUP, DFT_N1, FNET_W), lambda b, n: (b, n, 0, 0))
    return pl.pallas_call(
        _dft1_kernel,
        out_shape=(shp, shp),
        grid=(N_LAT_SEQ, DFT_N2 // DFT1_GROUP),
        in_specs=[src, src, _resident((DFT_N1, DFT_N1)), _resident((DFT_N1, DFT_N1)),
                  _resident((DFT_N1, DFT_N2)), _resident((DFT_N1, DFT_N2))],
        out_specs=(dst, dst),
        compiler_params=_params(("arbitrary", "arbitrary")),
        name="dft_stage1",
    )(y1r, y2r, tb["c256"], tb["s256"], tb["tw_cos"], tb["tw_sin"])


def _cmul_const(z, wr, wi):
    re, im = z
    tol = 1e-12
    if abs(wi) < tol:
        return (re, im) if wr > 0 else (-re, -im)
    if abs(wr) < tol:
        return (-im, re) if wi > 0 else (im, -re)
    return (re * wr - im * wi, re * wi + im * wr)


def _fft(zs):
    n = len(zs)
    if n == 1:
        return zs
    even, odd = _fft(zs[0::2]), _fft(zs[1::2])
    out = [None] * n
    for k in range(n // 2):
        ang = 2.0 * math.pi * k / n
        t = _cmul_const(odd[k], math.cos(ang), -math.sin(ang))
        out[k] = (even[k][0] + t[0], even[k][1] + t[1])
        out[k + n // 2] = (even[k][0] - t[0], even[k][1] - t[1])
    return out


def _mixout_lat_kernel(br_ref, bi_ref, a_ref, x_ref, mod_ref, wout_ref, o_ref):
    spec = _fft([(br_ref[n2].astype(F32), bi_ref[n2].astype(F32)) for n2 in range(DFT_N2)])
    f = jnp.concatenate([re for re, _ in spec], axis=0) * (1.0 / math.sqrt(LAT_LEN * GW))
    rows = DFT_N2 * DFT_ROWS
    mix = (jnp.dot(a_ref[...].reshape(rows, SGU_W), wout_ref[:SGU_W, :], preferred_element_type=F32)
           + jnp.dot(f.astype(BF16), wout_ref[SGU_W:, :], preferred_element_type=F32))
    out = x_ref[...].reshape(rows, D) + mod_ref[2:3, :] * mix
    o_ref[...] = out.reshape(DFT_N2, DFT_ROWS, D)


def _mixout_lat(br, bi, a_lat, x, mod, w_out):
    blk = lambda w, off=0: pl.BlockSpec((None, DFT_N2, DFT_ROWS, w), lambda b, r: (b + off, 0, r, 0))
    return pl.pallas_call(
        _mixout_lat_kernel,
        out_shape=jax.ShapeDtypeStruct((N_LAT_SEQ, DFT_N2, DFT_N1, D), F32),
        grid=(N_LAT_SEQ, DFT_N1 // DFT_ROWS),
        in_specs=[blk(FNET_W), blk(FNET_W), blk(SGU_W, NP // LAT_LEN), blk(D),
                  pl.BlockSpec((None, None, 6, D), lambda b, r: (0, b, 0, 0)),
                  _resident((D, D))],
        out_specs=blk(D),
        compiler_params=_params(("arbitrary", "arbitrary")),
        name="mixout_lat",
    )(br, bi, a_lat, x, mod, w_out)


def _ffn_kernel(xc_ref, xl_ref, mod_ref, g_ref, w1_ref, w3_ref, w2_ref, o_ref):
    x = _pick_stream(TM, xc_ref, xl_ref)
    h = _modulate(x, g_ref[...], mod_ref[4:5, :], mod_ref[3:4, :]).astype(BF16)
    acc = None
    for c in range(D_FF // FFN_CHUNK):
        lo, hi = c * FFN_CHUNK, (c + 1) * FFN_CHUNK
        a = jnp.dot(h, w1_ref[:, lo:hi], preferred_element_type=F32)
        b = jnp.dot(h, w3_ref[:, lo:hi], preferred_element_type=F32)
        d = jnp.dot((_silu(a) * b).astype(BF16), w2_ref[lo:hi, :], preferred_element_type=F32)
        acc = d if acc is None else acc + d
    o_ref[...] = x + mod_ref[5:6, :] * acc


def _ffn(x_ctx, x_lat, mod, g, w1, w3, w2):
    ctx, lat = _two_stream_specs(TM, D)
    return pl.pallas_call(
        _ffn_kernel,
        out_shape=jax.ShapeDtypeStruct((T, D), F32),
        grid=(T // TM,),
        in_specs=[
            ctx, lat,
            _mod_spec(0, TM),
            _resident((1, D)),
            _resident((D, D_FF)),
            _resident((D, D_FF)),
            _resident((D_FF, D)),
        ],
        out_specs=pl.BlockSpec((TM, D), lambda i: (i, 0)),
        compiler_params=_params(("arbitrary",)),
        name="ffn",
    )(x_ctx, x_lat, mod, g, w1, w3, w2)


def _qkv_lat_kernel(x_ref, mod_ref, g_ref, w_ref, cos_ref, sin_ref, q_ref, k_ref, v_ref):
    h = _modulate(x_ref[...], g_ref[...], mod_ref[1:2, :], mod_ref[0:1, :]).astype(BF16)
    qkv = jnp.dot(h, w_ref[...], preferred_element_type=F32)
    cos = cos_ref[...]
    sin = sin_ref[...]
    first = (lax.broadcasted_iota(I32, (TM, LANES), 1) & 31) < 16

    def rotate(xh):
        partner = jnp.where(first, pltpu.roll(xh, LANES - 16, 1), pltpu.roll(xh, 16, 1))
        return xh * cos + partner * sin

    for hh in range(N_HEADS):
        sl = slice(hh * LANES, (hh + 1) * LANES)
        q_ref[:, sl] = (rotate(qkv[:, sl]) * Q_SCALE).astype(BF16)
        k_ref[:, sl] = rotate(qkv[:, D + hh * LANES:D + (hh + 1) * LANES]).astype(BF16)
    v_ref[...] = qkv[:, 2 * D:].astype(BF16)


def _qkv_lat(x, mod, g, w_qkv, tb):
    tok = pl.BlockSpec((TM, D), lambda i: (i, 0))
    lshape = jax.ShapeDtypeStruct((NS, D), BF16)
    off = NP // TM
    ltok = pl.BlockSpec((TM, D), lambda i: (i + off, 0))
    rope = pl.BlockSpec((TM, LANES), lambda i: (i % (LAT_LEN // TM), 0))
    return pl.pallas_call(
        _qkv_lat_kernel,
        out_shape=(lshape, lshape, lshape),
        grid=(NS // TM,),
        in_specs=[ltok, _mod_spec(1, TM, off), _resident((1, D)), _resident((D, 3 * D)), rope, rope],
        out_specs=(tok, tok, tok),
        compiler_params=_params(("arbitrary",)),
        name="qkv_lat",
    )(x, mod, g, w_qkv, tb["rope_cos"], tb["rope_sin"])


def _lam(lp_ref):
    lp = lp_ref[...]
    a = jnp.sum(lp[0:1, :] * lp[1:2, :], axis=-1, keepdims=True)
    b = jnp.sum(lp[2:3, :] * lp[3:4, :], axis=-1, keepdims=True)
    return jnp.exp(a) - jnp.exp(b) + LAM0


def _head_norm(o, sg):
    ms = jnp.mean(o * o, axis=-1, keepdims=True)
    return o * lax.rsqrt(ms + EPS) * (sg * (1.0 - LAM0))


_NT = (((1,), (1,)), ((), ()))


def _qkv_attn_ctx_kernel(x_ref, mod_ref, g_ref, w_ref, lp_ref, sg_ref, o_ref, kf_ref, vf_ref):
    lam = _lam(lp_ref)
    map0 = lax.broadcasted_iota(I32, (CTX_LEN, LANES), 1) < HEAD_DIM
    init = _softmax_init(CTX_LEN)
    for pair in range(CTX_ATTN_ROWS // TM):
        rows = slice(pair * TM, (pair + 1) * TM)
        h = _modulate(x_ref[rows, :], g_ref[...], mod_ref[1:2, :], mod_ref[0:1, :]).astype(BF16)
        qkv = jnp.dot(h, w_ref[...], preferred_element_type=F32)
        kf_ref[rows, :] = qkv[:, D:2 * D]
        vf_ref[rows, :] = qkv[:, 2 * D:]
        qb = (qkv[:, :D] * Q_SCALE).astype(BF16)
        kb = qkv[:, D:2 * D].astype(BF16)
        vb = qkv[:, 2 * D:].astype(BF16)
        for s in range(TM // CTX_LEN):
            seq = slice(s * CTX_LEN, (s + 1) * CTX_LEN)
            for hh in range(N_HEADS):
                sl = slice(hh * LANES, (hh + 1) * LANES)
                q = qb[seq, sl]
                zero = jnp.zeros_like(q)
                outs = []
                for qc in (jnp.where(map0, q, zero), jnp.where(map0, zero, q)):
                    _, l, acc = _softmax_step(qc, kb[seq, sl], vb[seq, sl], init)
                    outs.append(acc / jnp.sum(l, axis=-1, keepdims=True))
                o_ref[pair * TM + s * CTX_LEN:pair * TM + (s + 1) * CTX_LEN, sl] = _head_norm(
                    outs[0] - lam * outs[1], sg_ref[...]).astype(BF16)


def _qkv_attn_ctx(x, mod, g, w_qkv, lam_p, subln_g):
    tok = pl.BlockSpec((CTX_ATTN_ROWS, D), lambda i: (i, 0))
    return pl.pallas_call(
        _qkv_attn_ctx_kernel,
        out_shape=(jax.ShapeDtypeStruct((NP, D), BF16), jax.ShapeDtypeStruct((NP, D), F32),
                   jax.ShapeDtypeStruct((NP, D), F32)),
        grid=(NP // CTX_ATTN_ROWS,),
        in_specs=[tok, pl.BlockSpec((None, None, 6, D), lambda i: (1, CTX_ROW, 0, 0)), _resident((1, D)),
                  _resident((D, 3 * D)), _resident((4, HEAD_DIM)), _resident((1, V_DIM))],
        out_specs=(tok, tok, tok),
        compiler_params=_params(("arbitrary",)),
        name="qkv_attn_ctx",
    )(x, mod, g, w_qkv, lam_p, subln_g)


def _softmax_init(rows):
    return (jnp.full((rows, LANES), -jnp.inf, F32), jnp.zeros((rows, LANES), F32), jnp.zeros((rows, LANES), F32))


def _softmax_step(qc, kc, vc, state):
    m, l, acc = state
    s = lax.dot_general(qc, kc, _NT, preferred_element_type=F32)
    blocks = [s[:, j * LANES:(j + 1) * LANES] for j in range(s.shape[1] // LANES)]
    bm = functools.reduce(jnp.maximum, blocks)
    m_new = jnp.maximum(m, jnp.max(bm, axis=-1, keepdims=True))
    alpha = jnp.exp2(m - m_new)
    ps = [jnp.exp2(b - m_new) for b in blocks]
    l_new = alpha * l + functools.reduce(jnp.add, ps)
    p = jnp.concatenate(ps, axis=1).astype(BF16)
    acc_new = alpha * acc + jnp.dot(p, vc, preferred_element_type=F32)
    return m_new, l_new, acc_new


def _attn_lat_kernel(q_ref, k_ref, v_ref, ck_ref, cv_ref, lp_ref, sg_ref, w1_ref, w3_ref, w2_ref,
                     o_ref, w1b_ref, w3b_ref, w2b_ref):
    step = (pl.program_id(0) * N_HEADS + pl.program_id(1)) * (LAT_LEN // TQ) + pl.program_id(2)
    for phase, (src, dst) in enumerate(((w1_ref, w1b_ref), (w3_ref, w3b_ref), (w2_ref, w2b_ref))):
        @pl.when((step >= phase * CAST_BLOCKS) & (step < (phase + 1) * CAST_BLOCKS))
        def _(src=src, dst=dst):
            dst[...] = src[...].astype(BF16)

    q = q_ref[...]
    zero = jnp.zeros_like(q)
    map0 = lax.broadcasted_iota(I32, (TQ, LANES), 1) < HEAD_DIM
    qs = (jnp.where(map0, q, zero), jnp.where(map0, zero, q))
    init = _softmax_init(TQ)
    states = [init, init]
    head_rows = pl.ds(pl.program_id(1), CTX_LEN, stride=N_HEADS)
    chunks = [(ck_ref[head_rows, :].astype(BF16), cv_ref[head_rows, :].astype(BF16))]
    chunks += [(k_ref[j * TK:(j + 1) * TK, :], v_ref[j * TK:(j + 1) * TK, :]) for j in range(LAT_LEN // TK)]
    for kc, vc in chunks:
        states = [_softmax_step(qs[c], kc, vc, states[c]) for c in range(2)]
    outs = [acc / jnp.sum(l, axis=-1, keepdims=True) for _, l, acc in states]
    o = outs[0] - _lam(lp_ref) * outs[1]
    o_ref[...] = _head_norm(o, sg_ref[...]).astype(BF16)


def _attn_lat(q, k, v, cache_k, cache_v, lam_p, subln_g, moe_w1, moe_w3, moe_w2):
    nq = LAT_LEN // TQ
    assert N_LAT_SEQ * N_HEADS * nq >= 3 * CAST_BLOCKS
    qspec = pl.BlockSpec((TQ, LANES), lambda b, h, i: (b * nq + i, h))
    kspec = pl.BlockSpec((LAT_LEN, LANES), lambda b, h, i: (b, h))
    cspec = pl.BlockSpec((None, CTX_LEN * N_HEADS, LANES), lambda b, h, i: (b, 0, 0))

    def cast_block(phase, b, h, i):
        j = jnp.clip((b * N_HEADS + h) * nq + i - phase * CAST_BLOCKS, 0, CAST_BLOCKS - 1)
        return j // CAST_SPLIT, j % CAST_SPLIT

    def up_spec(phase):
        return pl.BlockSpec((None, D, CAST_COLS), lambda b, h, i: (cast_block(phase, b, h, i)[0], 0,
                                                                    cast_block(phase, b, h, i)[1]))

    down_spec = pl.BlockSpec((None, CAST_COLS, D), lambda b, h, i: (*cast_block(2, b, h, i), 0))
    up_shape = jax.ShapeDtypeStruct((N_EXPERTS, D, D_FF_EXPERT), BF16)
    return pl.pallas_call(
        _attn_lat_kernel,
        out_shape=(jax.ShapeDtypeStruct((NS, D), BF16), up_shape, up_shape,
                   jax.ShapeDtypeStruct((N_EXPERTS, D_FF_EXPERT, D), BF16)),
        grid=(N_LAT_SEQ, N_HEADS, nq),
        in_specs=[qspec, kspec, kspec, cspec, cspec, _resident((4, HEAD_DIM)), _resident((1, V_DIM)),
                  up_spec(0), up_spec(1), down_spec],
        out_specs=(qspec, up_spec(0), up_spec(1), down_spec),
        compiler_params=_params(("arbitrary", "arbitrary", "arbitrary")),
        name="attn_lat",
    )(q, k, v, cache_k, cache_v, lam_p, subln_g, moe_w1, moe_w3, moe_w2)


def _oproj_route_kernel(oc_ref, ol_ref, x_ref, mod_ref, wo_ref, g_ref, wr_ref, tri_ref, below_ref, x3_ref, h_ref,
                        route_ref, cnt_ref):
    is_ctx = pl.program_id(0) < NP // (ROUTE_SUB * TM)
    for sub in range(ROUTE_SUB):
        rows = slice(sub * TM, (sub + 1) * TM)
        o = jnp.where(is_ctx, oc_ref[rows, :], ol_ref[rows, :])
        _route_tile(o, x_ref[rows, :], mod_ref, wo_ref, g_ref, wr_ref, tri_ref, below_ref,
                    x3_ref.at[rows, :], h_ref.at[rows, :], route_ref.at[rows, :], cnt_ref.at[sub])


def _route_tile(o, x, mod_ref, wo_ref, g_ref, wr_ref, tri_ref, below_ref, x3_ref, h_ref, route_ref, cnt_ref):
    x3 = x + mod_ref[2:3, :] * jnp.dot(o, wo_ref[...], preferred_element_type=F32)
    x3_ref[...] = x3
    hb = _modulate(x3, g_ref[...], mod_ref[4:5, :], mod_ref[3:4, :]).astype(BF16)
    h_ref[...] = hb

    logits = jnp.dot(hb, wr_ref[...], preferred_element_type=F32)
    lane = lax.broadcasted_iota(I32, (TM, LANES), 1).astype(F32)
    neg = jnp.float32(-jnp.inf)
    lg = jnp.where(lane < N_EXPERTS, logits, neg)
    v1 = jnp.max(lg, axis=-1, keepdims=True)
    i1 = jnp.min(jnp.where(lg == v1, lane, float(LANES)), axis=-1, keepdims=True)
    lg2 = jnp.where(lane == i1, neg, lg)
    v2 = jnp.max(lg2, axis=-1, keepdims=True)
    i2 = jnp.min(jnp.where(lg2 == v2, lane, float(LANES)), axis=-1, keepdims=True)
    e = jnp.exp(v2 - v1)
    g1 = 1.0 / (1.0 + e)
    g2 = e / (1.0 + e)
    sel1 = lane == i1
    sel2 = lane == i2
    onehot = jnp.where(sel1 | sel2, 1.0, 0.0)
    rank = jnp.dot(tri_ref[...], onehot.astype(BF16), preferred_element_type=F32)
    cnt8 = jnp.floor((jnp.sum(onehot, axis=0, keepdims=True) + (SEG_ALIGN - 1)) * (1.0 / SEG_ALIGN))
    base8 = jnp.dot(jnp.broadcast_to(cnt8, (8, LANES)).astype(BF16), below_ref[...], preferred_element_type=F32)
    slot = base8[0:1, :] * float(SEG_ALIGN) + rank
    pos1 = jnp.sum(jnp.where(sel1, slot, 0.0), axis=-1, keepdims=True)
    pos2 = jnp.sum(jnp.where(sel2, slot, 0.0), axis=-1, keepdims=True)
    cnt_ref[...] = cnt8
    route = jnp.where(lane == 0, i1, 0.0)
    route = jnp.where(lane == 1, i2, route)
    route = jnp.where(lane == 2, g1, route)
    route = jnp.where(lane == 3, g2, route)
    route = jnp.where(lane == 4, pos1, route)
    route = jnp.where(lane == 5, pos2, route)
    route_ref[...] = route


def _oproj_route(o_ctx, o_lat, x, mod, w_o, g, w_r, tb):
    rows = ROUTE_SUB * TM
    tok = pl.BlockSpec((rows, D), lambda i: (i, 0))
    ctx, lat = _two_stream_specs(rows, D)
    return pl.pallas_call(
        _oproj_route_kernel,
        out_shape=(jax.ShapeDtypeStruct((T, D), F32), jax.ShapeDtypeStruct((T, D), BF16),
                   jax.ShapeDtypeStruct((T, LANES), F32), jax.ShapeDtypeStruct((T // TM, 1, LANES), F32)),
        grid=(T // rows,),
        in_specs=[ctx, lat, tok, _mod_spec(1, rows), _resident((D, D)), _resident((1, D)), _resident((D, LANES)),
                  _resident((TM, TM)), _resident((LANES, LANES))],
        out_specs=(tok, tok, pl.BlockSpec((rows, LANES), lambda i: (i, 0)),
                   pl.BlockSpec((ROUTE_SUB, 1, LANES), lambda i: (i, 0, 0))),
        compiler_params=_params(("arbitrary",)),
        name="oproj_route",
    )(o_ctx, o_lat, x, mod, w_o, g, w_r, tb["tri"], tb["below"])


def _bit_copies(n, src0, dst0, bits, make_copy, op):
    for bit in bits:
        done = n & (-2 * bit)

        @pl.when((n & bit) != 0)
        def _(done=done, bit=bit):
            src = pl.multiple_of((src0 + done) * SEG_ALIGN, SEG_ALIGN)
            dst = pl.multiple_of((dst0 + done) * SEG_ALIGN, SEG_ALIGN)
            op(make_copy(src, dst, bit * SEG_ALIGN))


def _segment_copies(tab, make_copy, op):
    for e in range(N_EXPERTS):
        _bit_copies(tab(e), tab(N_EXPERTS + e), tab(2 * N_EXPERTS + e), SEG_BITS, make_copy, op)


def _start(copy):
    copy.start()


def _wait(copy):
    copy.wait()


def _dispatch_kernel(tab_ref, fill_ref, h_ref, route_ref, xs_ref, sorted_s, zero_s, sem, fill_sem):
    i = pl.program_id(0)
    slot = i % 2
    slots = route_ref[...].T
    row = lax.broadcasted_iota(I32, (L_TILE, TM), 0).astype(F32)
    perm = jnp.where((row == slots[4:5, :]) | (row == slots[5:6, :]), 1.0, 0.0).astype(BF16)
    sorted_s[slot] = jnp.dot(perm, h_ref[...], preferred_element_type=F32)

    def copies(tile, s, op):
        def seg(src, dst, rows):
            return pltpu.make_async_copy(sorted_s.at[s, pl.ds(src, rows), :], xs_ref.at[pl.ds(dst, rows), :], sem.at[s])

        _segment_copies(lambda j: tab_ref[tile, j], seg, op)

    copies(i, slot, _start)

    @pl.when(i > 0)
    def _():
        copies(i - 1, 1 - slot, _wait)

    @pl.when(i == pl.num_programs(0) - 1)
    def _():
        copies(i, slot, _wait)

    @pl.when(i == 0)
    def _():
        zero_s[...] = jnp.zeros(zero_s.shape, F32)

        def fill(src, dst, rows):
            del src
            return pltpu.make_async_copy(zero_s.at[pl.ds(0, rows), :], xs_ref.at[pl.ds(dst, rows), :], fill_sem)

        def fills(op):
            for e in range(N_EXPERTS):
                _bit_copies(fill_ref[0, N_EXPERTS + e], 0, fill_ref[0, e], SEG_BITS[1:], fill, op)
            for k in range(MAX_UNUSED_TILES):
                @pl.when(k < fill_ref[0, 2 * N_EXPERTS + 1])
                def _(k=k):
                    dst = pl.multiple_of(fill_ref[0, 2 * N_EXPERTS] * SEG_ALIGN + k * T_MOE, SEG_ALIGN)
                    op(fill(0, dst, T_MOE))

        fills(_start)
        fills(_wait)


def _dispatch(tab, fill, h, route):
    smem = pl.BlockSpec(memory_space=pltpu.SMEM)
    return pl.pallas_call(
        _dispatch_kernel,
        out_shape=jax.ShapeDtypeStruct((N_ROWS, D), F32),
        grid=(T // TM,),
        in_specs=[smem, smem, pl.BlockSpec((TM, D), lambda i: (i, 0)), pl.BlockSpec((TM, LANES), lambda i: (i, 0))],
        out_specs=pl.BlockSpec(memory_space=pl.ANY),
        scratch_shapes=[pltpu.VMEM((2, L_TILE, D), F32), pltpu.VMEM((T_MOE, D), F32),
                        pltpu.SemaphoreType.DMA((2,)), pltpu.SemaphoreType.DMA(())],
        compiler_params=_params(("arbitrary",)),
        name="moe_dispatch",
    )(tab, fill, h, route)


def _moe_kernel(te_ref, tv_ref, xs_ref, w1_ref, w3_ref, w2_ref, y_ref, xb_s, acc_s):
    del te_ref
    i = pl.program_id(0)
    f = pl.program_id(1)
    valid = tv_ref[i] == 1

    def chunk(x):
        a = jnp.dot(x, w1_ref[...], preferred_element_type=F32)
        b = jnp.dot(x, w3_ref[...], preferred_element_type=F32)
        return jnp.dot((_silu(a) * b).astype(BF16), w2_ref[...], preferred_element_type=F32)

    @pl.when(valid & (f == 0))
    def _():
        x = xs_ref[...].astype(BF16)
        xb_s[...] = x
        acc_s[...] = chunk(x)

    @pl.when(valid & (f == N_F_MOE - 1))
    def _():
        y_ref[...] = acc_s[...] + chunk(xb_s[...])

    @pl.when(jnp.logical_not(valid) & (f == N_F_MOE - 1))
    def _():
        y_ref[...] = jnp.zeros(y_ref.shape, F32)


def _moe(tile_expert, tile_valid, xs, w1, w3, w2):
    def fidx(i, f, te, tv):
        return jnp.where(tv[i] == 1, f, N_F_MOE - 1)

    grid_spec = pltpu.PrefetchScalarGridSpec(
        num_scalar_prefetch=2,
        grid=(N_MOE_TILES, N_F_MOE),
        in_specs=[
            pl.BlockSpec((T_MOE, D), lambda i, f, te, tv: (i, 0)),
            pl.BlockSpec((None, D, F_MOE), lambda i, f, te, tv: (te[i], 0, fidx(i, f, te, tv))),
            pl.BlockSpec((None, D, F_MOE), lambda i, f, te, tv: (te[i], 0, fidx(i, f, te, tv))),
            pl.BlockSpec((None, F_MOE, D), lambda i, f, te, tv: (te[i], fidx(i, f, te, tv), 0)),
        ],
        out_specs=pl.BlockSpec((T_MOE, D), lambda i, f, te, tv: (i, 0)),
        scratch_shapes=[pltpu.VMEM((T_MOE, D), BF16), pltpu.VMEM((T_MOE, D), F32)],
    )
    return pl.pallas_call(
        _moe_kernel,
        out_shape=jax.ShapeDtypeStruct((N_ROWS, D), F32),
        grid_spec=grid_spec,
        compiler_params=_params(("arbitrary", "arbitrary")),
        name="moe_experts",
    )(tile_expert, tile_valid, xs, w1, w3, w2)


def _combine_kernel(tab_ref, y_ref, x_ref, route_ref, mod_ref, fg_ref, o_ref, buf, sem, *, first_tile):
    i = pl.program_id(0)
    slot = i % 2

    def fetch(tile, s, op):
        def seg(src, dst, rows):
            return pltpu.make_async_copy(y_ref.at[pl.ds(dst, rows), :], buf.at[s, pl.ds(src, rows), :], sem.at[s])

        _segment_copies(lambda j: tab_ref[tile, j], seg, op)

    @pl.when(i == 0)
    def _():
        buf[...] = jnp.zeros(buf.shape, F32)
        fetch(first_tile, 0, _start)

    @pl.when(i + 1 < pl.num_programs(0))
    def _():
        fetch(first_tile + i + 1, 1 - slot, _start)

    fetch(first_tile + i, slot, _wait)
    yb = buf[slot].astype(BF16)
    route = route_ref[...]
    col = lax.broadcasted_iota(I32, (TM, L_TILE), 1).astype(F32)
    picks = [jnp.dot(jnp.where(col == route[:, c:c + 1], 1.0, 0.0).astype(BF16), yb, preferred_element_type=F32)
             for c in (4, 5)]
    moe = route[:, 2:3] * picks[0] + route[:, 3:4] * picks[1]
    x = x_ref[...] + mod_ref[5:6, :] * moe
    ms = jnp.mean(x * x, axis=-1, keepdims=True)
    o_ref[...] = x * lax.rsqrt(ms + EPS) * fg_ref[...]


def _combine(tab, y, x, route, mod, final_g, n_tok, off_tiles):
    tok = lambda w: pl.BlockSpec((TM, w), lambda i: (i + off_tiles, 0))
    return pl.pallas_call(
        functools.partial(_combine_kernel, first_tile=off_tiles),
        out_shape=jax.ShapeDtypeStruct((n_tok, D), F32),
        grid=(n_tok // TM,),
        in_specs=[
            pl.BlockSpec(memory_space=pltpu.SMEM),
            pl.BlockSpec(memory_space=pl.ANY),
            tok(D), tok(LANES), _mod_spec(1, TM, off_tiles), _resident((1, D)),
        ],
        out_specs=pl.BlockSpec((TM, D), lambda i: (i, 0)),
        scratch_shapes=[pltpu.VMEM((2, L_TILE, D), F32), pltpu.SemaphoreType.DMA((2,))],
        compiler_params=_params(("arbitrary",)),
        name="moe_combine",
    )(tab, y, x, route, mod, final_g)


def kernel(x_prompt, x_sample, cache_k, cache_v, c, c_ctx, ada_w, ada_b, norm1_g, norm2_g, final_g, mix_w_in, sgu_g, sgu_w, sgu_b, mix_w_out, ffn_w1, ffn_w3, ffn_w2, attn_w_qkv, lam_q1, lam_k1, lam_q2, lam_k2, subln_g, attn_w_o, router_w, moe_w1, moe_w3, moe_w2):
    tb = _device_tables()
    xc = x_prompt.reshape(NP, D)
    xl = x_sample.reshape(NS, D)
    cvec = jnp.concatenate([c, c_ctx[None, :], jnp.zeros((3, D), F32)], axis=0)
    mod = _adaln(cvec, ada_w, ada_b)

    bias = jnp.repeat(sgu_b[0].T, GW, axis=1)
    a, y, y1r, y2r = _inproj(xc, xl, mod, norm1_g[0][None, :], mix_w_in[0].astype(BF16), sgu_g[0].reshape(1, SGU_W),
                             sgu_w[0].astype(BF16), bias, tb)
    w_out = mix_w_out[0].astype(BF16)
    x1c = _mixout_ctx(y, a, xc, mod, w_out, tb)
    br, bi = _dft1(y1r.reshape(N_LAT_SEQ, DFT_N1, DFT_N2 * FNET_W), y2r.reshape(N_LAT_SEQ, DFT_N1, DFT_N2 * FNET_W), tb)
    x1l = _mixout_lat(br, bi, a.reshape(T // LAT_LEN, DFT_N2, DFT_N1, SGU_W),
                      xl.reshape(N_LAT_SEQ, DFT_N2, DFT_N1, D), mod, w_out).reshape(NS, D)
    x2 = _ffn(x1c, x1l, mod, norm2_g[0][None, :], ffn_w1[0].astype(BF16), ffn_w3[0].astype(BF16), ffn_w2[0].astype(BF16))

    w_qkv = attn_w_qkv[0].astype(BF16)
    lam_p = jnp.stack([lam_q1[0], lam_k1[0], lam_q2[0], lam_k2[0]], axis=0)
    sg = subln_g[0][None, :]
    oc, kf, vf = _qkv_attn_ctx(x2, mod, norm1_g[1][None, :], w_qkv, lam_p, sg)
    qkv_l = _qkv_lat(x2, mod, norm1_g[1][None, :], w_qkv, tb)
    cache_rows = (N_LAT_SEQ, CTX_LEN * N_HEADS, LANES)
    ol, w1b, w3b, w2b = _attn_lat(*qkv_l, cache_k.reshape(cache_rows), cache_v.reshape(cache_rows), lam_p, sg,
                                  moe_w1[0], moe_w3[0], moe_w2[0])

    w_r = jnp.pad(router_w[0], ((0, 0), (0, LANES - N_EXPERTS))).astype(BF16)
    x3, h2, route, cnt = _oproj_route(oc, ol, x2, mod, attn_w_o[0].astype(BF16), norm2_g[1][None, :], w_r, tb)
    unit_per_tile = T_MOE // SEG_ALIGN
    seg_len = cnt[:, 0, :N_EXPERTS].astype(I32)
    seg_local = jnp.cumsum(seg_len, axis=1) - seg_len
    group_len = jnp.sum(seg_len, axis=0)
    group_tiles = (group_len + unit_per_tile - 1) // unit_per_tile
    tile_ends = jnp.cumsum(group_tiles)
    group_start = (tile_ends - group_tiles) * unit_per_tile
    seg_global = group_start[None, :] + jnp.cumsum(seg_len, axis=0) - seg_len
    tab = jnp.concatenate([seg_len, seg_local, seg_global], axis=1)
    tab = jnp.pad(tab, ((0, 0), (0, LANES - 3 * N_EXPERTS)))
    fill = jnp.concatenate([group_start + group_len, group_tiles * unit_per_tile - group_len,
                            tile_ends[-1:] * unit_per_tile, N_MOE_TILES - tile_ends[-1:]])
    fill = jnp.pad(fill, (0, LANES - 2 * N_EXPERTS - 2)).reshape(1, LANES)
    tile_idx = jnp.arange(N_MOE_TILES, dtype=I32)
    tile_valid = (tile_idx < tile_ends[-1]).astype(I32)
    tile_expert = jnp.sum((jnp.minimum(tile_idx, tile_ends[-1] - 1)[:, None] >= tile_ends[None, :]).astype(I32), axis=-1)

    xs = _dispatch(tab, fill, h2, route)
    ys = _moe(tile_expert, tile_valid, xs, w1b, w3b, w2b)
    fg = final_g[None, :]
    y_prompt = _combine(tab, ys, x3, route, mod, fg, NP, 0)
    y_sample = _combine(tab, ys, x3, route, mod, fg, NS, NP // TM)

    return (y_prompt.reshape(N_CTX_SEQ, CTX_LEN, D), y_sample.reshape(N_LAT_SEQ, LAT_LEN, D),
            kf.reshape(N_CTX_SEQ, 1, CTX_LEN, N_HEADS, 2 * HEAD_DIM), vf.reshape(N_CTX_SEQ, 1, CTX_LEN, N_HEADS, V_DIM))
```

```python
import functools
import math

import numpy as np
import jax
import jax.numpy as jnp
from jax import lax
from jax.experimental import pallas as pl
from jax.experimental.pallas import tpu as pltpu

F32 = jnp.float32
BF16 = jnp.bfloat16
I32 = jnp.int32

D = 1024
N_CTX_SEQ = 32
CTX_LEN = 256
N_LAT_SEQ = 4
LAT_LEN = 4096
NP = N_CTX_SEQ * CTX_LEN
NS = N_LAT_SEQ * LAT_LEN
T = NP + NS
GRID_W = 64
CHUNK = 128
SGU_W = 512
FNET_W = 512
GW = 128
N_HEADS = 8
HEAD_DIM = 64
V_DIM = 128
ROPE_THETA = 10000.0
D_FF = 2816
N_EXPERTS = 8
D_FF_EXPERT = 3584
EPS = 1e-6
LAM0 = 0.8 - 0.6 * math.exp(-0.3 * 1)
CTX_ROW = 4

LANES = 128
TM = 512
TQ = 1024
TK = 2048
Q_SCALE = HEAD_DIM ** -0.5 * math.log2(math.e)
T_MOE = 512
F_MOE = 1792
N_F_MOE = D_FF_EXPERT // F_MOE
assert N_F_MOE == 2
CAST_SPLIT = 4
CAST_COLS = D_FF_EXPERT // CAST_SPLIT
CAST_BLOCKS = N_EXPERTS * CAST_SPLIT
SEG_ALIGN = 8
SEG_BITS = (64, 32, 16, 8, 4, 2, 1)
L_TILE = 1152
N_ROWS = -(-(2 * T + (T // TM) * N_EXPERTS * (SEG_ALIGN - 1) + N_EXPERTS * (T_MOE - SEG_ALIGN)) // T_MOE) * T_MOE
N_MOE_TILES = N_ROWS // T_MOE
MAX_UNUSED_TILES = N_MOE_TILES - 2 * T // T_MOE
FFN_CHUNK = 2816
DFT_N1 = 256
DFT_N2 = 16
DFT_ROWS = 32
DFT1_GROUP = 4
CTX_PER_STEP = 2
ROUTE_SUB = 2
CTX_ATTN_ROWS = 1024
QKV_LAT_ROWS = 1024
VMEM_LIMIT = 56 * 2 ** 20


def _params(sem, vmem=VMEM_LIMIT):
    return pltpu.CompilerParams(dimension_semantics=sem, vmem_limit_bytes=vmem)


def _resident(shape):
    nd = len(shape)
    return pl.BlockSpec(shape, lambda *_: (0,) * nd, pipeline_mode=pl.Buffered(1))


def _mod_row(i, tm):
    npt = NP // tm
    return jnp.where(i < npt, CTX_ROW, (i - npt) // (LAT_LEN // tm))


def _mod_spec(layer, tm, offset_tiles=0):
    return pl.BlockSpec((None, None, 6, D), lambda i, *_: (layer, _mod_row(i + offset_tiles, tm), 0, 0))


def _two_stream_specs(tm, width):
    npt = NP // tm
    ctx = pl.BlockSpec((tm, width), lambda i: (jnp.minimum(i, npt - 1), 0))
    lat = pl.BlockSpec((tm, width), lambda i: (jnp.maximum(i - npt, 0), 0))
    return ctx, lat


def _pick_stream(tm, ctx_ref, lat_ref):
    return jnp.where(pl.program_id(0) < NP // tm, ctx_ref[...], lat_ref[...])


def _modulate(x, g, scale, shift):
    ms = jnp.mean(x * x, axis=-1, keepdims=True)
    return x * lax.rsqrt(ms + EPS) * (g * (1.0 + scale)) + shift


def _silu(a):
    return a * jax.nn.sigmoid(a)


def _gelu_tanh(x):
    return 0.5 * x * (1.0 + jnp.tanh(0.7978845608028654 * (x + 0.044715 * (x * x * x))))


@functools.lru_cache(maxsize=None)
def _tables():
    def cs(n):
        k = np.arange(n, dtype=np.int64)
        ang = 2.0 * np.pi * ((k[:, None] * k[None, :]) % n) / n
        return np.cos(ang), np.sin(ang)

    c128, s128 = cs(GW)
    dft_ch = np.concatenate([c128, s128], axis=1)
    c256, s256 = cs(CTX_LEN)
    dft_ctx = np.concatenate([c256, -s256], axis=1)
    k1 = np.arange(DFT_N1, dtype=np.int64)[:, None]
    n2 = np.arange(DFT_N2, dtype=np.int64)[None, :]
    tw = 2.0 * np.pi * ((k1 * n2) % LAT_LEN) / LAT_LEN
    half = HEAD_DIM // 2
    inv_freq = ROPE_THETA ** (-np.arange(0, half, 2, dtype=np.float64) / half)
    t = np.arange(LAT_LEN)
    ang_r = (t // GRID_W).astype(np.float64)[:, None] * inv_freq[None, :]
    ang_c = (t % GRID_W).astype(np.float64)[:, None] * inv_freq[None, :]

    def blk(ang):
        c = np.cos(ang)
        s = np.sin(ang)
        return np.concatenate([c, c], axis=1), np.concatenate([-s, s], axis=1)

    cr, sr = blk(ang_r)
    cc, sc = blk(ang_c)
    cos64 = np.concatenate([cr, cc], axis=1)
    sin64 = np.concatenate([sr, sc], axis=1)
    rope_cos = np.concatenate([cos64, cos64], axis=1).astype(np.float32)
    rope_sin = np.concatenate([sin64, sin64], axis=1).astype(np.float32)
    tri = np.tril(np.ones((TM, TM), np.float32), k=-1)
    below = np.triu(np.ones((LANES, LANES), np.float32), k=1)
    f32 = lambda a: np.asarray(a, np.float32)
    return dict(dft_ch=f32(dft_ch), dft_ctx=f32(dft_ctx), c256=f32(c256), s256=f32(s256),
                tw_cos=f32(np.cos(tw)), tw_sin=f32(np.sin(tw)),
                rope_cos=rope_cos, rope_sin=rope_sin, tri=tri, below=below)


def _device_tables():
    tb = {k: jnp.asarray(v) for k, v in _tables().items()}
    for k in ("dft_ch", "dft_ctx", "c256", "s256", "tri", "below"):
        tb[k] = tb[k].astype(BF16)
    return tb


def _adaln_kernel(c_ref, w_ref, b_ref, o_ref):
    s = _silu(c_ref[...]).astype(BF16)
    o_ref[...] = jnp.dot(s, w_ref[...].astype(BF16), preferred_element_type=F32) + b_ref[...]


def _adaln(cvec, ada_w, ada_b):
    tn = 1536
    out = pl.pallas_call(
        _adaln_kernel,
        out_shape=jax.ShapeDtypeStruct((2, 8, 6 * D), F32),
        grid=(2, 6 * D // tn),
        in_specs=[
            pl.BlockSpec((8, D), lambda l, j: (0, 0)),
            pl.BlockSpec((None, D, tn), lambda l, j: (l, 0, j)),
            pl.BlockSpec((None, 1, tn), lambda l, j: (l, 0, j)),
        ],
        out_specs=pl.BlockSpec((None, 8, tn), lambda l, j: (l, 0, j)),
        compiler_params=_params(("arbitrary", "arbitrary")),
        name="adaln",
    )(cvec, ada_w, ada_b.reshape(2, 1, 6 * D))
    return out.reshape(2, 8, 6, D)


def _inproj_kernel(xc_ref, xl_ref, mod_ref, g_ref, win_ref, sgug_ref, sguw_ref, sgub_ref, dft_ref, a_ref, y_ref,
                   y1r_ref, y2r_ref, slab_s):
    is_ctx = pl.program_id(0) < NP // TM
    x = _pick_stream(TM, xc_ref, xl_ref)
    h = _modulate(x, g_ref[...], mod_ref[1:2, :], mod_ref[0:1, :]).astype(BF16)
    p = jnp.dot(h, win_ref[...], preferred_element_type=F32)
    act = _gelu_tanh(p[:, :2 * SGU_W])
    spectra = []
    for g in range(4):
        lo, hi = g * GW, (g + 1) * GW
        u = act[:, lo:hi]
        v = act[:, SGU_W + lo:SGU_W + hi]
        ms = jnp.mean(v * v, axis=-1, keepdims=True)
        vn = (v * lax.rsqrt(ms + EPS) * sgug_ref[:, lo:hi]).astype(BF16)
        w = sguw_ref[g]
        for c in range(TM // CHUNK):
            r0, r1 = c * CHUNK, (c + 1) * CHUNK
            mix = jnp.dot(w, vn[r0:r1, :], preferred_element_type=F32) + sgub_ref[:, lo:hi]
            a_ref[r0:r1, lo:hi] = (u[r0:r1, :] * mix).astype(BF16)
        fg = p[:, 2 * SGU_W + lo:2 * SGU_W + hi].astype(BF16)
        spectra.append(jnp.dot(fg, dft_ref[...], preferred_element_type=F32))

    @pl.when(is_ctx)
    def _():
        for g, yy in enumerate(spectra):
            y_ref[:, g * GW:(g + 1) * GW] = yy[:, :GW].astype(BF16)
            y_ref[:, FNET_W + g * GW:FNET_W + (g + 1) * GW] = yy[:, GW:].astype(BF16)

    @pl.when(jnp.logical_not(is_ctx))
    def _():
        for g, yy in enumerate(spectra):
            for half, dst in ((0, y1r_ref), (1, y2r_ref)):
                slab = slab_s.at[2 * g + half]
                slab[...] = yy[:, half * GW:(half + 1) * GW]
                for n2 in range(DFT_N2):
                    cols = slice(n2 * FNET_W + g * GW, n2 * FNET_W + (g + 1) * GW)
                    dst[:, cols] = slab[pl.ds(n2, TM // DFT_N2, stride=DFT_N2), :].astype(BF16)


def _inproj(x_ctx, x_lat, mod, g, w_in, sgu_g, sgu_w, sgu_b, tb):
    ctx, lat = _two_stream_specs(TM, D)
    npt = NP // TM
    rows_r = TM // DFT_N2
    yr_shape = jax.ShapeDtypeStruct((NS // DFT_N2, DFT_N2 * FNET_W), BF16)
    yr_spec = pl.BlockSpec((rows_r, DFT_N2 * FNET_W), lambda i: (jnp.maximum(i - npt, 0), 0))
    return pl.pallas_call(
        _inproj_kernel,
        out_shape=(jax.ShapeDtypeStruct((T, SGU_W), BF16), jax.ShapeDtypeStruct((NP, 2 * FNET_W), BF16),
                   yr_shape, yr_shape),
        grid=(T // TM,),
        in_specs=[
            ctx, lat,
            _mod_spec(0, TM),
            _resident((1, D)),
            _resident((D, 3 * SGU_W)),
            _resident((1, SGU_W)),
            _resident((4, CHUNK, CHUNK)),
            _resident((CHUNK, SGU_W)),
            _resident((GW, 2 * GW)),
        ],
        out_specs=(pl.BlockSpec((TM, SGU_W), lambda i: (i, 0)),
                   pl.BlockSpec((TM, 2 * FNET_W), lambda i: (jnp.minimum(i, npt - 1), 0)), yr_spec, yr_spec),
        scratch_shapes=[pltpu.VMEM((8, TM, LANES), F32)],
        compiler_params=_params(("arbitrary",)),
        name="inproj_mix",
    )(x_ctx, x_lat, mod, g, w_in, sgu_g, sgu_w, sgu_b, tb["dft_ch"])


def _mix_out(f, a_ref, x_ref, mod_ref, wout_ref, o_ref):
    mix = (jnp.dot(a_ref[...], wout_ref[:SGU_W, :], preferred_element_type=F32)
           + jnp.dot(f.astype(BF16), wout_ref[SGU_W:, :], preferred_element_type=F32))
    o_ref[...] = x_ref[...] + mod_ref[2:3, :] * mix


def _mixout_ctx_kernel(y_ref, a_ref, x_ref, mod_ref, dft_ref, wout_ref, o_ref):
    fs = []
    for s in range(CTX_PER_STEP):
        rows = slice(s * CTX_LEN, (s + 1) * CTX_LEN)
        fs.append(jnp.dot(dft_ref[:, :CTX_LEN], y_ref[rows, :FNET_W], preferred_element_type=F32)
                  + jnp.dot(dft_ref[:, CTX_LEN:], y_ref[rows, FNET_W:], preferred_element_type=F32))
    f = jnp.concatenate(fs, axis=0) * (1.0 / math.sqrt(CTX_LEN * GW))
    _mix_out(f, a_ref, x_ref, mod_ref, wout_ref, o_ref)


def _mixout_ctx(y, a, x, mod, w_out, tb):
    rows = CTX_PER_STEP * CTX_LEN
    tok = lambda w: pl.BlockSpec((rows, w), lambda i: (i, 0))
    return pl.pallas_call(
        _mixout_ctx_kernel,
        out_shape=jax.ShapeDtypeStruct((NP, D), F32),
        grid=(N_CTX_SEQ // CTX_PER_STEP,),
        in_specs=[tok(2 * FNET_W), tok(SGU_W), tok(D),
                  pl.BlockSpec((None, None, 6, D), lambda i: (0, CTX_ROW, 0, 0)),
                  _resident((CTX_LEN, 2 * CTX_LEN)), _resident((D, D))],
        out_specs=tok(D),
        compiler_params=_params(("arbitrary",)),
        name="mixout_ctx",
    )(y, a, x, mod, tb["dft_ctx"], w_out)


def _dft1_kernel(y1_ref, y2_ref, c_ref, s_ref, twc_ref, tws_ref, br_ref, bi_ref):
    y1 = y1_ref[...]
    y2 = y2_ref[...]
    c = c_ref[...]
    s = s_ref[...]
    ar = jnp.dot(c, y1, preferred_element_type=F32) - jnp.dot(s, y2, preferred_element_type=F32)
    ai = -(jnp.dot(c, y2, preferred_element_type=F32) + jnp.dot(s, y1, preferred_element_type=F32))
    lane = lax.broadcasted_iota(I32, (DFT_N1, DFT_N2), 1)
    for r in range(DFT1_GROUP):
        n2 = pl.program_id(1) * DFT1_GROUP + r
        cols = slice(r * FNET_W, (r + 1) * FNET_W)
        tc = jnp.sum(jnp.where(lane == n2, twc_ref[...], 0.0), axis=-1, keepdims=True)
        ts = jnp.sum(jnp.where(lane == n2, tws_ref[...], 0.0), axis=-1, keepdims=True)
        br_ref[r] = (tc * ar[:, cols] + ts * ai[:, cols]).astype(BF16)
        bi_ref[r] = (tc * ai[:, cols] - ts * ar[:, cols]).astype(BF16)


def _dft1(y1r, y2r, tb):
    shp = jax.ShapeDtypeStruct((N_LAT_SEQ, DFT_N2, DFT_N1, FNET_W), BF16)
    src = pl.BlockSpec((None, DFT_N1, DFT1_GROUP * FNET_W), lambda b, n: (b, 0, n))
    dst = pl.BlockSpec((None, DFT1_GROUP, DFT_N1, FNET_W), lambda b, n: (b, n, 0, 0))
    return pl.pallas_call(
        _dft1_kernel,
        out_shape=(shp, shp),
        grid=(N_LAT_SEQ, DFT_N2 // DFT1_GROUP),
        in_specs=[src, src, _resident((DFT_N1, DFT_N1)), _resident((DFT_N1, DFT_N1)),
                  _resident((DFT_N1, DFT_N2)), _resident((DFT_N1, DFT_N2))],
        out_specs=(dst, dst),
        compiler_params=_params(("arbitrary", "arbitrary")),
        name="dft_stage1",
    )(y1r, y2r, tb["c256"], tb["s256"], tb["tw_cos"], tb["tw_sin"])


def _cmul_const(z, wr, wi):
    re, im = z
    tol = 1e-12
    if abs(wi) < tol:
        return (re, im) if wr > 0 else (-re, -im)
    if abs(wr) < tol:
        return (-im, re) if wi > 0 else (im, -re)
    return (re * wr - im * wi, re * wi + im * wr)


def _fft(zs):
    n = len(zs)
    if n == 1:
        return zs
    even, odd = _fft(zs[0::2]), _fft(zs[1::2])
    out = [None] * n
    for k in range(n // 2):
        ang = 2.0 * math.pi * k / n
        t = _cmul_const(odd[k], math.cos(ang), -math.sin(ang))
        out[k] = (even[k][0] + t[0], even[k][1] + t[1])
        out[k + n // 2] = (even[k][0] - t[0], even[k][1] - t[1])
    return out


def _mixout_lat_kernel(br_ref, bi_ref, a_ref, x_ref, mod_ref, wout_ref, o_ref):
    spec = _fft([(br_ref[n2].astype(F32), bi_ref[n2].astype(F32)) for n2 in range(DFT_N2)])
    f = jnp.concatenate([re for re, _ in spec], axis=0) * (1.0 / math.sqrt(LAT_LEN * GW))
    rows = DFT_N2 * DFT_ROWS
    mix = (jnp.dot(a_ref[...].reshape(rows, SGU_W), wout_ref[:SGU_W, :], preferred_element_type=F32)
           + jnp.dot(f.astype(BF16), wout_ref[SGU_W:, :], preferred_element_type=F32))
    out = x_ref[...].reshape(rows, D) + mod_ref[2:3, :] * mix
    o_ref[...] = out.reshape(DFT_N2, DFT_ROWS, D)


def _mixout_lat(br, bi, a_lat, x, mod, w_out):
    blk = lambda w, off=0: pl.BlockSpec((None, DFT_N2, DFT_ROWS, w), lambda b, r: (b + off, 0, r, 0))
    return pl.pallas_call(
        _mixout_lat_kernel,
        out_shape=jax.ShapeDtypeStruct((N_LAT_SEQ, DFT_N2, DFT_N1, D), F32),
        grid=(N_LAT_SEQ, DFT_N1 // DFT_ROWS),
        in_specs=[blk(FNET_W), blk(FNET_W), blk(SGU_W, NP // LAT_LEN), blk(D),
                  pl.BlockSpec((None, None, 6, D), lambda b, r: (0, b, 0, 0)),
                  _resident((D, D))],
        out_specs=blk(D),
        compiler_params=_params(("arbitrary", "arbitrary")),
        name="mixout_lat",
    )(br, bi, a_lat, x, mod, w_out)


def _ffn_kernel(xc_ref, xl_ref, mod_ref, g_ref, w1_ref, w3_ref, w2_ref, o_ref):
    x = _pick_stream(TM, xc_ref, xl_ref)
    h = _modulate(x, g_ref[...], mod_ref[4:5, :], mod_ref[3:4, :]).astype(BF16)
    acc = None
    for c in range(D_FF // FFN_CHUNK):
        lo, hi = c * FFN_CHUNK, (c + 1) * FFN_CHUNK
        a = jnp.dot(h, w1_ref[:, lo:hi], preferred_element_type=F32)
        b = jnp.dot(h, w3_ref[:, lo:hi], preferred_element_type=F32)
        d = jnp.dot((_silu(a) * b).astype(BF16), w2_ref[lo:hi, :], preferred_element_type=F32)
        acc = d if acc is None else acc + d
    o_ref[...] = x + mod_ref[5:6, :] * acc


def _ffn(x_ctx, x_lat, mod, g, w1, w3, w2):
    ctx, lat = _two_stream_specs(TM, D)
    return pl.pallas_call(
        _ffn_kernel,
        out_shape=jax.ShapeDtypeStruct((T, D), F32),
        grid=(T // TM,),
        in_specs=[
            ctx, lat,
            _mod_spec(0, TM),
            _resident((1, D)),
            _resident((D, D_FF)),
            _resident((D, D_FF)),
            _resident((D_FF, D)),
        ],
        out_specs=pl.BlockSpec((TM, D), lambda i: (i, 0)),
        compiler_params=_params(("arbitrary",)),
        name="ffn",
    )(x_ctx, x_lat, mod, g, w1, w3, w2)


def _qkv_lat_kernel(x_ref, mod_ref, g_ref, w_ref, cos_ref, sin_ref, q_ref, k_ref, v_ref):
    first = (lax.broadcasted_iota(I32, (TM, LANES), 1) & 31) < 16
    for sub in range(QKV_LAT_ROWS // TM):
        rows = slice(sub * TM, (sub + 1) * TM)
        h = _modulate(x_ref[rows, :], g_ref[...], mod_ref[1:2, :], mod_ref[0:1, :]).astype(BF16)
        qkv = jnp.dot(h, w_ref[...], preferred_element_type=F32)
        cos = cos_ref[rows, :]
        sin = sin_ref[rows, :]

        def rotate(xh):
            partner = jnp.where(first, pltpu.roll(xh, LANES - 16, 1), pltpu.roll(xh, 16, 1))
            return xh * cos + partner * sin

        for hh in range(N_HEADS):
            sl = slice(hh * LANES, (hh + 1) * LANES)
            q_ref[rows, sl] = (rotate(qkv[:, sl]) * Q_SCALE).astype(BF16)
            k_ref[rows, sl] = rotate(qkv[:, D + hh * LANES:D + (hh + 1) * LANES]).astype(BF16)
        v_ref[rows, :] = qkv[:, 2 * D:].astype(BF16)


def _qkv_lat(x, mod, g, w_qkv, tb):
    rows = QKV_LAT_ROWS
    tok = pl.BlockSpec((rows, D), lambda i: (i, 0))
    lshape = jax.ShapeDtypeStruct((NS, D), BF16)
    off = NP // rows
    ltok = pl.BlockSpec((rows, D), lambda i: (i + off, 0))
    rope = pl.BlockSpec((rows, LANES), lambda i: (i % (LAT_LEN // rows), 0))
    return pl.pallas_call(
        _qkv_lat_kernel,
        out_shape=(lshape, lshape, lshape),
        grid=(NS // rows,),
        in_specs=[ltok, _mod_spec(1, rows, off), _resident((1, D)), _resident((D, 3 * D)), rope, rope],
        out_specs=(tok, tok, tok),
        compiler_params=_params(("arbitrary",)),
        name="qkv_lat",
    )(x, mod, g, w_qkv, tb["rope_cos"], tb["rope_sin"])


def _lam(lp_ref):
    lp = lp_ref[...]
    a = jnp.sum(lp[0:1, :] * lp[1:2, :], axis=-1, keepdims=True)
    b = jnp.sum(lp[2:3, :] * lp[3:4, :], axis=-1, keepdims=True)
    return jnp.exp(a) - jnp.exp(b) + LAM0


def _head_norm(o, sg):
    ms = jnp.mean(o * o, axis=-1, keepdims=True)
    return o * lax.rsqrt(ms + EPS) * (sg * (1.0 - LAM0))


_NT = (((1,), (1,)), ((), ()))


def _qkv_attn_ctx_kernel(x_ref, mod_ref, g_ref, w_ref, lp_ref, sg_ref, o_ref, kf_ref, vf_ref):
    lam = _lam(lp_ref)
    map0 = lax.broadcasted_iota(I32, (CTX_LEN, LANES), 1) < HEAD_DIM
    init = _softmax_init(CTX_LEN)
    for pair in range(CTX_ATTN_ROWS // TM):
        rows = slice(pair * TM, (pair + 1) * TM)
        h = _modulate(x_ref[rows, :], g_ref[...], mod_ref[1:2, :], mod_ref[0:1, :]).astype(BF16)
        qkv = jnp.dot(h, w_ref[...], preferred_element_type=F32)
        kf_ref[rows, :] = qkv[:, D:2 * D]
        vf_ref[rows, :] = qkv[:, 2 * D:]
        qb = (qkv[:, :D] * Q_SCALE).astype(BF16)
        kb = qkv[:, D:2 * D].astype(BF16)
        vb = qkv[:, 2 * D:].astype(BF16)
        for s in range(TM // CTX_LEN):
            seq = slice(s * CTX_LEN, (s + 1) * CTX_LEN)
            for hh in range(N_HEADS):
                sl = slice(hh * LANES, (hh + 1) * LANES)
                q = qb[seq, sl]
                zero = jnp.zeros_like(q)
                outs = []
                for qc in (jnp.where(map0, q, zero), jnp.where(map0, zero, q)):
                    _, l, acc = _softmax_step(qc, kb[seq, sl], vb[seq, sl], init)
                    outs.append(acc / jnp.sum(l, axis=-1, keepdims=True))
                o_ref[pair * TM + s * CTX_LEN:pair * TM + (s + 1) * CTX_LEN, sl] = _head_norm(
                    outs[0] - lam * outs[1], sg_ref[...]).astype(BF16)


def _qkv_attn_ctx(x, mod, g, w_qkv, lam_p, subln_g):
    tok = pl.BlockSpec((CTX_ATTN_ROWS, D), lambda i: (i, 0))
    return pl.pallas_call(
        _qkv_attn_ctx_kernel,
        out_shape=(jax.ShapeDtypeStruct((NP, D), BF16), jax.ShapeDtypeStruct((NP, D), F32),
                   jax.ShapeDtypeStruct((NP, D), F32)),
        grid=(NP // CTX_ATTN_ROWS,),
        in_specs=[tok, pl.BlockSpec((None, None, 6, D), lambda i: (1, CTX_ROW, 0, 0)), _resident((1, D)),
                  _resident((D, 3 * D)), _resident((4, HEAD_DIM)), _resident((1, V_DIM))],
        out_specs=(tok, tok, tok),
        compiler_params=_params(("arbitrary",)),
        name="qkv_attn_ctx",
    )(x, mod, g, w_qkv, lam_p, subln_g)


def _softmax_init(rows):
    return (jnp.full((rows, LANES), -jnp.inf, F32), jnp.zeros((rows, LANES), F32), jnp.zeros((rows, LANES), F32))


def _softmax_step(qc, kc, vc, state):
    m, l, acc = state
    s = lax.dot_general(qc, kc, _NT, preferred_element_type=F32)
    blocks = [s[:, j * LANES:(j + 1) * LANES] for j in range(s.shape[1] // LANES)]
    bm = functools.reduce(jnp.maximum, blocks)
    m_new = jnp.maximum(m, jnp.max(bm, axis=-1, keepdims=True))
    alpha = jnp.exp2(m - m_new)
    ps = [jnp.exp2(b - m_new) for b in blocks]
    l_new = alpha * l + functools.reduce(jnp.add, ps)
    p = jnp.concatenate(ps, axis=1).astype(BF16)
    acc_new = alpha * acc + jnp.dot(p, vc, preferred_element_type=F32)
    return m_new, l_new, acc_new


def _attn_lat_kernel(q_ref, k_ref, v_ref, ck_ref, cv_ref, lp_ref, sg_ref, w1_ref, w3_ref, w2_ref,
                     o_ref, w1b_ref, w3b_ref, w2b_ref):
    step = (pl.program_id(0) * N_HEADS + pl.program_id(1)) * (LAT_LEN // TQ) + pl.program_id(2)
    for phase, (src, dst) in enumerate(((w1_ref, w1b_ref), (w3_ref, w3b_ref), (w2_ref, w2b_ref))):
        @pl.when((step >= phase * CAST_BLOCKS) & (step < (phase + 1) * CAST_BLOCKS))
        def _(src=src, dst=dst):
            dst[...] = src[...].astype(BF16)

    q = q_ref[...]
    zero = jnp.zeros_like(q)
    map0 = lax.broadcasted_iota(I32, (TQ, LANES), 1) < HEAD_DIM
    qs = (jnp.where(map0, q, zero), jnp.where(map0, zero, q))
    init = _softmax_init(TQ)
    states = [init, init]
    head_rows = pl.ds(pl.program_id(1), CTX_LEN, stride=N_HEADS)
    chunks = [(ck_ref[head_rows, :].astype(BF16), cv_ref[head_rows, :].astype(BF16))]
    chunks += [(k_ref[j * TK:(j + 1) * TK, :], v_ref[j * TK:(j + 1) * TK, :]) for j in range(LAT_LEN // TK)]
    for kc, vc in chunks:
        states = [_softmax_step(qs[c], kc, vc, states[c]) for c in range(2)]
    outs = [acc / jnp.sum(l, axis=-1, keepdims=True) for _, l, acc in states]
    o = outs[0] - _lam(lp_ref) * outs[1]
    o_ref[...] = _head_norm(o, sg_ref[...]).astype(BF16)


def _attn_lat(q, k, v, cache_k, cache_v, lam_p, subln_g, moe_w1, moe_w3, moe_w2):
    nq = LAT_LEN // TQ
    assert N_LAT_SEQ * N_HEADS * nq >= 3 * CAST_BLOCKS
    qspec = pl.BlockSpec((TQ, LANES), lambda b, h, i: (b * nq + i, h))
    kspec = pl.BlockSpec((LAT_LEN, LANES), lambda b, h, i: (b, h))
    cspec = pl.BlockSpec((None, CTX_LEN * N_HEADS, LANES), lambda b, h, i: (b, 0, 0))

    def cast_block(phase, b, h, i):
        j = jnp.clip((b * N_HEADS + h) * nq + i - phase * CAST_BLOCKS, 0, CAST_BLOCKS - 1)
        return j // CAST_SPLIT, j % CAST_SPLIT

    def up_spec(phase):
        return pl.BlockSpec((None, D, CAST_COLS), lambda b, h, i: (cast_block(phase, b, h, i)[0], 0,
                                                                    cast_block(phase, b, h, i)[1]))

    down_spec = pl.BlockSpec((None, CAST_COLS, D), lambda b, h, i: (*cast_block(2, b, h, i), 0))
    up_shape = jax.ShapeDtypeStruct((N_EXPERTS, D, D_FF_EXPERT), BF16)
    return pl.pallas_call(
        _attn_lat_kernel,
        out_shape=(jax.ShapeDtypeStruct((NS, D), BF16), up_shape, up_shape,
                   jax.ShapeDtypeStruct((N_EXPERTS, D_FF_EXPERT, D), BF16)),
        grid=(N_LAT_SEQ, N_HEADS, nq),
        in_specs=[qspec, kspec, kspec, cspec, cspec, _resident((4, HEAD_DIM)), _resident((1, V_DIM)),
                  up_spec(0), up_spec(1), down_spec],
        out_specs=(qspec, up_spec(0), up_spec(1), down_spec),
        compiler_params=_params(("arbitrary", "arbitrary", "arbitrary")),
        name="attn_lat",
    )(q, k, v, cache_k, cache_v, lam_p, subln_g, moe_w1, moe_w3, moe_w2)


def _oproj_route_kernel(oc_ref, ol_ref, x_ref, mod_ref, wo_ref, g_ref, wr_ref, tri_ref, below_ref, x3_ref, h_ref,
                        route_ref, cnt_ref):
    is_ctx = pl.program_id(0) < NP // (ROUTE_SUB * TM)
    for sub in range(ROUTE_SUB):
        rows = slice(sub * TM, (sub + 1) * TM)
        o = jnp.where(is_ctx, oc_ref[rows, :], ol_ref[rows, :])
        _route_tile(o, x_ref[rows, :], mod_ref, wo_ref, g_ref, wr_ref, tri_ref, below_ref,
                    x3_ref.at[rows, :], h_ref.at[rows, :], route_ref.at[rows, :], cnt_ref.at[sub])


def _route_tile(o, x, mod_ref, wo_ref, g_ref, wr_ref, tri_ref, below_ref, x3_ref, h_ref, route_ref, cnt_ref):
    x3 = x + mod_ref[2:3, :] * jnp.dot(o, wo_ref[...], preferred_element_type=F32)
    x3_ref[...] = x3
    hb = _modulate(x3, g_ref[...], mod_ref[4:5, :], mod_ref[3:4, :]).astype(BF16)
    h_ref[...] = hb

    logits = jnp.dot(hb, wr_ref[...], preferred_element_type=F32)
    lane = lax.broadcasted_iota(I32, (TM, LANES), 1).astype(F32)
    neg = jnp.float32(-jnp.inf)
    lg = jnp.where(lane < N_EXPERTS, logits, neg)
    v1 = jnp.max(lg, axis=-1, keepdims=True)
    i1 = jnp.min(jnp.where(lg == v1, lane, float(LANES)), axis=-1, keepdims=True)
    lg2 = jnp.where(lane == i1, neg, lg)
    v2 = jnp.max(lg2, axis=-1, keepdims=True)
    i2 = jnp.min(jnp.where(lg2 == v2, lane, float(LANES)), axis=-1, keepdims=True)
    e = jnp.exp(v2 - v1)
    g1 = 1.0 / (1.0 + e)
    g2 = e / (1.0 + e)
    sel1 = lane == i1
    sel2 = lane == i2
    onehot = jnp.where(sel1 | sel2, 1.0, 0.0)
    rank = jnp.dot(tri_ref[...], onehot.astype(BF16), preferred_element_type=F32)
    cnt8 = jnp.floor((jnp.sum(onehot, axis=0, keepdims=True) + (SEG_ALIGN - 1)) * (1.0 / SEG_ALIGN))
    base8 = jnp.dot(jnp.broadcast_to(cnt8, (8, LANES)).astype(BF16), below_ref[...], preferred_element_type=F32)
    slot = base8[0:1, :] * float(SEG_ALIGN) + rank
    pos1 = jnp.sum(jnp.where(sel1, slot, 0.0), axis=-1, keepdims=True)
    pos2 = jnp.sum(jnp.where(sel2, slot, 0.0), axis=-1, keepdims=True)
    cnt_ref[...] = cnt8
    route = jnp.where(lane == 0, i1, 0.0)
    route = jnp.where(lane == 1, i2, route)
    route = jnp.where(lane == 2, g1, route)
    route = jnp.where(lane == 3, g2, route)
    route = jnp.where(lane == 4, pos1, route)
    route = jnp.where(lane == 5, pos2, route)
    route_ref[...] = route


def _oproj_route(o_ctx, o_lat, x, mod, w_o, g, w_r, tb):
    rows = ROUTE_SUB * TM
    tok = pl.BlockSpec((rows, D), lambda i: (i, 0))
    ctx, lat = _two_stream_specs(rows, D)
    return pl.pallas_call(
        _oproj_route_kernel,
        out_shape=(jax.ShapeDtypeStruct((T, D), F32), jax.ShapeDtypeStruct((T, D), BF16),
                   jax.ShapeDtypeStruct((T, LANES), F32), jax.ShapeDtypeStruct((T // TM, 1, LANES), F32)),
        grid=(T // rows,),
        in_specs=[ctx, lat, tok, _mod_spec(1, rows), _resident((D, D)), _resident((1, D)), _resident((D, LANES)),
                  _resident((TM, TM)), _resident((LANES, LANES))],
        out_specs=(tok, tok, pl.BlockSpec((rows, LANES), lambda i: (i, 0)),
                   pl.BlockSpec((ROUTE_SUB, 1, LANES), lambda i: (i, 0, 0))),
        compiler_params=_params(("arbitrary",)),
        name="oproj_route",
    )(o_ctx, o_lat, x, mod, w_o, g, w_r, tb["tri"], tb["below"])


def _bit_copies(n, src0, dst0, bits, make_copy, op):
    for bit in bits:
        done = n & (-2 * bit)

        @pl.when((n & bit) != 0)
        def _(done=done, bit=bit):
            src = pl.multiple_of((src0 + done) * SEG_ALIGN, SEG_ALIGN)
            dst = pl.multiple_of((dst0 + done) * SEG_ALIGN, SEG_ALIGN)
            op(make_copy(src, dst, bit * SEG_ALIGN))


def _segment_copies(tab, make_copy, op):
    for e in range(N_EXPERTS):
        _bit_copies(tab(e), tab(N_EXPERTS + e), tab(2 * N_EXPERTS + e), SEG_BITS, make_copy, op)


def _start(copy):
    copy.start()


def _wait(copy):
    copy.wait()


def _dispatch_kernel(tab_ref, fill_ref, h_ref, route_ref, xs_ref, sorted_s, zero_s, sem, fill_sem):
    i = pl.program_id(0)
    slot = i % 2
    slots = route_ref[...].T
    row = lax.broadcasted_iota(I32, (L_TILE, TM), 0).astype(F32)
    perm = jnp.where((row == slots[4:5, :]) | (row == slots[5:6, :]), 1.0, 0.0).astype(BF16)
    sorted_s[slot] = jnp.dot(perm, h_ref[...], preferred_element_type=F32)

    def copies(tile, s, op):
        def seg(src, dst, rows):
            return pltpu.make_async_copy(sorted_s.at[s, pl.ds(src, rows), :], xs_ref.at[pl.ds(dst, rows), :], sem.at[s])

        _segment_copies(lambda j: tab_ref[tile, j], seg, op)

    copies(i, slot, _start)

    @pl.when(i > 0)
    def _():
        copies(i - 1, 1 - slot, _wait)

    @pl.when(i == pl.num_programs(0) - 1)
    def _():
        copies(i, slot, _wait)

    @pl.when(i == 0)
    def _():
        zero_s[...] = jnp.zeros(zero_s.shape, F32)

        def fill(src, dst, rows):
            del src
            return pltpu.make_async_copy(zero_s.at[pl.ds(0, rows), :], xs_ref.at[pl.ds(dst, rows), :], fill_sem)

        def fills(op):
            for e in range(N_EXPERTS):
                _bit_copies(fill_ref[0, N_EXPERTS + e], 0, fill_ref[0, e], SEG_BITS[1:], fill, op)
            for k in range(MAX_UNUSED_TILES):
                @pl.when(k < fill_ref[0, 2 * N_EXPERTS + 1])
                def _(k=k):
                    dst = pl.multiple_of(fill_ref[0, 2 * N_EXPERTS] * SEG_ALIGN + k * T_MOE, SEG_ALIGN)
                    op(fill(0, dst, T_MOE))

        fills(_start)
        fills(_wait)


def _dispatch(tab, fill, h, route):
    smem = pl.BlockSpec(memory_space=pltpu.SMEM)
    return pl.pallas_call(
        _dispatch_kernel,
        out_shape=jax.ShapeDtypeStruct((N_ROWS, D), F32),
        grid=(T // TM,),
        in_specs=[smem, smem, pl.BlockSpec((TM, D), lambda i: (i, 0)), pl.BlockSpec((TM, LANES), lambda i: (i, 0))],
        out_specs=pl.BlockSpec(memory_space=pl.ANY),
        scratch_shapes=[pltpu.VMEM((2, L_TILE, D), F32), pltpu.VMEM((T_MOE, D), F32),
                        pltpu.SemaphoreType.DMA((2,)), pltpu.SemaphoreType.DMA(())],
        compiler_params=_params(("arbitrary",)),
        name="moe_dispatch",
    )(tab, fill, h, route)


def _moe_kernel(te_ref, tv_ref, xs_ref, w1_ref, w3_ref, w2_ref, y_ref, xb_s, acc_s):
    del te_ref
    i = pl.program_id(0)
    f = pl.program_id(1)
    valid = tv_ref[i] == 1

    def chunk(x):
        a = jnp.dot(x, w1_ref[...], preferred_element_type=F32)
        b = jnp.dot(x, w3_ref[...], preferred_element_type=F32)
        return jnp.dot((_silu(a) * b).astype(BF16), w2_ref[...], preferred_element_type=F32)

    @pl.when(valid & (f == 0))
    def _():
        x = xs_ref[...].astype(BF16)
        xb_s[...] = x
        acc_s[...] = chunk(x)

    @pl.when(valid & (f == N_F_MOE - 1))
    def _():
        y_ref[...] = acc_s[...] + chunk(xb_s[...])

    @pl.when(jnp.logical_not(valid) & (f == N_F_MOE - 1))
    def _():
        y_ref[...] = jnp.zeros(y_ref.shape, F32)


def _moe(tile_expert, tile_valid, xs, w1, w3, w2):
    def fidx(i, f, te, tv):
        return jnp.where(tv[i] == 1, f, N_F_MOE - 1)

    grid_spec = pltpu.PrefetchScalarGridSpec(
        num_scalar_prefetch=2,
        grid=(N_MOE_TILES, N_F_MOE),
        in_specs=[
            pl.BlockSpec((T_MOE, D), lambda i, f, te, tv: (i, 0)),
            pl.BlockSpec((None, D, F_MOE), lambda i, f, te, tv: (te[i], 0, fidx(i, f, te, tv))),
            pl.BlockSpec((None, D, F_MOE), lambda i, f, te, tv: (te[i], 0, fidx(i, f, te, tv))),
            pl.BlockSpec((None, F_MOE, D), lambda i, f, te, tv: (te[i], fidx(i, f, te, tv), 0)),
        ],
        out_specs=pl.BlockSpec((T_MOE, D), lambda i, f, te, tv: (i, 0)),
        scratch_shapes=[pltpu.VMEM((T_MOE, D), BF16), pltpu.VMEM((T_MOE, D), F32)],
    )
    return pl.pallas_call(
        _moe_kernel,
        out_shape=jax.ShapeDtypeStruct((N_ROWS, D), F32),
        grid_spec=grid_spec,
        compiler_params=_params(("arbitrary", "arbitrary")),
        name="moe_experts",
    )(tile_expert, tile_valid, xs, w1, w3, w2)


def _combine_kernel(tab_ref, y_ref, x_ref, route_ref, mod_ref, fg_ref, o_ref, buf, sem, *, first_tile):
    i = pl.program_id(0)
    slot = i % 2

    def fetch(tile, s, op):
        def seg(src, dst, rows):
            return pltpu.make_async_copy(y_ref.at[pl.ds(dst, rows), :], buf.at[s, pl.ds(src, rows), :], sem.at[s])

        _segment_copies(lambda j: tab_ref[tile, j], seg, op)

    @pl.when(i == 0)
    def _():
        buf[...] = jnp.zeros(buf.shape, F32)
        fetch(first_tile, 0, _start)

    @pl.when(i + 1 < pl.num_programs(0))
    def _():
        fetch(first_tile + i + 1, 1 - slot, _start)

    fetch(first_tile + i, slot, _wait)
    yb = buf[slot].astype(BF16)
    route = route_ref[...]
    col = lax.broadcasted_iota(I32, (TM, L_TILE), 1).astype(F32)
    picks = [jnp.dot(jnp.where(col == route[:, c:c + 1], 1.0, 0.0).astype(BF16), yb, preferred_element_type=F32)
             for c in (4, 5)]
    moe = route[:, 2:3] * picks[0] + route[:, 3:4] * picks[1]
    x = x_ref[...] + mod_ref[5:6, :] * moe
    ms = jnp.mean(x * x, axis=-1, keepdims=True)
    o_ref[...] = x * lax.rsqrt(ms + EPS) * fg_ref[...]


def _combine(tab, y, x, route, mod, final_g, n_tok, off_tiles):
    tok = lambda w: pl.BlockSpec((TM, w), lambda i: (i + off_tiles, 0))
    return pl.pallas_call(
        functools.partial(_combine_kernel, first_tile=off_tiles),
        out_shape=jax.ShapeDtypeStruct((n_tok, D), F32),
        grid=(n_tok // TM,),
        in_specs=[
            pl.BlockSpec(memory_space=pltpu.SMEM),
            pl.BlockSpec(memory_space=pl.ANY),
            tok(D), tok(LANES), _mod_spec(1, TM, off_tiles), _resident((1, D)),
        ],
        out_specs=pl.BlockSpec((TM, D), lambda i: (i, 0)),
        scratch_shapes=[pltpu.VMEM((2, L_TILE, D), F32), pltpu.SemaphoreType.DMA((2,))],
        compiler_params=_params(("arbitrary",)),
        name="moe_combine",
    )(tab, y, x, route, mod, final_g)


def kernel(x_prompt, x_sample, cache_k, cache_v, c, c_ctx, ada_w, ada_b, norm1_g, norm2_g, final_g, mix_w_in, sgu_g, sgu_w, sgu_b, mix_w_out, ffn_w1, ffn_w3, ffn_w2, attn_w_qkv, lam_q1, lam_k1, lam_q2, lam_k2, subln_g, attn_w_o, router_w, moe_w1, moe_w3, moe_w2):
    tb = _device_tables()
    xc = x_prompt.reshape(NP, D)
    xl = x_sample.reshape(NS, D)
    cvec = jnp.concatenate([c, c_ctx[None, :], jnp.zeros((3, D), F32)], axis=0)
    mod = _adaln(cvec, ada_w, ada_b)

    bias = jnp.repeat(sgu_b[0].T, GW, axis=1)
    a, y, y1r, y2r = _inproj(xc, xl, mod, norm1_g[0][None, :], mix_w_in[0].astype(BF16), sgu_g[0].reshape(1, SGU_W),
                             sgu_w[0].astype(BF16), bias, tb)
    w_out = mix_w_out[0].astype(BF16)
    x1c = _mixout_ctx(y, a, xc, mod, w_out, tb)
    br, bi = _dft1(y1r.reshape(N_LAT_SEQ, DFT_N1, DFT_N2 * FNET_W), y2r.reshape(N_LAT_SEQ, DFT_N1, DFT_N2 * FNET_W), tb)
    x1l = _mixout_lat(br, bi, a.reshape(T // LAT_LEN, DFT_N2, DFT_N1, SGU_W),
                      xl.reshape(N_LAT_SEQ, DFT_N2, DFT_N1, D), mod, w_out).reshape(NS, D)
    x2 = _ffn(x1c, x1l, mod, norm2_g[0][None, :], ffn_w1[0].astype(BF16), ffn_w3[0].astype(BF16), ffn_w2[0].astype(BF16))

    w_qkv = attn_w_qkv[0].astype(BF16)
    lam_p = jnp.stack([lam_q1[0], lam_k1[0], lam_q2[0], lam_k2[0]], axis=0)
    sg = subln_g[0][None, :]
    oc, kf, vf = _qkv_attn_ctx(x2, mod, norm1_g[1][None, :], w_qkv, lam_p, sg)
    qkv_l = _qkv_lat(x2, mod, norm1_g[1][None, :], w_qkv, tb)
    cache_rows = (N_LAT_SEQ, CTX_LEN * N_HEADS, LANES)
    ol, w1b, w3b, w2b = _attn_lat(*qkv_l, cache_k.reshape(cache_rows), cache_v.reshape(cache_rows), lam_p, sg,
                                  moe_w1[0], moe_w3[0], moe_w2[0])

    w_r = jnp.pad(router_w[0], ((0, 0), (0, LANES - N_EXPERTS))).astype(BF16)
    x3, h2, route, cnt = _oproj_route(oc, ol, x2, mod, attn_w_o[0].astype(BF16), norm2_g[1][None, :], w_r, tb)
    unit_per_tile = T_MOE // SEG_ALIGN
    seg_len = cnt[:, 0, :N_EXPERTS].astype(I32)
    seg_local = jnp.cumsum(seg_len, axis=1) - seg_len
    group_len = jnp.sum(seg_len, axis=0)
    group_tiles = (group_len + unit_per_tile - 1) // unit_per_tile
    tile_ends = jnp.cumsum(group_tiles)
    group_start = (tile_ends - group_tiles) * unit_per_tile
    seg_global = group_start[None, :] + jnp.cumsum(seg_len, axis=0) - seg_len
    tab = jnp.concatenate([seg_len, seg_local, seg_global], axis=1)
    tab = jnp.pad(tab, ((0, 0), (0, LANES - 3 * N_EXPERTS)))
    fill = jnp.concatenate([group_start + group_len, group_tiles * unit_per_tile - group_len,
                            tile_ends[-1:] * unit_per_tile, N_MOE_TILES - tile_ends[-1:]])
    fill = jnp.pad(fill, (0, LANES - 2 * N_EXPERTS - 2)).reshape(1, LANES)
    tile_idx = jnp.arange(N_MOE_TILES, dtype=I32)
    tile_valid = (tile_idx < tile_ends[-1]).astype(I32)
    tile_expert = jnp.sum((jnp.minimum(tile_idx, tile_ends[-1] - 1)[:, None] >= tile_ends[None, :]).astype(I32), axis=-1)

    xs = _dispatch(tab, fill, h2, route)
    ys = _moe(tile_expert, tile_valid, xs, w1b, w3b, w2b)
    fg = final_g[None, :]
    y_prompt = _combine(tab, ys, x3, route, mod, fg, NP, 0)
    y_sample = _combine(tab, ys, x3, route, mod, fg, NS, NP // TM)

    return (y_prompt.reshape(N_CTX_SEQ, CTX_LEN, D), y_sample.reshape(N_LAT_SEQ, LAT_LEN, D),
            kf.reshape(N_CTX_SEQ, 1, CTX_LEN, N_HEADS, 2 * HEAD_DIM), vf.reshape(N_CTX_SEQ, 1, CTX_LEN, N_HEADS, V_DIM))
```

```python
import functools
import math

import numpy as np
import jax
import jax.numpy as jnp
from jax import lax
from jax.experimental import pallas as pl
from jax.experimental.pallas import tpu as pltpu

F32 = jnp.float32
BF16 = jnp.bfloat16
I32 = jnp.int32

D = 1024
N_CTX_SEQ = 32
CTX_LEN = 256
N_LAT_SEQ = 4
LAT_LEN = 4096
NP = N_CTX_SEQ * CTX_LEN
NS = N_LAT_SEQ * LAT_LEN
T = NP + NS
GRID_W = 64
CHUNK = 128
SGU_W = 512
FNET_W = 512
GW = 128
N_HEADS = 8
HEAD_DIM = 64
V_DIM = 128
ROPE_THETA = 10000.0
D_FF = 2816
N_EXPERTS = 8
D_FF_EXPERT = 3584
EPS = 1e-6
LAM0 = 0.8 - 0.6 * math.exp(-0.3 * 1)
CTX_ROW = 4

LANES = 128
TM = 512
TQ = 1024
TK = 2048
Q_SCALE = HEAD_DIM ** -0.5 * math.log2(math.e)
T_MOE = 512
F_MOE = 1792
N_F_MOE = D_FF_EXPERT // F_MOE
assert N_F_MOE == 2
CAST_SPLIT = 4
CAST_COLS = D_FF_EXPERT // CAST_SPLIT
CAST_BLOCKS = N_EXPERTS * CAST_SPLIT
SEG_ALIGN = 8
SEG_BITS = (64, 32, 16, 8, 4, 2, 1)
L_TILE = 1152
N_ROWS = -(-(2 * T + (T // TM) * N_EXPERTS * (SEG_ALIGN - 1) + N_EXPERTS * (T_MOE - SEG_ALIGN)) // T_MOE) * T_MOE
N_MOE_TILES = N_ROWS // T_MOE
MAX_UNUSED_TILES = N_MOE_TILES - 2 * T // T_MOE
FFN_CHUNK = 2816
DFT_N1 = 256
DFT_N2 = 16
DFT_ROWS = 32
DFT1_GROUP = 4
CTX_PER_STEP = 2
ROUTE_SUB = 2
CTX_ATTN_ROWS = 1024
VMEM_LIMIT = 56 * 2 ** 20


def _params(sem, vmem=VMEM_LIMIT):
    return pltpu.CompilerParams(dimension_semantics=sem, vmem_limit_bytes=vmem)


def _resident(shape):
    nd = len(shape)
    return pl.BlockSpec(shape, lambda *_: (0,) * nd, pipeline_mode=pl.Buffered(1))


def _mod_row(i, tm):
    npt = NP // tm
    return jnp.where(i < npt, CTX_ROW, (i - npt) // (LAT_LEN // tm))


def _mod_spec(layer, tm, offset_tiles=0):
    return pl.BlockSpec((None, None, 6, D), lambda i, *_: (layer, _mod_row(i + offset_tiles, tm), 0, 0))


def _two_stream_specs(tm, width):
    npt = NP // tm
    ctx = pl.BlockSpec((tm, width), lambda i: (jnp.minimum(i, npt - 1), 0))
    lat = pl.BlockSpec((tm, width), lambda i: (jnp.maximum(i - npt, 0), 0))
    return ctx, lat


def _pick_stream(tm, ctx_ref, lat_ref):
    return jnp.where(pl.program_id(0) < NP // tm, ctx_ref[...], lat_ref[...])


def _modulate(x, g, scale, shift):
    ms = jnp.mean(x * x, axis=-1, keepdims=True)
    return x * lax.rsqrt(ms + EPS) * (g * (1.0 + scale)) + shift


def _silu(a):
    return a * jax.nn.sigmoid(a)


def _gelu_tanh(x):
    return 0.5 * x * (1.0 + jnp.tanh(0.7978845608028654 * (x + 0.044715 * (x * x * x))))


@functools.lru_cache(maxsize=None)
def _tables():
    def cs(n):
        k = np.arange(n, dtype=np.int64)
        ang = 2.0 * np.pi * ((k[:, None] * k[None, :]) % n) / n
        return np.cos(ang), np.sin(ang)

    c128, s128 = cs(GW)
    dft_ch = np.concatenate([c128, s128], axis=1)
    c256, s256 = cs(CTX_LEN)
    dft_ctx = np.concatenate([c256, -s256], axis=1)
    k1 = np.arange(DFT_N1, dtype=np.int64)[:, None]
    n2 = np.arange(DFT_N2, dtype=np.int64)[None, :]
    tw = 2.0 * np.pi * ((k1 * n2) % LAT_LEN) / LAT_LEN
    half = HEAD_DIM // 2
    inv_freq = ROPE_THETA ** (-np.arange(0, half, 2, dtype=np.float64) / half)
    t = np.arange(LAT_LEN)
    ang_r = (t // GRID_W).astype(np.float64)[:, None] * inv_freq[None, :]
    ang_c = (t % GRID_W).astype(np.float64)[:, None] * inv_freq[None, :]

    def blk(ang):
        c = np.cos(ang)
        s = np.sin(ang)
        return np.concatenate([c, c], axis=1), np.concatenate([-s, s], axis=1)

    cr, sr = blk(ang_r)
    cc, sc = blk(ang_c)
    cos64 = np.concatenate([cr, cc], axis=1)
    sin64 = np.concatenate([sr, sc], axis=1)
    rope_cos = np.concatenate([cos64, cos64], axis=1).astype(np.float32)
    rope_sin = np.concatenate([sin64, sin64], axis=1).astype(np.float32)
    tri = np.tril(np.ones((TM, TM), np.float32), k=-1)
    below = np.triu(np.ones((LANES, LANES), np.float32), k=1)
    f32 = lambda a: np.asarray(a, np.float32)
    return dict(dft_ch=f32(dft_ch), dft_ctx=f32(dft_ctx), c256=f32(c256), s256=f32(s256),
                tw_cos=f32(np.cos(tw)), tw_sin=f32(np.sin(tw)),
                rope_cos=rope_cos, rope_sin=rope_sin, tri=tri, below=below)


def _device_tables():
    tb = {k: jnp.asarray(v) for k, v in _tables().items()}
    for k in ("dft_ch", "dft_ctx", "c256", "s256", "tri", "below"):
        tb[k] = tb[k].astype(BF16)
    return tb


def _adaln_kernel(c_ref, w_ref, b_ref, o_ref):
    s = _silu(c_ref[...]).astype(BF16)
    o_ref[...] = jnp.dot(s, w_ref[...].astype(BF16), preferred_element_type=F32) + b_ref[...]


def _adaln(cvec, ada_w, ada_b):
    tn = 1536
    out = pl.pallas_call(
        _adaln_kernel,
        out_shape=jax.ShapeDtypeStruct((2, 8, 6 * D), F32),
        grid=(2, 6 * D // tn),
        in_specs=[
            pl.BlockSpec((8, D), lambda l, j: (0, 0)),
            pl.BlockSpec((None, D, tn), lambda l, j: (l, 0, j)),
            pl.BlockSpec((None, 1, tn), lambda l, j: (l, 0, j)),
        ],
        out_specs=pl.BlockSpec((None, 8, tn), lambda l, j: (l, 0, j)),
        compiler_params=_params(("arbitrary", "arbitrary")),
        name="adaln",
    )(cvec, ada_w, ada_b.reshape(2, 1, 6 * D))
    return out.reshape(2, 8, 6, D)


def _inproj_kernel(xc_ref, xl_ref, mod_ref, g_ref, win_ref, sgug_ref, sguw_ref, sgub_ref, dft_ref, a_ref, y_ref,
                   y1r_ref, y2r_ref, slab_s):
    is_ctx = pl.program_id(0) < NP // TM
    x = _pick_stream(TM, xc_ref, xl_ref)
    h = _modulate(x, g_ref[...], mod_ref[1:2, :], mod_ref[0:1, :]).astype(BF16)
    p = jnp.dot(h, win_ref[...], preferred_element_type=F32)
    act = _gelu_tanh(p[:, :2 * SGU_W])
    spectra = []
    for g in range(4):
        lo, hi = g * GW, (g + 1) * GW
        u = act[:, lo:hi]
        v = act[:, SGU_W + lo:SGU_W + hi]
        ms = jnp.mean(v * v, axis=-1, keepdims=True)
        vn = (v * lax.rsqrt(ms + EPS) * sgug_ref[:, lo:hi]).astype(BF16)
        w = sguw_ref[g]
        for c in range(TM // CHUNK):
            r0, r1 = c * CHUNK, (c + 1) * CHUNK
            mix = jnp.dot(w, vn[r0:r1, :], preferred_element_type=F32) + sgub_ref[:, lo:hi]
            a_ref[r0:r1, lo:hi] = (u[r0:r1, :] * mix).astype(BF16)
        fg = p[:, 2 * SGU_W + lo:2 * SGU_W + hi].astype(BF16)
        spectra.append(jnp.dot(fg, dft_ref[...], preferred_element_type=F32))

    @pl.when(is_ctx)
    def _():
        for g, yy in enumerate(spectra):
            y_ref[:, g * GW:(g + 1) * GW] = yy[:, :GW].astype(BF16)
            y_ref[:, FNET_W + g * GW:FNET_W + (g + 1) * GW] = yy[:, GW:].astype(BF16)

    @pl.when(jnp.logical_not(is_ctx))
    def _():
        for g, yy in enumerate(spectra):
            for half, dst in ((0, y1r_ref), (1, y2r_ref)):
                slab = slab_s.at[2 * g + half]
                slab[...] = yy[:, half * GW:(half + 1) * GW]
                for n2 in range(DFT_N2):
                    cols = slice(n2 * FNET_W + g * GW, n2 * FNET_W + (g + 1) * GW)
                    dst[:, cols] = slab[pl.ds(n2, TM // DFT_N2, stride=DFT_N2), :].astype(BF16)


def _inproj(x_ctx, x_lat, mod, g, w_in, sgu_g, sgu_w, sgu_b, tb):
    ctx, lat = _two_stream_specs(TM, D)
    npt = NP // TM
    rows_r = TM // DFT_N2
    yr_shape = jax.ShapeDtypeStruct((NS // DFT_N2, DFT_N2 * FNET_W), BF16)
    yr_spec = pl.BlockSpec((rows_r, DFT_N2 * FNET_W), lambda i: (jnp.maximum(i - npt, 0), 0))
    return pl.pallas_call(
        _inproj_kernel,
        out_shape=(jax.ShapeDtypeStruct((T, SGU_W), BF16), jax.ShapeDtypeStruct((NP, 2 * FNET_W), BF16),
                   yr_shape, yr_shape),
        grid=(T // TM,),
        in_specs=[
            ctx, lat,
            _mod_spec(0, TM),
            _resident((1, D)),
            _resident((D, 3 * SGU_W)),
            _resident((1, SGU_W)),
            _resident((4, CHUNK, CHUNK)),
            _resident((CHUNK, SGU_W)),
            _resident((GW, 2 * GW)),
        ],
        out_specs=(pl.BlockSpec((TM, SGU_W), lambda i: (i, 0)),
                   pl.BlockSpec((TM, 2 * FNET_W), lambda i: (jnp.minimum(i, npt - 1), 0)), yr_spec, yr_spec),
        scratch_shapes=[pltpu.VMEM((8, TM, LANES), F32)],
        compiler_params=_params(("arbitrary",)),
        name="inproj_mix",
    )(x_ctx, x_lat, mod, g, w_in, sgu_g, sgu_w, sgu_b, tb["dft_ch"])


def _mix_out(f, a_ref, x_ref, mod_ref, wout_ref, o_ref):
    mix = (jnp.dot(a_ref[...], wout_ref[:SGU_W, :], preferred_element_type=F32)
           + jnp.dot(f.astype(BF16), wout_ref[SGU_W:, :], preferred_element_type=F32))
    o_ref[...] = x_ref[...] + mod_ref[2:3, :] * mix


def _mixout_ctx_kernel(y_ref, a_ref, x_ref, mod_ref, dft_ref, wout_ref, o_ref):
    fs = []
    for s in range(CTX_PER_STEP):
        rows = slice(s * CTX_LEN, (s + 1) * CTX_LEN)
        fs.append(jnp.dot(dft_ref[:, :CTX_LEN], y_ref[rows, :FNET_W], preferred_element_type=F32)
                  + jnp.dot(dft_ref[:, CTX_LEN:], y_ref[rows, FNET_W:], preferred_element_type=F32))
    f = jnp.concatenate(fs, axis=0) * (1.0 / math.sqrt(CTX_LEN * GW))
    _mix_out(f, a_ref, x_ref, mod_ref, wout_ref, o_ref)


def _mixout_ctx(y, a, x, mod, w_out, tb):
    rows = CTX_PER_STEP * CTX_LEN
    tok = lambda w: pl.BlockSpec((rows, w), lambda i: (i, 0))
    return pl.pallas_call(
        _mixout_ctx_kernel,
        out_shape=jax.ShapeDtypeStruct((NP, D), F32),
        grid=(N_CTX_SEQ // CTX_PER_STEP,),
        in_specs=[tok(2 * FNET_W), tok(SGU_W), tok(D),
                  pl.BlockSpec((None, None, 6, D), lambda i: (0, CTX_ROW, 0, 0)),
                  _resident((CTX_LEN, 2 * CTX_LEN)), _resident((D, D))],
        out_specs=tok(D),
        compiler_params=_params(("arbitrary",)),
        name="mixout_ctx",
    )(y, a, x, mod, tb["dft_ctx"], w_out)


def _dft1_kernel(y1_ref, y2_ref, c_ref, s_ref, twc_ref, tws_ref, br_ref, bi_ref):
    y1 = y1_ref[...]
    y2 = y2_ref[...]
    c = c_ref[...]
    s = s_ref[...]
    ar = jnp.dot(c, y1, preferred_element_type=F32) - jnp.dot(s, y2, preferred_element_type=F32)
    ai = -(jnp.dot(c, y2, preferred_element_type=F32) + jnp.dot(s, y1, preferred_element_type=F32))
    lane = lax.broadcasted_iota(I32, (DFT_N1, DFT_N2), 1)
    for r in range(DFT1_GROUP):
        n2 = pl.program_id(1) * DFT1_GROUP + r
        cols = slice(r * FNET_W, (r + 1) * FNET_W)
        tc = jnp.sum(jnp.where(lane == n2, twc_ref[...], 0.0), axis=-1, keepdims=True)
        ts = jnp.sum(jnp.where(lane == n2, tws_ref[...], 0.0), axis=-1, keepdims=True)
        br_ref[r] = (tc * ar[:, cols] + ts * ai[:, cols]).astype(BF16)
        bi_ref[r] = (tc * ai[:, cols] - ts * ar[:, cols]).astype(BF16)


def _dft1(y1r, y2r, tb):
    shp = jax.ShapeDtypeStruct((N_LAT_SEQ, DFT_N2, DFT_N1, FNET_W), BF16)
    src = pl.BlockSpec((None, DFT_N1, DFT1_GROUP * FNET_W), lambda b, n: (b, 0, n))
    dst = pl.BlockSpec((None, DFT1_GROUP, DFT_N1, FNET_W), lambda b, n: (b, n, 0, 0))
    return pl.pallas_call(
        _dft1_kernel,
        out_shape=(shp, shp),
        grid=(N_LAT_SEQ, DFT_N2 // DFT1_GROUP),
        in_specs=[src, src, _resident((DFT_N1, DFT_N1)), _resident((DFT_N1, DFT_N1)),
                  _resident((DFT_N1, DFT_N2)), _resident((DFT_N1, DFT_N2))],
        out_specs=(dst, dst),
        compiler_params=_params(("arbitrary", "arbitrary")),
        name="dft_stage1",
    )(y1r, y2r, tb["c256"], tb["s256"], tb["tw_cos"], tb["tw_sin"])


def _cmul_const(z, wr, wi):
    re, im = z
    tol = 1e-12
    if abs(wi) < tol:
        return (re, im) if wr > 0 else (-re, -im)
    if abs(wr) < tol:
        return (-im, re) if wi > 0 else (im, -re)
    return (re * wr - im * wi, re * wi + im * wr)


def _fft(zs):
    n = len(zs)
    if n == 1:
        return zs
    even, odd = _fft(zs[0::2]), _fft(zs[1::2])
    out = [None] * n
    for k in range(n // 2):
        ang = 2.0 * math.pi * k / n
        t = _cmul_const(odd[k], math.cos(ang), -math.sin(ang))
        out[k] = (even[k][0] + t[0], even[k][1] + t[1])
        out[k + n // 2] = (even[k][0] - t[0], even[k][1] - t[1])
    return out


def _mixout_lat_kernel(br_ref, bi_ref, a_ref, x_ref, mod_ref, wout_ref, o_ref):
    spec = _fft([(br_ref[n2].astype(F32), bi_ref[n2].astype(F32)) for n2 in range(DFT_N2)])
    f = jnp.concatenate([re for re, _ in spec], axis=0) * (1.0 / math.sqrt(LAT_LEN * GW))
    rows = DFT_N2 * DFT_ROWS
    mix = (jnp.dot(a_ref[...].reshape(rows, SGU_W), wout_ref[:SGU_W, :], preferred_element_type=F32)
           + jnp.dot(f.astype(BF16), wout_ref[SGU_W:, :], preferred_element_type=F32))
    out = x_ref[...].reshape(rows, D) + mod_ref[2:3, :] * mix
    o_ref[...] = out.reshape(DFT_N2, DFT_ROWS, D)


def _mixout_lat(br, bi, a_lat, x, mod, w_out):
    blk = lambda w, off=0: pl.BlockSpec((None, DFT_N2, DFT_ROWS, w), lambda b, r: (b + off, 0, r, 0))
    return pl.pallas_call(
        _mixout_lat_kernel,
        out_shape=jax.ShapeDtypeStruct((N_LAT_SEQ, DFT_N2, DFT_N1, D), F32),
        grid=(N_LAT_SEQ, DFT_N1 // DFT_ROWS),
        in_specs=[blk(FNET_W), blk(FNET_W), blk(SGU_W, NP // LAT_LEN), blk(D),
                  pl.BlockSpec((None, None, 6, D), lambda b, r: (0, b, 0, 0)),
                  _resident((D, D))],
        out_specs=blk(D),
        compiler_params=_params(("arbitrary", "arbitrary")),
        name="mixout_lat",
    )(br, bi, a_lat, x, mod, w_out)


def _ffn_kernel(xc_ref, xl_ref, mod_ref, g_ref, w1_ref, w3_ref, w2_ref, o_ref):
    x = _pick_stream(TM, xc_ref, xl_ref)
    h = _modulate(x, g_ref[...], mod_ref[4:5, :], mod_ref[3:4, :]).astype(BF16)
    acc = None
    for c in range(D_FF // FFN_CHUNK):
        lo, hi = c * FFN_CHUNK, (c + 1) * FFN_CHUNK
        a = jnp.dot(h, w1_ref[:, lo:hi], preferred_element_type=F32)
        b = jnp.dot(h, w3_ref[:, lo:hi], preferred_element_type=F32)
        d = jnp.dot((_silu(a) * b).astype(BF16), w2_ref[lo:hi, :], preferred_element_type=F32)
        acc = d if acc is None else acc + d
    o_ref[...] = x + mod_ref[5:6, :] * acc


def _ffn(x_ctx, x_lat, mod, g, w1, w3, w2):
    ctx, lat = _two_stream_specs(TM, D)
    return pl.pallas_call(
        _ffn_kernel,
        out_shape=jax.ShapeDtypeStruct((T, D), F32),
        grid=(T // TM,),
        in_specs=[
            ctx, lat,
            _mod_spec(0, TM),
            _resident((1, D)),
            _resident((D, D_FF)),
            _resident((D, D_FF)),
            _resident((D_FF, D)),
        ],
        out_specs=pl.BlockSpec((TM, D), lambda i: (i, 0)),
        compiler_params=_params(("arbitrary",)),
        name="ffn",
    )(x_ctx, x_lat, mod, g, w1, w3, w2)


def _qkv_lat_kernel(x_ref, mod_ref, g_ref, w_ref, cos_ref, sin_ref, q_ref, k_ref, v_ref):
    h = _modulate(x_ref[...], g_ref[...], mod_ref[1:2, :], mod_ref[0:1, :]).astype(BF16)
    qkv = jnp.dot(h, w_ref[...], preferred_element_type=F32)
    cos = cos_ref[...]
    sin = sin_ref[...]
    first = (lax.broadcasted_iota(I32, (TM, LANES), 1) & 31) < 16

    def rotate(xh):
        partner = jnp.where(first, pltpu.roll(xh, LANES - 16, 1), pltpu.roll(xh, 16, 1))
        return xh * cos + partner * sin

    for hh in range(N_HEADS):
        sl = slice(hh * LANES, (hh + 1) * LANES)
        q_ref[:, sl] = (rotate(qkv[:, sl]) * Q_SCALE).astype(BF16)
        k_ref[:, sl] = rotate(qkv[:, D + hh * LANES:D + (hh + 1) * LANES]).astype(BF16)
    v_ref[...] = qkv[:, 2 * D:].astype(BF16)


def _qkv_lat(x, mod, g, w_qkv, tb):
    tok = pl.BlockSpec((TM, D), lambda i: (i, 0))
    lshape = jax.ShapeDtypeStruct((NS, D), BF16)
    off = NP // TM
    ltok = pl.BlockSpec((TM, D), lambda i: (i + off, 0))
    rope = pl.BlockSpec((TM, LANES), lambda i: (i % (LAT_LEN // TM), 0))
    return pl.pallas_call(
        _qkv_lat_kernel,
        out_shape=(lshape, lshape, lshape),
        grid=(NS // TM,),
        in_specs=[ltok, _mod_spec(1, TM, off), _resident((1, D)), _resident((D, 3 * D)), rope, rope],
        out_specs=(tok, tok, tok),
        compiler_params=_params(("arbitrary",)),
        name="qkv_lat",
    )(x, mod, g, w_qkv, tb["rope_cos"], tb["rope_sin"])


def _lam(lp_ref):
    lp = lp_ref[...]
    a = jnp.sum(lp[0:1, :] * lp[1:2, :], axis=-1, keepdims=True)
    b = jnp.sum(lp[2:3, :] * lp[3:4, :], axis=-1, keepdims=True)
    return jnp.exp(a) - jnp.exp(b) + LAM0


def _head_norm(o, sg):
    ms = jnp.mean(o * o, axis=-1, keepdims=True)
    return o * lax.rsqrt(ms + EPS) * (sg * (1.0 - LAM0))


_NT = (((1,), (1,)), ((), ()))


def _qkv_attn_ctx_kernel(x_ref, mod_ref, g_ref, w_ref, lp_ref, sg_ref, o_ref, kf_ref, vf_ref):
    lam = _lam(lp_ref)
    map0 = lax.broadcasted_iota(I32, (CTX_LEN, LANES), 1) < HEAD_DIM
    init = _softmax_init(CTX_LEN)
    for pair in range(CTX_ATTN_ROWS // TM):
        rows = slice(pair * TM, (pair + 1) * TM)
        h = _modulate(x_ref[rows, :], g_ref[...], mod_ref[1:2, :], mod_ref[0:1, :]).astype(BF16)
        qkv = jnp.dot(h, w_ref[...], preferred_element_type=F32)
        kf_ref[rows, :] = qkv[:, D:2 * D]
        vf_ref[rows, :] = qkv[:, 2 * D:]
        qb = (qkv[:, :D] * Q_SCALE).astype(BF16)
        kb = qkv[:, D:2 * D].astype(BF16)
        vb = qkv[:, 2 * D:].astype(BF16)
        for s in range(TM // CTX_LEN):
            seq = slice(s * CTX_LEN, (s + 1) * CTX_LEN)
            for hh in range(N_HEADS):
                sl = slice(hh * LANES, (hh + 1) * LANES)
                q = qb[seq, sl]
                zero = jnp.zeros_like(q)
                outs = []
                for qc in (jnp.where(map0, q, zero), jnp.where(map0, zero, q)):
                    _, l, acc = _softmax_step(qc, kb[seq, sl], vb[seq, sl], init)
                    outs.append(acc / jnp.sum(l, axis=-1, keepdims=True))
                o_ref[pair * TM + s * CTX_LEN:pair * TM + (s + 1) * CTX_LEN, sl] = _head_norm(
                    outs[0] - lam * outs[1], sg_ref[...]).astype(BF16)


def _qkv_attn_ctx(x, mod, g, w_qkv, lam_p, subln_g):
    tok = pl.BlockSpec((CTX_ATTN_ROWS, D), lambda i: (i, 0))
    return pl.pallas_call(
        _qkv_attn_ctx_kernel,
        out_shape=(jax.ShapeDtypeStruct((NP, D), BF16), jax.ShapeDtypeStruct((NP, D), F32),
                   jax.ShapeDtypeStruct((NP, D), F32)),
        grid=(NP // CTX_ATTN_ROWS,),
        in_specs=[tok, pl.BlockSpec((None, None, 6, D), lambda i: (1, CTX_ROW, 0, 0)), _resident((1, D)),
                  _resident((D, 3 * D)), _resident((4, HEAD_DIM)), _resident((1, V_DIM))],
        out_specs=(tok, tok, tok),
        compiler_params=_params(("arbitrary",)),
        name="qkv_attn_ctx",
    )(x, mod, g, w_qkv, lam_p, subln_g)


def _softmax_init(rows):
    return (jnp.full((rows, LANES), -jnp.inf, F32), jnp.zeros((rows, LANES), F32), jnp.zeros((rows, LANES), F32))


def _softmax_step(qc, kc, vc, state):
    m, l, acc = state
    s = lax.dot_general(qc, kc, _NT, preferred_element_type=F32)
    blocks = [s[:, j * LANES:(j + 1) * LANES] for j in range(s.shape[1] // LANES)]
    bm = functools.reduce(jnp.maximum, blocks)
    m_new = jnp.maximum(m, jnp.max(bm, axis=-1, keepdims=True))
    alpha = jnp.exp2(m - m_new)
    ps = [jnp.exp2(b - m_new) for b in blocks]
    l_new = alpha * l + functools.reduce(jnp.add, ps)
    p = jnp.concatenate(ps, axis=1).astype(BF16)
    acc_new = alpha * acc + jnp.dot(p, vc, preferred_element_type=F32)
    return m_new, l_new, acc_new


def _attn_lat_kernel(q_ref, k_ref, v_ref, ck_ref, cv_ref, lp_ref, sg_ref, w1_ref, w3_ref, w2_ref,
                     o_ref, w1b_ref, w3b_ref, w2b_ref):
    step = (pl.program_id(0) * N_HEADS + pl.program_id(1)) * (LAT_LEN // TQ) + pl.program_id(2)
    for phase, (src, dst) in enumerate(((w1_ref, w1b_ref), (w3_ref, w3b_ref), (w2_ref, w2b_ref))):
        @pl.when((step >= phase * CAST_BLOCKS) & (step < (phase + 1) * CAST_BLOCKS))
        def _(src=src, dst=dst):
            dst[...] = src[...].astype(BF16)

    q = q_ref[...]
    zero = jnp.zeros_like(q)
    map0 = lax.broadcasted_iota(I32, (TQ, LANES), 1) < HEAD_DIM
    qs = (jnp.where(map0, q, zero), jnp.where(map0, zero, q))
    init = _softmax_init(TQ)
    states = [init, init]
    head_rows = pl.ds(pl.program_id(1), CTX_LEN, stride=N_HEADS)
    chunks = [(ck_ref[head_rows, :].astype(BF16), cv_ref[head_rows, :].astype(BF16))]
    chunks += [(k_ref[j * TK:(j + 1) * TK, :], v_ref[j * TK:(j + 1) * TK, :]) for j in range(LAT_LEN // TK)]
    for kc, vc in chunks:
        states = [_softmax_step(qs[c], kc, vc, states[c]) for c in range(2)]
    outs = [acc / jnp.sum(l, axis=-1, keepdims=True) for _, l, acc in states]
    o = outs[0] - _lam(lp_ref) * outs[1]
    o_ref[...] = _head_norm(o, sg_ref[...]).astype(BF16)


def _attn_lat(q, k, v, cache_k, cache_v, lam_p, subln_g, moe_w1, moe_w3, moe_w2):
    nq = LAT_LEN // TQ
    assert N_LAT_SEQ * N_HEADS * nq >= 3 * CAST_BLOCKS
    qspec = pl.BlockSpec((TQ, LANES), lambda b, h, i: (b * nq + i, h))
    kspec = pl.BlockSpec((LAT_LEN, LANES), lambda b, h, i: (b, h))
    cspec = pl.BlockSpec((None, CTX_LEN * N_HEADS, LANES), lambda b, h, i: (b, 0, 0))

    def cast_block(phase, b, h, i):
        j = jnp.clip((b * N_HEADS + h) * nq + i - phase * CAST_BLOCKS, 0, CAST_BLOCKS - 1)
        return j // CAST_SPLIT, j % CAST_SPLIT

    def up_spec(phase):
        return pl.BlockSpec((None, D, CAST_COLS), lambda b, h, i: (cast_block(phase, b, h, i)[0], 0,
                                                                    cast_block(phase, b, h, i)[1]))

    down_spec = pl.BlockSpec((None, CAST_COLS, D), lambda b, h, i: (*cast_block(2, b, h, i), 0))
    up_shape = jax.ShapeDtypeStruct((N_EXPERTS, D, D_FF_EXPERT), BF16)
    return pl.pallas_call(
        _attn_lat_kernel,
        out_shape=(jax.ShapeDtypeStruct((NS, D), BF16), up_shape, up_shape,
                   jax.ShapeDtypeStruct((N_EXPERTS, D_FF_EXPERT, D), BF16)),
        grid=(N_LAT_SEQ, N_HEADS, nq),
        in_specs=[qspec, kspec, kspec, cspec, cspec, _resident((4, HEAD_DIM)), _resident((1, V_DIM)),
                  up_spec(0), up_spec(1), down_spec],
        out_specs=(qspec, up_spec(0), up_spec(1), down_spec),
        compiler_params=_params(("arbitrary", "arbitrary", "arbitrary")),
        name="attn_lat",
    )(q, k, v, cache_k, cache_v, lam_p, subln_g, moe_w1, moe_w3, moe_w2)


def _oproj_route_kernel(oc_ref, ol_ref, x_ref, mod_ref, wo_ref, g_ref, wr_ref, tri_ref, below_ref, x3_ref, h_ref,
                        route_ref, cnt_ref):
    is_ctx = pl.program_id(0) < NP // (ROUTE_SUB * TM)
    for sub in range(ROUTE_SUB):
        rows = slice(sub * TM, (sub + 1) * TM)
        o = jnp.where(is_ctx, oc_ref[rows, :], ol_ref[rows, :])
        _route_tile(o, x_ref[rows, :], mod_ref, wo_ref, g_ref, wr_ref, tri_ref, below_ref,
                    x3_ref.at[rows, :], h_ref.at[rows, :], route_ref.at[rows, :], cnt_ref.at[sub])


def _route_tile(o, x, mod_ref, wo_ref, g_ref, wr_ref, tri_ref, below_ref, x3_ref, h_ref, route_ref, cnt_ref):
    x3 = x + mod_ref[2:3, :] * jnp.dot(o, wo_ref[...], preferred_element_type=F32)
    x3_ref[...] = x3
    hb = _modulate(x3, g_ref[...], mod_ref[4:5, :], mod_ref[3:4, :]).astype(BF16)
    h_ref[...] = hb

    logits = jnp.dot(hb, wr_ref[...], preferred_element_type=F32)
    lane = lax.broadcasted_iota(I32, (TM, LANES), 1).astype(F32)
    neg = jnp.float32(-jnp.inf)
    lg = jnp.where(lane < N_EXPERTS, logits, neg)
    v1 = jnp.max(lg, axis=-1, keepdims=True)
    i1 = jnp.min(jnp.where(lg == v1, lane, float(LANES)), axis=-1, keepdims=True)
    lg2 = jnp.where(lane == i1, neg, lg)
    v2 = jnp.max(lg2, axis=-1, keepdims=True)
    i2 = jnp.min(jnp.where(lg2 == v2, lane, float(LANES)), axis=-1, keepdims=True)
    e = jnp.exp(v2 - v1)
    g1 = 1.0 / (1.0 + e)
    g2 = e / (1.0 + e)
    sel1 = lane == i1
    sel2 = lane == i2
    onehot = jnp.where(sel1 | sel2, 1.0, 0.0)
    rank = jnp.dot(tri_ref[...], onehot.astype(BF16), preferred_element_type=F32)
    cnt8 = jnp.floor((jnp.sum(onehot, axis=0, keepdims=True) + (SEG_ALIGN - 1)) * (1.0 / SEG_ALIGN))
    base8 = jnp.dot(jnp.broadcast_to(cnt8, (8, LANES)).astype(BF16), below_ref[...], preferred_element_type=F32)
    slot = base8[0:1, :] * float(SEG_ALIGN) + rank
    pos1 = jnp.sum(jnp.where(sel1, slot, 0.0), axis=-1, keepdims=True)
    pos2 = jnp.sum(jnp.where(sel2, slot, 0.0), axis=-1, keepdims=True)
    cnt_ref[...] = cnt8
    route = jnp.where(lane == 0, i1, 0.0)
    route = jnp.where(lane == 1, i2, route)
    route = jnp.where(lane == 2, g1, route)
    route = jnp.where(lane == 3, g2, route)
    route = jnp.where(lane == 4, pos1, route)
    route = jnp.where(lane == 5, pos2, route)
    route_ref[...] = route


def _oproj_route(o_ctx, o_lat, x, mod, w_o, g, w_r, tb):
    rows = ROUTE_SUB * TM
    tok = pl.BlockSpec((rows, D), lambda i: (i, 0))
    ctx, lat = _two_stream_specs(rows, D)
    return pl.pallas_call(
        _oproj_route_kernel,
        out_shape=(jax.ShapeDtypeStruct((T, D), F32), jax.ShapeDtypeStruct((T, D), BF16),
                   jax.ShapeDtypeStruct((T, LANES), F32), jax.ShapeDtypeStruct((T // TM, 1, LANES), F32)),
        grid=(T // rows,),
        in_specs=[ctx, lat, tok, _mod_spec(1, rows), _resident((D, D)), _resident((1, D)), _resident((D, LANES)),
                  _resident((TM, TM)), _resident((LANES, LANES))],
        out_specs=(tok, tok, pl.BlockSpec((rows, LANES), lambda i: (i, 0)),
                   pl.BlockSpec((ROUTE_SUB, 1, LANES), lambda i: (i, 0, 0))),
        compiler_params=_params(("arbitrary",)),
        name="oproj_route",
    )(o_ctx, o_lat, x, mod, w_o, g, w_r, tb["tri"], tb["below"])


def _bit_copies(n, src0, dst0, bits, make_copy, op):
    for bit in bits:
        done = n & (-2 * bit)

        @pl.when((n & bit) != 0)
        def _(done=done, bit=bit):
            src = pl.multiple_of((src0 + done) * SEG_ALIGN, SEG_ALIGN)
            dst = pl.multiple_of((dst0 + done) * SEG_ALIGN, SEG_ALIGN)
            op(make_copy(src, dst, bit * SEG_ALIGN))


def _segment_copies(tab, make_copy, op):
    for e in range(N_EXPERTS):
        _bit_copies(tab(e), tab(N_EXPERTS + e), tab(2 * N_EXPERTS + e), SEG_BITS, make_copy, op)


def _start(copy):
    copy.start()


def _wait(copy):
    copy.wait()


def _dispatch_kernel(tab_ref, fill_ref, h_ref, route_ref, xs_ref, sorted_s, zero_s, sem, fill_sem):
    i = pl.program_id(0)
    slot = i % 2
    slots = route_ref[...].T
    row = lax.broadcasted_iota(I32, (L_TILE, TM), 0).astype(F32)
    perm = jnp.where((row == slots[4:5, :]) | (row == slots[5:6, :]), 1.0, 0.0).astype(BF16)
    sorted_s[slot] = jnp.dot(perm, h_ref[...], preferred_element_type=F32)

    def copies(tile, s, op):
        def seg(src, dst, rows):
            return pltpu.make_async_copy(sorted_s.at[s, pl.ds(src, rows), :], xs_ref.at[pl.ds(dst, rows), :], sem.at[s])

        _segment_copies(lambda j: tab_ref[tile, j], seg, op)

    copies(i, slot, _start)

    @pl.when(i > 0)
    def _():
        copies(i - 1, 1 - slot, _wait)

    @pl.when(i == pl.num_programs(0) - 1)
    def _():
        copies(i, slot, _wait)

    @pl.when(i == 0)
    def _():
        zero_s[...] = jnp.zeros(zero_s.shape, F32)

        def fill(src, dst, rows):
            del src
            return pltpu.make_async_copy(zero_s.at[pl.ds(0, rows), :], xs_ref.at[pl.ds(dst, rows), :], fill_sem)

        def fills(op):
            for e in range(N_EXPERTS):
                _bit_copies(fill_ref[0, N_EXPERTS + e], 0, fill_ref[0, e], SEG_BITS[1:], fill, op)
            for k in range(MAX_UNUSED_TILES):
                @pl.when(k < fill_ref[0, 2 * N_EXPERTS + 1])
                def _(k=k):
                    dst = pl.multiple_of(fill_ref[0, 2 * N_EXPERTS] * SEG_ALIGN + k * T_MOE, SEG_ALIGN)
                    op(fill(0, dst, T_MOE))

        fills(_start)
        fills(_wait)


def _dispatch(tab, fill, h, route):
    smem = pl.BlockSpec(memory_space=pltpu.SMEM)
    return pl.pallas_call(
        _dispatch_kernel,
        out_shape=jax.ShapeDtypeStruct((N_ROWS, D), F32),
        grid=(T // TM,),
        in_specs=[smem, smem, pl.BlockSpec((TM, D), lambda i: (i, 0)), pl.BlockSpec((TM, LANES), lambda i: (i, 0))],
        out_specs=pl.BlockSpec(memory_space=pl.ANY),
        scratch_shapes=[pltpu.VMEM((2, L_TILE, D), F32), pltpu.VMEM((T_MOE, D), F32),
                        pltpu.SemaphoreType.DMA((2,)), pltpu.SemaphoreType.DMA(())],
        compiler_params=_params(("arbitrary",)),
        name="moe_dispatch",
    )(tab, fill, h, route)


def _moe_kernel(te_ref, tv_ref, xs_ref, w1_ref, w3_ref, w2_ref, y_ref, xb_s, acc_s):
    del te_ref
    i = pl.program_id(0)
    f = pl.program_id(1)
    valid = tv_ref[i] == 1

    def chunk(x):
        a = jnp.dot(x, w1_ref[...], preferred_element_type=F32)
        b = jnp.dot(x, w3_ref[...], preferred_element_type=F32)
        return jnp.dot((_silu(a) * b).astype(BF16), w2_ref[...], preferred_element_type=F32)

    @pl.when(valid & (f == 0))
    def _():
        x = xs_ref[...].astype(BF16)
        xb_s[...] = x
        acc_s[...] = chunk(x)

    @pl.when(valid & (f == N_F_MOE - 1))
    def _():
        y_ref[...] = acc_s[...] + chunk(xb_s[...])

    @pl.when(jnp.logical_not(valid) & (f == N_F_MOE - 1))
    def _():
        y_ref[...] = jnp.zeros(y_ref.shape, F32)


def _moe(tile_expert, tile_valid, xs, w1, w3, w2):
    def fidx(i, f, te, tv):
        return jnp.where(tv[i] == 1, f, N_F_MOE - 1)

    grid_spec = pltpu.PrefetchScalarGridSpec(
        num_scalar_prefetch=2,
        grid=(N_MOE_TILES, N_F_MOE),
        in_specs=[
            pl.BlockSpec((T_MOE, D), lambda i, f, te, tv: (i, 0)),
            pl.BlockSpec((None, D, F_MOE), lambda i, f, te, tv: (te[i], 0, fidx(i, f, te, tv))),
            pl.BlockSpec((None, D, F_MOE), lambda i, f, te, tv: (te[i], 0, fidx(i, f, te, tv))),
            pl.BlockSpec((None, F_MOE, D), lambda i, f, te, tv: (te[i], fidx(i, f, te, tv), 0)),
        ],
        out_specs=pl.BlockSpec((T_MOE, D), lambda i, f, te, tv: (i, 0)),
        scratch_shapes=[pltpu.VMEM((T_MOE, D), BF16), pltpu.VMEM((T_MOE, D), F32)],
    )
    return pl.pallas_call(
        _moe_kernel,
        out_shape=jax.ShapeDtypeStruct((N_ROWS, D), F32),
        grid_spec=grid_spec,
        compiler_params=_params(("arbitrary", "arbitrary")),
        name="moe_experts",
    )(tile_expert, tile_valid, xs, w1, w3, w2)


def _combine_kernel(tab_ref, y_ref, x_ref, route_ref, mod_ref, fg_ref, oc_ref, ol_ref, buf, sem, *, first_tile):
    i = pl.program_id(0)
    slot = i % 2

    def fetch(tile, s, op):
        def seg(src, dst, rows):
            return pltpu.make_async_copy(y_ref.at[pl.ds(dst, rows), :], buf.at[s, pl.ds(src, rows), :], sem.at[s])

        _segment_copies(lambda j: tab_ref[tile, j], seg, op)

    @pl.when(i == 0)
    def _():
        buf[...] = jnp.zeros(buf.shape, F32)
        fetch(first_tile, 0, _start)

    @pl.when(i + 1 < pl.num_programs(0))
    def _():
        fetch(first_tile + i + 1, 1 - slot, _start)

    fetch(first_tile + i, slot, _wait)
    yb = buf[slot].astype(BF16)
    route = route_ref[...]
    col = lax.broadcasted_iota(I32, (TM, L_TILE), 1).astype(F32)
    picks = [jnp.dot(jnp.where(col == route[:, c:c + 1], 1.0, 0.0).astype(BF16), yb, preferred_element_type=F32)
             for c in (4, 5)]
    moe = route[:, 2:3] * picks[0] + route[:, 3:4] * picks[1]
    x = x_ref[...] + mod_ref[5:6, :] * moe
    ms = jnp.mean(x * x, axis=-1, keepdims=True)
    res = x * lax.rsqrt(ms + EPS) * fg_ref[...]
    is_ctx = i < NP // TM

    @pl.when(is_ctx)
    def _():
        oc_ref[...] = res

    @pl.when(jnp.logical_not(is_ctx))
    def _():
        ol_ref[...] = res


def _combine(tab, y, x, route, mod, final_g):
    tok = lambda w: pl.BlockSpec((TM, w), lambda i: (i, 0))
    npt = NP // TM
    return pl.pallas_call(
        functools.partial(_combine_kernel, first_tile=0),
        out_shape=(jax.ShapeDtypeStruct((NP, D), F32), jax.ShapeDtypeStruct((NS, D), F32)),
        grid=(T // TM,),
        in_specs=[
            pl.BlockSpec(memory_space=pltpu.SMEM),
            pl.BlockSpec(memory_space=pl.ANY),
            tok(D), tok(LANES), _mod_spec(1, TM), _resident((1, D)),
        ],
        out_specs=(pl.BlockSpec((TM, D), lambda i: (jnp.minimum(i, npt - 1), 0)),
                   pl.BlockSpec((TM, D), lambda i: (jnp.maximum(i - npt, 0), 0))),
        scratch_shapes=[pltpu.VMEM((2, L_TILE, D), F32), pltpu.SemaphoreType.DMA((2,))],
        compiler_params=_params(("arbitrary",)),
        name="moe_combine",
    )(tab, y, x, route, mod, final_g)


def kernel(x_prompt, x_sample, cache_k, cache_v, c, c_ctx, ada_w, ada_b, norm1_g, norm2_g, final_g, mix_w_in, sgu_g, sgu_w, sgu_b, mix_w_out, ffn_w1, ffn_w3, ffn_w2, attn_w_qkv, lam_q1, lam_k1, lam_q2, lam_k2, subln_g, attn_w_o, router_w, moe_w1, moe_w3, moe_w2):
    tb = _device_tables()
    xc = x_prompt.reshape(NP, D)
    xl = x_sample.reshape(NS, D)
    cvec = jnp.concatenate([c, c_ctx[None, :], jnp.zeros((3, D), F32)], axis=0)
    mod = _adaln(cvec, ada_w, ada_b)

    bias = jnp.repeat(sgu_b[0].T, GW, axis=1)
    a, y, y1r, y2r = _inproj(xc, xl, mod, norm1_g[0][None, :], mix_w_in[0].astype(BF16), sgu_g[0].reshape(1, SGU_W),
                             sgu_w[0].astype(BF16), bias, tb)
    w_out = mix_w_out[0].astype(BF16)
    x1c = _mixout_ctx(y, a, xc, mod, w_out, tb)
    br, bi = _dft1(y1r.reshape(N_LAT_SEQ, DFT_N1, DFT_N2 * FNET_W), y2r.reshape(N_LAT_SEQ, DFT_N1, DFT_N2 * FNET_W), tb)
    x1l = _mixout_lat(br, bi, a.reshape(T // LAT_LEN, DFT_N2, DFT_N1, SGU_W),
                      xl.reshape(N_LAT_SEQ, DFT_N2, DFT_N1, D), mod, w_out).reshape(NS, D)
    x2 = _ffn(x1c, x1l, mod, norm2_g[0][None, :], ffn_w1[0].astype(BF16), ffn_w3[0].astype(BF16), ffn_w2[0].astype(BF16))

    w_qkv = attn_w_qkv[0].astype(BF16)
    lam_p = jnp.stack([lam_q1[0], lam_k1[0], lam_q2[0], lam_k2[0]], axis=0)
    sg = subln_g[0][None, :]
    oc, kf, vf = _qkv_attn_ctx(x2, mod, norm1_g[1][None, :], w_qkv, lam_p, sg)
    qkv_l = _qkv_lat(x2, mod, norm1_g[1][None, :], w_qkv, tb)
    cache_rows = (N_LAT_SEQ, CTX_LEN * N_HEADS, LANES)
    ol, w1b, w3b, w2b = _attn_lat(*qkv_l, cache_k.reshape(cache_rows), cache_v.reshape(cache_rows), lam_p, sg,
                                  moe_w1[0], moe_w3[0], moe_w2[0])

    w_r = jnp.pad(router_w[0], ((0, 0), (0, LANES - N_EXPERTS))).astype(BF16)
    x3, h2, route, cnt = _oproj_route(oc, ol, x2, mod, attn_w_o[0].astype(BF16), norm2_g[1][None, :], w_r, tb)
    unit_per_tile = T_MOE // SEG_ALIGN
    seg_len = cnt[:, 0, :N_EXPERTS].astype(I32)
    seg_local = jnp.cumsum(seg_len, axis=1) - seg_len
    group_len = jnp.sum(seg_len, axis=0)
    group_tiles = (group_len + unit_per_tile - 1) // unit_per_tile
    tile_ends = jnp.cumsum(group_tiles)
    group_start = (tile_ends - group_tiles) * unit_per_tile
    seg_global = group_start[None, :] + jnp.cumsum(seg_len, axis=0) - seg_len
    tab = jnp.concatenate([seg_len, seg_local, seg_global], axis=1)
    tab = jnp.pad(tab, ((0, 0), (0, LANES - 3 * N_EXPERTS)))
    fill = jnp.concatenate([group_start + group_len, group_tiles * unit_per_tile - group_len,
                            tile_ends[-1:] * unit_per_tile, N_MOE_TILES - tile_ends[-1:]])
    fill = jnp.pad(fill, (0, LANES - 2 * N_EXPERTS - 2)).reshape(1, LANES)
    tile_idx = jnp.arange(N_MOE_TILES, dtype=I32)
    tile_valid = (tile_idx < tile_ends[-1]).astype(I32)
    tile_expert = jnp.sum((jnp.minimum(tile_idx, tile_ends[-1] - 1)[:, None] >= tile_ends[None, :]).astype(I32), axis=-1)

    xs = _dispatch(tab, fill, h2, route)
    ys = _moe(tile_expert, tile_valid, xs, w1b, w3b, w2b)
    fg = final_g[None, :]
    y_prompt, y_sample = _combine(tab, ys, x3, route, mod, fg)

    return (y_prompt.reshape(N_CTX_SEQ, CTX_LEN, D), y_sample.reshape(N_LAT_SEQ, LAT_LEN, D),
            kf.reshape(N_CTX_SEQ, 1, CTX_LEN, N_HEADS, 2 * HEAD_DIM), vf.reshape(N_CTX_SEQ, 1, CTX_LEN, N_HEADS, V_DIM))
```
